```python
import math
import jax
import jax.numpy as jnp
from jax import lax

D_MODEL = 4096
BATCH = 4
SEQ = 4096
DEPTH = 4

CHUNK = 64
N_MIXERS = 3
N_GLA = (DEPTH + 2) // 3
N_RWKV = (DEPTH + 1) // 3
N_S5 = DEPTH // 3
NORM_EPS = 1e-6

GLA_HEADS = 4
GLA_DK = D_MODEL // 2
GLA_DV = D_MODEL
GLA_DKH = GLA_DK // GLA_HEADS
GLA_DVH = GLA_DV // GLA_HEADS
GLA_GATE_RANK = 16
GLA_GATE_NORMALIZER = 16.0
GLA_IN_WIDTH = 2 * GLA_DK + 2 * GLA_DV + GLA_GATE_RANK

RWKV_HEAD = 64
RWKV_HEADS = D_MODEL // RWKV_HEAD
RWKV_DECAY_RANK = max(32, int(round(1.8 * D_MODEL ** 0.5 / 32)) * 32)
RWKV_AAA_RANK = max(32, int(round(1.8 * D_MODEL ** 0.5 / 32)) * 32)
RWKV_GATE_RANK = max(32, int(round(0.6 * D_MODEL ** 0.8 / 32)) * 32)
RWKV_LNX_EPS = 64e-5

S5_GROUP = 16
S5_GROUPS = D_MODEL // S5_GROUP
S5_STATE = 64
S5_DT_MIN = 1e-3
S5_DT_MAX = 1e-1

N_MEM = 256
XA_HEADS = 4
XA_HEAD_DIM = 128
XA_WIDTH = XA_HEADS * XA_HEAD_DIM

D_FF = 4 * D_MODEL

kernel_name = 'hybrid_gla_rwkv7_s5_stream_encoder'


def rms_norm(x, gain, eps=NORM_EPS):
    xf = x.astype(jnp.float32)
    y = xf * lax.rsqrt(jnp.mean(xf * xf, axis=-1, keepdims=True) + eps)
    return (y * gain.astype(jnp.float32)).astype(x.dtype)


def gla_mixer(h, w_in, w_gk_up, b_gk_up, o_norm, w_out):
    bsz, seq, _ = h.shape
    n_chunks = seq // CHUNK
    proj = h @ w_in
    q, k, v, g, gk_lo = jnp.split(
        proj, [GLA_DK, 2 * GLA_DK, 2 * GLA_DK + GLA_DV, 2 * GLA_DK + 2 * GLA_DV], axis=-1)
    log_alpha = jax.nn.log_sigmoid(
        (gk_lo @ w_gk_up + b_gk_up).astype(jnp.float32)) / GLA_GATE_NORMALIZER

    def to_chunks(t, hd):
        return t.reshape(bsz, n_chunks, CHUNK, GLA_HEADS, hd).transpose(1, 0, 3, 2, 4)

    qc = to_chunks(q * GLA_DKH ** -0.5, GLA_DKH)
    kc = to_chunks(k, GLA_DKH)
    vc = to_chunks(v, GLA_DVH)
    cum = jnp.cumsum(to_chunks(log_alpha, GLA_DKH), axis=3)
    cum_last = cum[:, :, :, -1:, :]
    k_dec = kc * jnp.exp(cum_last - cum)
    chunk_decay = jnp.exp(cum_last[:, :, :, 0, :])

    def chunk_step(state, inp):
        q_i, k_i, v_i, d_i = inp
        state = state * d_i[..., None] + jnp.einsum('bhck,bhcv->bhkv', k_i, v_i)
        return state, jnp.einsum('bhck,bhkv->bhcv', q_i, state)

    state0 = jnp.zeros((bsz, GLA_HEADS, GLA_DKH, GLA_DVH), jnp.float32)
    _, o = lax.scan(chunk_step, state0, (qc, k_dec, vc, chunk_decay))
    o = o.transpose(1, 0, 3, 2, 4).reshape(bsz, seq, GLA_HEADS, GLA_DVH)
    o = rms_norm(o, o_norm).reshape(bsz, seq, GLA_DV) * jax.nn.silu(g)
    return (o @ w_out).astype(h.dtype)


def rwkv7_mixer(h, mix, w_r, w_k, w_v, w_o, w0, w_w1, w_w2, a0, w_a1, w_a2,
                w_g1, w_g2, k_k, k_a, r_k, lnx_g, lnx_b):
    bsz, seq, _ = h.shape
    f32 = jnp.float32
    xx = jnp.pad(h, ((0, 0), (1, 0), (0, 0)))[:, :-1] - h
    xr, xw, xk, xv, xa, xg = (h + xx * mix[j] for j in range(6))
    r = xr @ w_r
    k = xk @ w_k
    v = xv @ w_v
    w_log = -jax.nn.softplus(-(w0 + jnp.tanh(xw @ w_w1) @ w_w2).astype(f32)) - 0.5
    decay = jnp.exp(-jnp.exp(w_log))
    a = jax.nn.sigmoid(a0 + (xa @ w_a1) @ w_a2)
    g = jax.nn.sigmoid(xg @ w_g1) @ w_g2

    def heads(t):
        return t.reshape(bsz, seq, RWKV_HEADS, RWKV_HEAD)

    kk = heads(k * k_k).astype(f32)
    kk = kk / jnp.maximum(jnp.sqrt(jnp.sum(kk * kk, axis=-1, keepdims=True)), 1e-12)
    k = k * (1 + (a - 1) * k_a)

    def time_major(t):
        return heads(t).transpose(1, 0, 2, 3)

    def step(state, inp):
        r_t, w_t, k_t, v_t, kk_t, a_t = inp
        sa = jnp.einsum('bhij,bhj->bhi', state, -kk_t)
        state = (state * w_t[:, :, None, :]
                 + sa[..., None] * (kk_t * a_t)[:, :, None, :]
                 + v_t[..., None] * k_t[:, :, None, :])
        return state, jnp.einsum('bhij,bhj->bhi', state, r_t)

    state0 = jnp.zeros((bsz, RWKV_HEADS, RWKV_HEAD, RWKV_HEAD), f32)
    _, y = lax.scan(step, state0, (time_major(r), time_major(decay), time_major(k),
                                   time_major(v), kk.transpose(1, 0, 2, 3), time_major(a)))
    y = y.transpose(1, 0, 2, 3)
    mu = jnp.mean(y, axis=-1, keepdims=True)
    var = jnp.mean(jnp.square(y - mu), axis=-1, keepdims=True)
    y = ((y - mu) * lax.rsqrt(var + RWKV_LNX_EPS)).reshape(bsz, seq, D_MODEL) * lnx_g + lnx_b
    bonus = jnp.sum(heads(r) * heads(k) * r_k.reshape(RWKV_HEADS, RWKV_HEAD),
                    axis=-1, keepdims=True) * heads(v)
    out = (y + bonus.reshape(bsz, seq, D_MODEL)) * g
    return (out @ w_o).astype(h.dtype)


def s5_combine(earlier, later):
    a1r, a1i, b1r, b1i = earlier
    a2r, a2i, b2r, b2i = later
    return (a2r * a1r - a2i * a1i, a2r * a1i + a2i * a1r,
            a2r * b1r - a2i * b1i + b2r, a2r * b1i + a2i * b1r + b2i)


def s5_mixer(h, lam_re, lam_im, log_dt, b_re, b_im, c_re, c_im, d_skip, w_glu1, w_glu2):
    bsz, seq, _ = h.shape
    f32 = jnp.float32
    n_chunks = seq // CHUNK
    lr = jnp.minimum(lam_re.astype(f32), -1e-4)
    li = lam_im.astype(f32)
    dt = jnp.exp(log_dt.astype(f32))[:, None]
    mag = jnp.exp(lr * dt)
    ab_re = mag * jnp.cos(li * dt)
    ab_im = mag * jnp.sin(li * dt)
    den = lr * lr + li * li
    nr, ni = ab_re - 1.0, ab_im
    z_re = (nr * lr + ni * li) / den
    z_im = (ni * lr - nr * li) / den
    bb_re = z_re[..., None] * b_re.astype(f32) - z_im[..., None] * b_im.astype(f32)
    bb_im = z_re[..., None] * b_im.astype(f32) + z_im[..., None] * b_re.astype(f32)
    cr = c_re.astype(f32)
    ci = c_im.astype(f32)
    a_re = jnp.broadcast_to(ab_re, (bsz, CHUNK, S5_GROUPS, S5_STATE))
    a_im = jnp.broadcast_to(ab_im, (bsz, CHUNK, S5_GROUPS, S5_STATE))
    u = h.reshape(bsz, n_chunks, CHUNK, S5_GROUPS, S5_GROUP).transpose(1, 0, 2, 3, 4).astype(f32)

    def chunk_step(carry, u_c):
        s_re, s_im = carry
        bu_re = jnp.einsum('bcgh,gph->bcgp', u_c, bb_re)
        bu_im = jnp.einsum('bcgh,gph->bcgp', u_c, bb_im)
        pr, pi, xr, xi = lax.associative_scan(s5_combine, (a_re, a_im, bu_re, bu_im), axis=1)
        xr = xr + pr * s_re[:, None] - pi * s_im[:, None]
        xi = xi + pr * s_im[:, None] + pi * s_re[:, None]
        y = jnp.einsum('gqp,bcgp->bcgq', cr, xr) - jnp.einsum('gqp,bcgp->bcgq', ci, xi)
        return (xr[:, -1], xi[:, -1]), y

    zeros = jnp.zeros((bsz, S5_GROUPS, S5_STATE), f32)
    _, y = lax.scan(chunk_step, (zeros, zeros), u)
    y = y.transpose(1, 0, 2, 3, 4).reshape(bsz, seq, D_MODEL) + d_skip * h
    y = jax.nn.gelu(y, approximate=False)
    return ((y @ w_glu1) * jax.nn.sigmoid(y @ w_glu2)).astype(h.dtype)


def cross_attention(h, mem_k, mem_v, wq, wo):
    bsz, seq, _ = h.shape
    q = (h @ wq).reshape(bsz, seq, XA_HEADS, XA_HEAD_DIM)
    s = jnp.einsum('bthd,bmhd->bhtm', q, mem_k).astype(jnp.float32) * XA_HEAD_DIM ** -0.5
    p = jax.nn.softmax(s, axis=-1).astype(h.dtype)
    o = jnp.einsum('bhtm,bmhd->bthd', p, mem_v).reshape(bsz, seq, XA_WIDTH)
    return o @ wo


def sq_relu_mlp(h, w1, w2):
    return jnp.square(jax.nn.relu(h @ w1)) @ w2


def setup_inputs(seed: int = 0) -> dict:
    key = jax.random.key(seed)
    keys = iter(jax.random.split(key, 64))
    D = D_MODEL

    def nrm(shape, scale):
        return scale * jax.random.normal(next(keys), shape, jnp.float32)

    def unif(shape, lo, hi):
        return jax.random.uniform(next(keys), shape, jnp.float32, lo, hi)

    def gain(shape):
        return 1.0 + nrm(shape, 0.02)

    n_idx = jnp.arange(S5_STATE, dtype=jnp.float32)
    return {
        'x': nrm((BATCH, SEQ, D), 1.0),
        'mem': nrm((BATCH, N_MEM, D), 1.0),
        'norm_gains': gain((DEPTH, 6, D)),
        'mem_norm': gain((D,)),
        'mem_w_kv': nrm((D, 2 * XA_WIDTH), D ** -0.5),
        'xa_wq': nrm((DEPTH, D, XA_WIDTH), D ** -0.5),
        'xa_wo': nrm((DEPTH, XA_WIDTH, D), XA_WIDTH ** -0.5),
        'mlp_w1': nrm((DEPTH, D, D_FF), D ** -0.5),
        'mlp_w2': nrm((DEPTH, D_FF, D), D_FF ** -0.5),
        'gla_w_in': nrm((N_GLA, D, GLA_IN_WIDTH), D ** -0.5),
        'gla_w_gk_up': nrm((N_GLA, GLA_GATE_RANK, GLA_DK), GLA_GATE_RANK ** -0.5),
        'gla_b_gk_up': nrm((N_GLA, GLA_DK), 0.1),
        'gla_o_norm': gain((N_GLA, GLA_DVH)),
        'gla_w_out': nrm((N_GLA, GLA_DV, D), GLA_DV ** -0.5),
        'rwkv_mix': unif((N_RWKV, 6, D), 0.0, 1.0),
        'rwkv_w_r': nrm((N_RWKV, D, D), D ** -0.5),
        'rwkv_w_k': nrm((N_RWKV, D, D), D ** -0.5),
        'rwkv_w_v': nrm((N_RWKV, D, D), D ** -0.5),
        'rwkv_w_o': nrm((N_RWKV, D, D), D ** -0.5),
        'rwkv_w0': unif((N_RWKV, D), -6.5, -1.5),
        'rwkv_w_w1': nrm((N_RWKV, D, RWKV_DECAY_RANK), D ** -0.5),
        'rwkv_w_w2': nrm((N_RWKV, RWKV_DECAY_RANK, D), 0.1 * RWKV_DECAY_RANK ** -0.5),
        'rwkv_a0': nrm((N_RWKV, D), 0.1),
        'rwkv_w_a1': nrm((N_RWKV, D, RWKV_AAA_RANK), D ** -0.5),
        'rwkv_w_a2': nrm((N_RWKV, RWKV_AAA_RANK, D), 0.1 * RWKV_AAA_RANK ** -0.5),
        'rwkv_w_g1': nrm((N_RWKV, D, RWKV_GATE_RANK), D ** -0.5),
        'rwkv_w_g2': nrm((N_RWKV, RWKV_GATE_RANK, D), RWKV_GATE_RANK ** -0.5),
        'rwkv_k_k': 0.85 + nrm((N_RWKV, D), 0.02),
        'rwkv_k_a': 1.0 + nrm((N_RWKV, D), 0.02),
        'rwkv_r_k': -0.04 + nrm((N_RWKV, D), 0.02),
        'rwkv_lnx_g': gain((N_RWKV, D)),
        'rwkv_lnx_b': nrm((N_RWKV, D), 0.02),
        's5_lam_re': -0.5 + nrm((N_S5, S5_GROUPS, S5_STATE), 0.01),
        's5_lam_im': math.pi * n_idx + nrm((N_S5, S5_GROUPS, S5_STATE), 0.01),
        's5_log_dt': unif((N_S5, S5_GROUPS), math.log(S5_DT_MIN), math.log(S5_DT_MAX)),
        's5_b_re': nrm((N_S5, S5_GROUPS, S5_STATE, S5_GROUP), S5_GROUP ** -0.5),
        's5_b_im': nrm((N_S5, S5_GROUPS, S5_STATE, S5_GROUP), S5_GROUP ** -0.5),
        's5_c_re': nrm((N_S5, S5_GROUPS, S5_GROUP, S5_STATE), S5_STATE ** -0.5),
        's5_c_im': nrm((N_S5, S5_GROUPS, S5_GROUP, S5_STATE), S5_STATE ** -0.5),
        's5_d': nrm((N_S5, D), 1.0),
        's5_w_glu1': nrm((N_S5, D, D), D ** -0.5),
        's5_w_glu2': nrm((N_S5, D, D), D ** -0.5),
    }


def reference(x, mem, norm_gains, mem_norm, mem_w_kv, xa_wq, xa_wo, mlp_w1, mlp_w2,
              gla_w_in, gla_w_gk_up, gla_b_gk_up, gla_o_norm, gla_w_out,
              rwkv_mix, rwkv_w_r, rwkv_w_k, rwkv_w_v, rwkv_w_o, rwkv_w0, rwkv_w_w1, rwkv_w_w2,
              rwkv_a0, rwkv_w_a1, rwkv_w_a2, rwkv_w_g1, rwkv_w_g2, rwkv_k_k, rwkv_k_a, rwkv_r_k,
              rwkv_lnx_g, rwkv_lnx_b,
              s5_lam_re, s5_lam_im, s5_log_dt, s5_b_re, s5_b_im, s5_c_re, s5_c_im, s5_d,
              s5_w_glu1, s5_w_glu2):
    bsz = mem.shape[0]
    mem_kv = rms_norm(mem, mem_norm) @ mem_w_kv
    mem_k = mem_kv[..., :XA_WIDTH].reshape(bsz, N_MEM, XA_HEADS, XA_HEAD_DIM)
    mem_v = mem_kv[..., XA_WIDTH:].reshape(bsz, N_MEM, XA_HEADS, XA_HEAD_DIM)

    for i in range(DEPTH):
        kind = i % N_MIXERS
        j = i // N_MIXERS
        g = norm_gains[i]
        hn = rms_norm(x, g[0])
        if kind == 0:
            m = gla_mixer(hn, gla_w_in[j], gla_w_gk_up[j], gla_b_gk_up[j], gla_o_norm[j],
                          gla_w_out[j])
        elif kind == 1:
            m = rwkv7_mixer(hn, rwkv_mix[j], rwkv_w_r[j], rwkv_w_k[j], rwkv_w_v[j], rwkv_w_o[j],
                            rwkv_w0[j], rwkv_w_w1[j], rwkv_w_w2[j], rwkv_a0[j], rwkv_w_a1[j],
                            rwkv_w_a2[j], rwkv_w_g1[j], rwkv_w_g2[j], rwkv_k_k[j], rwkv_k_a[j],
                            rwkv_r_k[j], rwkv_lnx_g[j], rwkv_lnx_b[j])
        else:
            m = s5_mixer(hn, s5_lam_re[j], s5_lam_im[j], s5_log_dt[j], s5_b_re[j], s5_b_im[j],
                         s5_c_re[j], s5_c_im[j], s5_d[j], s5_w_glu1[j], s5_w_glu2[j])
        x = x + rms_norm(m, g[1])
        x = x + rms_norm(cross_attention(rms_norm(x, g[2]), mem_k, mem_v, xa_wq[i], xa_wo[i]), g[3])
        x = x + rms_norm(sq_relu_mlp(rms_norm(x, g[4]), mlp_w1[i], mlp_w2[i]), g[5])
    return x
```

```python
import functools
import math

import jax
import jax.numpy as jnp
from jax import lax
from jax.experimental import pallas as pl
from jax.experimental.pallas import tpu as pltpu

F32 = jnp.float32
BF16 = jnp.bfloat16
HI = lax.Precision.HIGHEST

NORM_EPS = 1e-6
CHUNK = 64
N_MIXERS = 3
GLA_HEADS = 4
GLA_GATE_NORMALIZER = 16.0
RWKV_HEAD = 64
RWKV_LNX_EPS = 64e-5
XA_HEADS = 4
XA_HEAD_DIM = 128
S5_SLAB_GROUPS = 16
SUBLANES = 8
LANES = 128
VMEM_LIMIT = 60 * 1024 * 1024


def _cp(*sem):
    return pltpu.CompilerParams(dimension_semantics=sem, vmem_limit_bytes=VMEM_LIMIT)


def _rms(x, gain, eps=NORM_EPS):
    ms = jnp.mean(x * x, axis=-1, keepdims=True)
    return x * lax.rsqrt(ms + eps) * gain


def _for_row_blocks(n_rows, body, carry=None, rows=16):
    def step(i, c):
        return body(pl.ds(pl.multiple_of(i * rows, rows), rows), c)
    return lax.fori_loop(0, n_rows // rows, step, carry)


def _norm_rows_into(dst_ref, x_ref, g_ref):
    def body(rs, c):
        dst_ref[rs, :] = _rms(x_ref[rs, :], g_ref[...]).astype(dst_ref.dtype)
        return c
    _for_row_blocks(x_ref.shape[0], body)


def _add_norm_rows_inplace(o_ref, res_ref, g_ref):
    def body(rs, c):
        o_ref[rs, :] = res_ref[rs, :] + _rms(o_ref[rs, :], g_ref[...])
        return c
    _for_row_blocks(o_ref.shape[0], body)


def _dot(a, b, precision=None):
    return jnp.dot(a, b, preferred_element_type=F32, precision=precision)


def _dot_nt(a, b, precision=None):
    return lax.dot_general(a, b, (((1,), (1,)), ((), ())), preferred_element_type=F32,
                           precision=precision)


def _dot_tn(a, b, precision=None):
    return lax.dot_general(a, b, (((0,), (0,)), ((), ())), preferred_element_type=F32,
                           precision=precision)


def _sigmoid(x):
    return 1.0 / (1.0 + jnp.exp(-x))


def _softplus(x):
    return jnp.maximum(x, 0.0) + jnp.log1p(jnp.exp(-jnp.abs(x)))


def _norm_kernel(x_ref, g_ref, o_ref):
    _norm_rows_into(o_ref, x_ref, g_ref)


def _norm(x, gain, out_dtype=F32, tm=256):
    m, d = x.shape
    tm = min(tm, m)
    return pl.pallas_call(
        _norm_kernel, out_shape=jax.ShapeDtypeStruct((m, d), out_dtype),
        grid=(m // tm,),
        in_specs=[pl.BlockSpec((tm, d), lambda i: (i, 0)), pl.BlockSpec((1, d), lambda i: (0, 0))],
        out_specs=pl.BlockSpec((tm, d), lambda i: (i, 0)),
        compiler_params=_cp("parallel"), name="rmsnorm")(x, gain.reshape(1, d))


def _norm_matmul_kernel(x_ref, g_ref, w_ref, o_ref, hn_ref):
    @pl.when(pl.program_id(1) == 0)
    def _():
        _norm_rows_into(hn_ref, x_ref, g_ref)

    o_ref[...] = _dot(hn_ref[...], w_ref[...]).astype(o_ref.dtype)


def _norm_matmul(x, gain, w, out_dtype, tm=512, tn=512):
    m, d = x.shape
    n = w.shape[1]
    tm, tn = min(tm, m), min(tn, n)
    return pl.pallas_call(
        _norm_matmul_kernel, out_shape=jax.ShapeDtypeStruct((m, n), out_dtype),
        grid=(m // tm, n // tn),
        in_specs=[pl.BlockSpec((tm, d), lambda i, j: (i, 0)),
                  pl.BlockSpec((1, d), lambda i, j: (0, 0)),
                  pl.BlockSpec((d, tn), lambda i, j: (0, j))],
        out_specs=pl.BlockSpec((tm, tn), lambda i, j: (i, j)),
        scratch_shapes=[pltpu.VMEM((tm, d), BF16)],
        compiler_params=_cp("parallel", "arbitrary"), name="norm_matmul")(x, gain.reshape(1, d), w)


def _matmul_norm_res_kernel(*refs, glu, tn):
    if glu:
        a_ref, w_ref, w2_ref, res_ref, g_ref, o_ref = refs
    else:
        a_ref, w_ref, res_ref, g_ref, o_ref = refs
    j = pl.program_id(1)
    a = a_ref[...]
    y = _dot(a, w_ref[...])
    if glu:
        y = y * _sigmoid(_dot(a, w2_ref[...]))
    o_ref[:, pl.ds(pl.multiple_of(j * tn, tn), tn)] = y

    @pl.when(j == pl.num_programs(1) - 1)
    def _():
        _add_norm_rows_inplace(o_ref, res_ref, g_ref)


def _matmul_norm_res(a, ws, res, gain, tm=512):
    m, k = a.shape
    n = ws[0].shape[1]
    glu = len(ws) == 2
    tm, tn = min(tm, m), min(256 if glu else 512, n)
    w_specs = [pl.BlockSpec((k, tn), lambda i, j: (0, j)) for _ in ws]
    return pl.pallas_call(
        functools.partial(_matmul_norm_res_kernel, glu=glu, tn=tn),
        out_shape=jax.ShapeDtypeStruct((m, n), F32),
        grid=(m // tm, n // tn),
        in_specs=[pl.BlockSpec((tm, k), lambda i, j: (i, 0))] + w_specs + [
            pl.BlockSpec((tm, n), lambda i, j: (i, 0)),
            pl.BlockSpec((1, n), lambda i, j: (0, 0))],
        out_specs=pl.BlockSpec((tm, n), lambda i, j: (i, 0)),
        compiler_params=_cp("parallel", "arbitrary"),
        name="glu_norm_res" if glu else "matmul_norm_res")(a, *ws, res, gain.reshape(1, n))


def _xa_kernel(x_ref, g_in_ref, wq_ref, k_ref, v_ref, wo_ref, g_out_ref, o_ref, hn_ref):
    _norm_rows_into(hn_ref, x_ref, g_in_ref)
    q = _dot(hn_ref[...], wq_ref[...]) * (XA_HEAD_DIM ** -0.5)
    heads = []
    for h in range(XA_HEADS):
        sl = slice(h * XA_HEAD_DIM, (h + 1) * XA_HEAD_DIM)
        s = _dot_nt(q[:, sl].astype(BF16), k_ref[0, :, sl])
        s = s - jnp.max(s, axis=-1, keepdims=True)
        p = jnp.exp(s)
        p = p / jnp.sum(p, axis=-1, keepdims=True)
        heads.append(_dot(p.astype(BF16), v_ref[0, :, sl]))
    o = jnp.concatenate(heads, axis=-1).astype(BF16)
    o_ref[...] = _dot(o, wo_ref[...])
    _add_norm_rows_inplace(o_ref, x_ref, g_out_ref)


def _cross_attention(x, g_in, wq, mem_kv, wo, g_out, seq, tm=256):
    m, d = x.shape
    xw = wq.shape[1]
    n_mem = mem_kv.shape[1]
    tm = min(tm, seq)
    tiles_per_seq = seq // tm
    return pl.pallas_call(
        _xa_kernel, out_shape=jax.ShapeDtypeStruct((m, d), F32),
        grid=(m // tm,),
        in_specs=[pl.BlockSpec((tm, d), lambda i: (i, 0)),
                  pl.BlockSpec((1, d), lambda i: (0, 0)),
                  pl.BlockSpec((d, xw), lambda i: (0, 0)),
                  pl.BlockSpec((1, n_mem, xw), lambda i: (i // tiles_per_seq, 0, 0)),
                  pl.BlockSpec((1, n_mem, xw), lambda i: (i // tiles_per_seq, 0, 1)),
                  pl.BlockSpec((xw, d), lambda i: (0, 0)),
                  pl.BlockSpec((1, d), lambda i: (0, 0))],
        out_specs=pl.BlockSpec((tm, d), lambda i: (i, 0)),
        scratch_shapes=[pltpu.VMEM((tm, d), BF16)],
        compiler_params=_cp("parallel"), name="cross_attention")(
            x, g_in.reshape(1, d), wq, mem_kv, mem_kv, wo, g_out.reshape(1, d))


def _mlp_kernel(x_ref, g_in_ref, w1_ref, w2_ref, g_out_ref, o_ref, hn_ref, *, tn):
    f = pl.program_id(1)

    @pl.when(f == 0)
    def _():
        _norm_rows_into(hn_ref, x_ref, g_in_ref)

    h = jnp.maximum(_dot(hn_ref[...], w1_ref[...]), 0.0)
    h = (h * h).astype(BF16)
    d = o_ref.shape[1]
    for n in range(d // tn):
        sl = slice(n * tn, (n + 1) * tn)
        part = _dot(h, w2_ref[:, sl])

        @pl.when(f == 0)
        def _():
            o_ref[:, sl] = part

        @pl.when(f != 0)
        def _():
            o_ref[:, sl] += part

    @pl.when(f == pl.num_programs(1) - 1)
    def _():
        _add_norm_rows_inplace(o_ref, x_ref, g_out_ref)


def _mlp(x, g_in, w1, w2, g_out, tm=512, tf=512):
    m, d = x.shape
    ff = w1.shape[1]
    tm, tf = min(tm, m), min(tf, ff)
    return pl.pallas_call(
        functools.partial(_mlp_kernel, tn=min(512, d)),
        out_shape=jax.ShapeDtypeStruct((m, d), F32),
        grid=(m // tm, ff // tf),
        in_specs=[pl.BlockSpec((tm, d), lambda i, f: (i, 0), pipeline_mode=pl.Buffered(1)),
                  pl.BlockSpec((1, d), lambda i, f: (0, 0)),
                  pl.BlockSpec((d, tf), lambda i, f: (0, f)),
                  pl.BlockSpec((tf, d), lambda i, f: (f, 0)),
                  pl.BlockSpec((1, d), lambda i, f: (0, 0))],
        out_specs=pl.BlockSpec((tm, d), lambda i, f: (i, 0)),
        scratch_shapes=[pltpu.VMEM((tm, d), BF16)],
        compiler_params=_cp("parallel", "arbitrary"), name="mlp")(
            x, g_in.reshape(1, d), w1, w2, g_out.reshape(1, d))


def _gla_kernel(q_ref, k_ref, v_ref, g_ref, gk_ref, wup_ref, bup_ref, onorm_ref, tril_ref,
                o_ref, st_ref):
    @pl.when(pl.program_id(2) == 0)
    def _():
        st_ref[...] = jnp.zeros_like(st_ref)

    dk = q_ref.shape[1]
    z = _dot(gk_ref[...], wup_ref[...], HI) + bup_ref[...]
    log_alpha = (jnp.minimum(z, 0.0) - jnp.log1p(jnp.exp(-jnp.abs(z)))) / GLA_GATE_NORMALIZER
    cum = _dot(tril_ref[...], log_alpha, HI)
    cum_last = cum[CHUNK - 1:CHUNK, :]
    k_dec = (k_ref[...].astype(F32) * jnp.exp(cum_last - cum)).astype(BF16)
    st = st_ref[...] * jnp.exp(cum_last) + _dot_tn(v_ref[...], k_dec)
    st_ref[...] = st
    q = (q_ref[...].astype(F32) * dk ** -0.5).astype(BF16)
    o = _dot_nt(q, st.astype(BF16))
    g = g_ref[...].astype(F32)
    o_ref[...] = (_rms(o, onorm_ref[...]) * (g * _sigmoid(g))).astype(o_ref.dtype)


def _gla_scan(proj, gk_lo, w_up, b_up, o_norm, bsz, seq):
    m = proj.shape[0]
    dk_all = w_up.shape[1]
    dkh = dk_all // GLA_HEADS
    dv_all = (proj.shape[1] - 2 * dk_all) // 2
    dvh = dv_all // GLA_HEADS
    nc = seq // CHUNK
    rank_pad = gk_lo.shape[1]
    tril = jnp.tril(jnp.ones((CHUNK, CHUNK), F32))
    row = lambda b, h, c: b * nc + c
    k_off = dk_all // dkh
    v_off = 2 * dk_all // dvh
    g_off = v_off + dv_all // dvh
    return pl.pallas_call(
        _gla_kernel, out_shape=jax.ShapeDtypeStruct((m, dv_all), BF16),
        grid=(bsz, GLA_HEADS, nc),
        in_specs=[pl.BlockSpec((CHUNK, dkh), lambda b, h, c: (row(b, h, c), h)),
                  pl.BlockSpec((CHUNK, dkh), lambda b, h, c: (row(b, h, c), k_off + h)),
                  pl.BlockSpec((CHUNK, dvh), lambda b, h, c: (row(b, h, c), v_off + h)),
                  pl.BlockSpec((CHUNK, dvh), lambda b, h, c: (row(b, h, c), g_off + h)),
                  pl.BlockSpec((CHUNK, rank_pad), lambda b, h, c: (row(b, h, c), 0)),
                  pl.BlockSpec((rank_pad, dkh), lambda b, h, c: (0, h)),
                  pl.BlockSpec((1, dkh), lambda b, h, c: (0, h)),
                  pl.BlockSpec((1, dvh), lambda b, h, c: (0, 0)),
                  pl.BlockSpec((CHUNK, CHUNK), lambda b, h, c: (0, 0))],
        out_specs=pl.BlockSpec((CHUNK, dvh), lambda b, h, c: (row(b, h, c), h)),
        scratch_shapes=[pltpu.VMEM((dvh, dkh), F32)],
        compiler_params=_cp("parallel", "parallel", "arbitrary"), name="gla_scan")(
            proj, proj, proj, proj, gk_lo, w_up, b_up.reshape(1, dk_all), o_norm.reshape(1, dvh), tril)


def _shiftmix_matmul_kernel(x_ref, prev_ref, g_ref, mix_ref, w_ref, o_ref, xm_ref, *, tiles_per_seq):
    @pl.when(pl.program_id(2) == 0)
    def _():
        gain = g_ref[...]
        mix = mix_ref[0]
        last = _rms(prev_ref[...], gain)[SUBLANES - 1:SUBLANES, :]
        first_tile = pl.program_id(0) % tiles_per_seq == 0
        last = jnp.where(first_tile, 0.0, last)

        def body(rs, last):
            hn = _rms(x_ref[rs, :], gain)
            rows = lax.broadcasted_iota(jnp.int32, hn.shape, 0)
            shifted = jnp.where(rows == 0, last, pltpu.roll(hn, 1, axis=0))
            xm_ref[rs, :] = (hn + (shifted - hn) * mix).astype(BF16)
            return hn[hn.shape[0] - 1:, :]

        _for_row_blocks(x_ref.shape[0], body, last)

    o_ref[0] = _dot(xm_ref[...], w_ref[0]).astype(o_ref.dtype)


def _shiftmix_matmul(x, gain, mix, w, out_dtype, seq, tm=512, tn=512):
    m, d = x.shape
    p_cnt, _, n = w.shape
    tm, tn = min(tm, seq), min(tn, n)
    blk = tm // SUBLANES
    return pl.pallas_call(
        functools.partial(_shiftmix_matmul_kernel, tiles_per_seq=seq // tm),
        out_shape=jax.ShapeDtypeStruct((p_cnt, m, n), out_dtype),
        grid=(m // tm, p_cnt, n // tn),
        in_specs=[pl.BlockSpec((tm, d), lambda i, p, j: (i, 0)),
                  pl.BlockSpec((SUBLANES, d), lambda i, p, j: (jnp.maximum(i * blk - 1, 0), 0)),
                  pl.BlockSpec((1, d), lambda i, p, j: (0, 0)),
                  pl.BlockSpec((1, 1, d), lambda i, p, j: (p, 0, 0)),
                  pl.BlockSpec((1, d, tn), lambda i, p, j: (p, 0, j))],
        out_specs=pl.BlockSpec((1, tm, tn), lambda i, p, j: (p, i, j)),
        scratch_shapes=[pltpu.VMEM((tm, d), BF16)],
        compiler_params=_cp("parallel", "arbitrary", "arbitrary"), name="shiftmix_matmul")(
            x, x, gain.reshape(1, d), mix.reshape(p_cnt, 1, d), w)


def _lowrank_out_kernel(h_ref, w_ref, b_ref, o_ref):
    p = pl.program_id(0)
    h = h_ref[0]
    act = jnp.where(p == 0, jnp.tanh(h), jnp.where(p == 1, h, _sigmoid(h)))
    o_ref[0] = _dot(act.astype(BF16), w_ref[0]) + b_ref[0]


def _lowrank_out(h, w2, bias, tm=512):
    p_cnt, m, r = h.shape
    d = w2.shape[2]
    tm = min(tm, m)
    return pl.pallas_call(
        _lowrank_out_kernel, out_shape=jax.ShapeDtypeStruct((p_cnt, m, d), F32),
        grid=(p_cnt, m // tm),
        in_specs=[pl.BlockSpec((1, tm, r), lambda p, i: (p, i, 0)),
                  pl.BlockSpec((1, r, d), lambda p, i: (p, 0, 0)),
                  pl.BlockSpec((1, 1, d), lambda p, i: (p, 0, 0))],
        out_specs=pl.BlockSpec((1, tm, d), lambda p, i: (p, i, 0)),
        compiler_params=_cp("arbitrary", "arbitrary"), name="lowrank_out")(h, w2, bias)


def _rwkv_kernel(rkv_ref, lr_ref, par_ref, bd_ref, tril_ref, o_ref, st_ref, y_ref):
    @pl.when(pl.program_id(2) == 0)
    def _():
        st_ref[...] = jnp.zeros_like(st_ref)

    n = RWKV_HEAD
    heads = st_ref.shape[0]
    r = rkv_ref[0].astype(F32)
    k = rkv_ref[1].astype(F32)
    v = rkv_ref[2].astype(F32)
    w_log = -_softplus(-lr_ref[0]) - 0.5
    log_w = -jnp.exp(w_log)
    a = _sigmoid(lr_ref[1])
    gate = lr_ref[2]
    k_k, k_a, r_k = par_ref[0:1, :], par_ref[1:2, :], par_ref[2:3, :]
    lnx_g, lnx_b = par_ref[3:4, :], par_ref[4:5, :]
    bd = bd_ref[...]

    kk = k * k_k
    kk = kk / jnp.maximum(jnp.sqrt(_dot(kk * kk, bd, HI)), 1e-12)
    k2 = k * (1.0 + (a - 1.0) * k_a)
    cw = _dot(tril_ref[...], log_w, HI)
    cw_last = cw[CHUNK - 1:CHUNK, :]
    e_neg = jnp.exp(-cw)
    e_end = jnp.exp(cw_last - cw)
    a_t = -kk * jnp.exp(cw - log_w)
    r_t = r * jnp.exp(cw)
    b_vec = kk * a
    b_t = b_vec * e_neg
    k_t = k2 * e_neg
    b_w = b_vec * e_end
    k_w = k2 * e_end
    w_end = jnp.exp(cw_last)

    rows = lax.broadcasted_iota(jnp.int32, (CHUNK, CHUNK), 0)
    cols = lax.broadcasted_iota(jnp.int32, (CHUNK, CHUNK), 1)
    strict = rows > cols
    incl = rows >= cols
    for h in range(heads):
        sl = slice(h * n, (h + 1) * n)
        ar = jnp.concatenate([a_t[:, sl], r_t[:, sl]], axis=0)
        bk = jnp.concatenate([b_t[:, sl], k_t[:, sl]], axis=0)
        gram = _dot_nt(ar, bk, HI)
        s0 = st_ref[h]
        proj = _dot_nt(ar, s0, HI)
        v_h = v[:, sl]
        nil = jnp.where(strict, gram[:CHUNK, :CHUNK], 0.0)
        a_ak = jnp.where(strict, gram[:CHUNK, CHUNK:], 0.0)
        a_rb = jnp.where(incl, gram[CHUNK:, :CHUNK], 0.0)
        a_rk = jnp.where(incl, gram[CHUNK:, CHUNK:], 0.0)
        u = proj[:CHUNK] + _dot(a_ak, v_h, HI)
        u = u + _dot(nil, u, HI)
        power = nil
        for _ in range(int(math.log2(CHUNK)) - 1):
            power = _dot(power, power, HI)
            u = u + _dot(power, u, HI)
        y_ref[:, sl] = proj[CHUNK:] + _dot(a_rb, u, HI) + _dot(a_rk, v_h, HI)
        st_ref[h] = (s0 * w_end[:, sl] + _dot_tn(u, b_w[:, sl], HI) + _dot_tn(v_h, k_w[:, sl], HI))

    y = y_ref[...]
    inv_n = 1.0 / n
    mu = _dot(y, bd, HI) * inv_n
    yc = y - mu
    var = _dot(yc * yc, bd, HI) * inv_n
    y = yc * lax.rsqrt(var + RWKV_LNX_EPS) * lnx_g + lnx_b
    bonus = _dot(r * k2 * r_k, bd, HI) * v
    o_ref[...] = ((y + bonus) * gate).astype(o_ref.dtype)


def _rwkv_scan(rkv, lr, params, bsz, seq, heads_per_step=8):
    _, m, d = rkv.shape
    n_heads = d // RWKV_HEAD
    hg = min(heads_per_step, n_heads)
    w = hg * RWKV_HEAD
    nc = seq // CHUNK
    lane_head = jnp.arange(w) // RWKV_HEAD
    bd = (lane_head[:, None] == lane_head[None, :]).astype(F32)
    tril = jnp.tril(jnp.ones((CHUNK, CHUNK), F32))
    n_par = params.shape[0]
    return pl.pallas_call(
        _rwkv_kernel, out_shape=jax.ShapeDtypeStruct((m, d), BF16),
        grid=(bsz, n_heads // hg, nc),
        in_specs=[pl.BlockSpec((3, CHUNK, w), lambda b, g, c: (0, b * nc + c, g)),
                  pl.BlockSpec((3, CHUNK, w), lambda b, g, c: (0, b * nc + c, g)),
                  pl.BlockSpec((n_par, w), lambda b, g, c: (0, g)),
                  pl.BlockSpec((w, w), lambda b, g, c: (0, 0)),
                  pl.BlockSpec((CHUNK, CHUNK), lambda b, g, c: (0, 0))],
        out_specs=pl.BlockSpec((CHUNK, w), lambda b, g, c: (b * nc + c, g)),
        scratch_shapes=[pltpu.VMEM((hg, RWKV_HEAD, RWKV_HEAD), F32), pltpu.VMEM((CHUNK, w), F32)],
        compiler_params=_cp("parallel", "parallel", "arbitrary"), name="rwkv_scan")(
            rkv, lr, params, bd, tril)


def _s5_kernel(h_ref, wb_ref, wc_ref, coef_ref, d_ref, o_ref, x_ref, carry_ref):
    @pl.when(pl.program_id(2) == 0)
    def _():
        carry_ref[...] = jnp.zeros_like(carry_ref)

    tt = h_ref.shape[0]
    ns = carry_ref.shape[2]
    h = h_ref[...]
    x_ref[...] = _dot(h.astype(BF16), wb_ref[0])

    def block(i, carry):
        pr, pi = carry
        r0 = pl.multiple_of(i * SUBLANES, SUBLANES)
        xr = x_ref[pl.ds(r0, SUBLANES), 0:ns]
        xi = x_ref[pl.ds(r0, SUBLANES), ns:2 * ns]
        for s_idx, shift in enumerate((1, 2, 4)):
            cr = coef_ref[0, 2 * s_idx]
            ci = coef_ref[0, 2 * s_idx + 1]
            sr = pltpu.roll(xr, shift, axis=0)
            si = pltpu.roll(xi, shift, axis=0)
            xr, xi = xr + cr * sr - ci * si, xi + cr * si + ci * sr
        cr = coef_ref[0, 6]
        ci = coef_ref[0, 7]
        xr, xi = xr + cr * pr - ci * pi, xi + cr * pi + ci * pr
        x_ref[pl.ds(r0, SUBLANES), 0:ns] = xr
        x_ref[pl.ds(r0, SUBLANES), ns:2 * ns] = xi
        return (jnp.broadcast_to(xr[SUBLANES - 1:SUBLANES, :], xr.shape),
                jnp.broadcast_to(xi[SUBLANES - 1:SUBLANES, :], xi.shape))

    pr0 = jnp.broadcast_to(carry_ref[0], (SUBLANES, ns))
    pi0 = jnp.broadcast_to(carry_ref[1], (SUBLANES, ns))
    pr, pi = lax.fori_loop(0, tt // SUBLANES, block, (pr0, pi0))
    carry_ref[0] = pr[0:1, :]
    carry_ref[1] = pi[0:1, :]

    y = _dot(x_ref[...].astype(BF16), wc_ref[0]) + d_ref[...] * h
    o_ref[...] = (0.5 * y * (1.0 + lax.erf(y * (2.0 ** -0.5)))).astype(o_ref.dtype)


def _s5_scan(hn, lam_re, lam_im, log_dt, b_re, b_im, c_re, c_im, d_skip, bsz, seq, tt=256):
    m, d = hn.shape
    n_groups, n_state, grp = b_re.shape
    sg = min(S5_SLAB_GROUPS, n_groups)
    slabs = n_groups // sg
    ch = sg * grp
    ns = sg * n_state
    tt = min(tt, seq)

    lr = jnp.minimum(lam_re.astype(F32), -1e-4)
    li = lam_im.astype(F32)
    dt = jnp.exp(log_dt.astype(F32))[:, None]
    mag = jnp.exp(lr * dt)
    ab_re = mag * jnp.cos(li * dt)
    ab_im = mag * jnp.sin(li * dt)
    den = lr * lr + li * li
    nr, ni = ab_re - 1.0, ab_im
    z_re = (nr * lr + ni * li) / den
    z_im = (ni * lr - nr * li) / den
    bb_re = z_re[..., None] * b_re - z_im[..., None] * b_im
    bb_im = z_re[..., None] * b_im + z_im[..., None] * b_re

    eye = jnp.eye(sg, dtype=F32)

    def block_diag_in(bb):
        t = bb.reshape(slabs, sg, n_state, grp)
        return jnp.einsum('sgph,gk->sghkp', t, eye).reshape(slabs, ch, ns)

    def block_diag_out(cc):
        t = cc.reshape(slabs, sg, grp, n_state)
        return jnp.einsum('sgqp,gk->sgpkq', t, eye).reshape(slabs, ns, ch)

    w_b = jnp.concatenate([block_diag_in(bb_re), block_diag_in(bb_im)], axis=2).astype(BF16)
    w_c = jnp.concatenate([block_diag_out(c_re.astype(F32)), -block_diag_out(c_im.astype(F32))],
                          axis=1).astype(BF16)

    def powers(e):
        e = e[:, None, None]
        mg = jnp.exp(lr[None] * dt[None] * e)
        ang = li[None] * dt[None] * e
        re = (mg * jnp.cos(ang)).reshape(-1, slabs, ns)
        im = (mg * jnp.sin(ang)).reshape(-1, slabs, ns)
        return re, im

    row = jnp.arange(SUBLANES, dtype=F32)
    coefs = []
    for shift in (1, 2, 4):
        re, im = powers(jnp.full((SUBLANES,), float(shift), F32))
        mask = (row >= shift)[:, None, None]
        coefs += [jnp.where(mask, re, 0.0), jnp.where(mask, im, 0.0)]
    re, im = powers(row + 1.0)
    coefs += [re, im]
    coef = jnp.stack(coefs, axis=0).transpose(2, 0, 1, 3)

    nt = seq // tt
    return pl.pallas_call(
        _s5_kernel, out_shape=jax.ShapeDtypeStruct((m, d), BF16),
        grid=(bsz, slabs, nt),
        in_specs=[pl.BlockSpec((tt, ch), lambda b, s, t: (b * nt + t, s)),
                  pl.BlockSpec((1, ch, 2 * ns), lambda b, s, t: (s, 0, 0)),
                  pl.BlockSpec((1, 2 * ns, ch), lambda b, s, t: (s, 0, 0)),
                  pl.BlockSpec((1, 8, SUBLANES, ns), lambda b, s, t: (s, 0, 0, 0)),
                  pl.BlockSpec((1, ch), lambda b, s, t: (0, s))],
        out_specs=pl.BlockSpec((tt, ch), lambda b, s, t: (b * nt + t, s)),
        scratch_shapes=[pltpu.VMEM((tt, 2 * ns), F32), pltpu.VMEM((2, 1, ns), F32)],
        compiler_params=_cp("parallel", "parallel", "arbitrary"), name="s5_scan")(
            hn, w_b, w_c, coef, d_skip.reshape(1, d))


def _pad_to(x, axis, size):
    pad = size - x.shape[axis]
    if pad == 0:
        return x
    widths = [(0, 0)] * x.ndim
    widths[axis] = (0, pad)
    return jnp.pad(x, widths)


def _round_up(n, k):
    return (n + k - 1) // k * k


def _gla_layer(x, g_pre, g_post, w_in, w_gk_up, b_gk_up, o_norm, w_out, bsz, seq):
    dk_all = w_gk_up.shape[1]
    rank = w_gk_up.shape[0]
    n_main = w_in.shape[1] - rank
    rank_pad = _round_up(rank, LANES)
    proj = _norm_matmul(x, g_pre, w_in[:, :n_main].astype(BF16), BF16)
    gk_lo = _norm_matmul(x, g_pre, _pad_to(w_in[:, n_main:], 1, rank_pad).astype(BF16), F32)
    o = _gla_scan(proj, gk_lo, _pad_to(w_gk_up, 0, rank_pad), b_gk_up, o_norm, bsz, seq)
    return _matmul_norm_res(o, [w_out.astype(BF16)], x, g_post)


def _rwkv_layer(x, g_pre, g_post, mix, w_r, w_k, w_v, w_o, w0, w_w1, w_w2, a0, w_a1, w_a2,
                w_g1, w_g2, k_k, k_a, r_k, lnx_g, lnx_b, bsz, seq):
    d = x.shape[1]
    rkv = _shiftmix_matmul(x, g_pre, jnp.stack([mix[0], mix[2], mix[3]]),
                           jnp.stack([w_r, w_k, w_v]).astype(BF16), BF16, seq)
    rank_pad = _round_up(max(w_w1.shape[1], w_a1.shape[1], w_g1.shape[1]), LANES)
    w1 = jnp.stack([_pad_to(w, 1, rank_pad) for w in (w_w1, w_a1, w_g1)]).astype(BF16)
    w2 = jnp.stack([_pad_to(w, 0, rank_pad) for w in (w_w2, w_a2, w_g2)]).astype(BF16)
    low = _shiftmix_matmul(x, g_pre, jnp.stack([mix[1], mix[4], mix[5]]), w1, F32, seq)
    bias = jnp.stack([w0, a0, jnp.zeros_like(w0)]).reshape(3, 1, d)
    lr = _lowrank_out(low, w2, bias)
    params = _pad_to(jnp.stack([k_k, k_a, r_k, lnx_g, lnx_b]), 0, SUBLANES)
    y = _rwkv_scan(rkv, lr, params, bsz, seq)
    return _matmul_norm_res(y, [w_o.astype(BF16)], x, g_post)


def _s5_layer(x, g_pre, g_post, lam_re, lam_im, log_dt, b_re, b_im, c_re, c_im, d_skip,
              w_glu1, w_glu2, bsz, seq):
    hn = _norm(x, g_pre)
    y = _s5_scan(hn, lam_re, lam_im, log_dt, b_re, b_im, c_re, c_im, d_skip, bsz, seq)
    return _matmul_norm_res(y, [w_glu1.astype(BF16), w_glu2.astype(BF16)], x, g_post)


def kernel(x, mem, norm_gains, mem_norm, mem_w_kv, xa_wq, xa_wo, mlp_w1, mlp_w2, gla_w_in, gla_w_gk_up, gla_b_gk_up, gla_o_norm, gla_w_out, rwkv_mix, rwkv_w_r, rwkv_w_k, rwkv_w_v, rwkv_w_o, rwkv_w0, rwkv_w_w1, rwkv_w_w2, rwkv_a0, rwkv_w_a1, rwkv_w_a2, rwkv_w_g1, rwkv_w_g2, rwkv_k_k, rwkv_k_a, rwkv_r_k, rwkv_lnx_g, rwkv_lnx_b, s5_lam_re, s5_lam_im, s5_log_dt, s5_b_re, s5_b_im, s5_c_re, s5_c_im, s5_d, s5_w_glu1, s5_w_glu2):
    bsz, seq, d = x.shape
    n_mem = mem.shape[1]
    depth = norm_gains.shape[0]
    assert seq % CHUNK == 0 and xa_wq.shape[2] == XA_HEADS * XA_HEAD_DIM

    mem_kv = _norm_matmul(mem.reshape(bsz * n_mem, d), mem_norm, mem_w_kv.astype(BF16), BF16)
    mem_kv = mem_kv.reshape(bsz, n_mem, mem_w_kv.shape[1])

    x = x.reshape(bsz * seq, d)
    for i in range(depth):
        kind, j = i % N_MIXERS, i // N_MIXERS
        g = norm_gains[i]
        if kind == 0:
            x = _gla_layer(x, g[0], g[1], gla_w_in[j], gla_w_gk_up[j], gla_b_gk_up[j],
                           gla_o_norm[j], gla_w_out[j], bsz, seq)
        elif kind == 1:
            x = _rwkv_layer(x, g[0], g[1], rwkv_mix[j], rwkv_w_r[j], rwkv_w_k[j], rwkv_w_v[j],
                            rwkv_w_o[j], rwkv_w0[j], rwkv_w_w1[j], rwkv_w_w2[j], rwkv_a0[j],
                            rwkv_w_a1[j], rwkv_w_a2[j], rwkv_w_g1[j], rwkv_w_g2[j], rwkv_k_k[j],
                            rwkv_k_a[j], rwkv_r_k[j], rwkv_lnx_g[j], rwkv_lnx_b[j], bsz, seq)
        else:
            x = _s5_layer(x, g[0], g[1], s5_lam_re[j], s5_lam_im[j], s5_log_dt[j], s5_b_re[j],
                          s5_b_im[j], s5_c_re[j], s5_c_im[j], s5_d[j], s5_w_glu1[j], s5_w_glu2[j],
                          bsz, seq)
        x = _cross_attention(x, g[2], xa_wq[i].astype(BF16), mem_kv, xa_wo[i].astype(BF16), g[3], seq)
        x = _mlp(x, g[4], mlp_w1[i].astype(BF16), mlp_w2[i].astype(BF16), g[5])
    return x.reshape(bsz, seq, d)
```

```python
import functools

import jax
import jax.numpy as jnp
from jax import lax
from jax.experimental import pallas as pl
from jax.experimental.pallas import tpu as pltpu

F32 = jnp.float32
BF16 = jnp.bfloat16
HI = lax.Precision.HIGHEST

NORM_EPS = 1e-6
CHUNK = 64
N_MIXERS = 3
GLA_HEADS = 4
GLA_GATE_NORMALIZER = 16.0
RWKV_HEAD = 64
RWKV_LNX_EPS = 64e-5
XA_HEADS = 4
XA_HEAD_DIM = 128
S5_SLAB_GROUPS = 16
SUBLANES = 8
LANES = 128
VMEM_LIMIT = 60 * 1024 * 1024


def _cp(*sem):
    return pltpu.CompilerParams(dimension_semantics=sem, vmem_limit_bytes=VMEM_LIMIT)


def _rms(x, gain, eps=NORM_EPS):
    ms = jnp.mean(x * x, axis=-1, keepdims=True)
    return x * lax.rsqrt(ms + eps) * gain


def _for_row_blocks(n_rows, body, carry=None, rows=16):
    def step(i, c):
        return body(pl.ds(pl.multiple_of(i * rows, rows), rows), c)
    return lax.fori_loop(0, n_rows // rows, step, carry)


def _norm_rows_into(dst_ref, x_ref, g_ref):
    def body(rs, c):
        dst_ref[rs, :] = _rms(x_ref[rs, :], g_ref[...]).astype(dst_ref.dtype)
        return c
    _for_row_blocks(x_ref.shape[0], body)


def _add_norm_rows_inplace(o_ref, res_ref, g_ref):
    def body(rs, c):
        o_ref[rs, :] = res_ref[rs, :] + _rms(o_ref[rs, :], g_ref[...])
        return c
    _for_row_blocks(o_ref.shape[0], body)


def _dot(a, b, precision=None):
    return jnp.dot(a, b, preferred_element_type=F32, precision=precision)


def _dot_nt(a, b, precision=None):
    return lax.dot_general(a, b, (((1,), (1,)), ((), ())), preferred_element_type=F32,
                           precision=precision)


def _dot_tn(a, b, precision=None):
    return lax.dot_general(a, b, (((0,), (0,)), ((), ())), preferred_element_type=F32,
                           precision=precision)


def _sigmoid(x):
    return 1.0 / (1.0 + jnp.exp(-x))


def _softplus(x):
    return jnp.maximum(x, 0.0) + jnp.log1p(jnp.exp(-jnp.abs(x)))


def _norm_kernel(x_ref, g_ref, o_ref):
    _norm_rows_into(o_ref, x_ref, g_ref)


def _norm(x, gain, out_dtype=F32, tm=256):
    m, d = x.shape
    tm = min(tm, m)
    return pl.pallas_call(
        _norm_kernel, out_shape=jax.ShapeDtypeStruct((m, d), out_dtype),
        grid=(m // tm,),
        in_specs=[pl.BlockSpec((tm, d), lambda i: (i, 0)), pl.BlockSpec((1, d), lambda i: (0, 0))],
        out_specs=pl.BlockSpec((tm, d), lambda i: (i, 0)),
        compiler_params=_cp("parallel"), name="rmsnorm")(x, gain.reshape(1, d))


def _norm_matmul_kernel(x_ref, g_ref, w_ref, o_ref, hn_ref):
    @pl.when(pl.program_id(1) == 0)
    def _():
        _norm_rows_into(hn_ref, x_ref, g_ref)

    o_ref[...] = _dot(hn_ref[...], w_ref[...]).astype(o_ref.dtype)


def _norm_matmul(x, gain, w, out_dtype, tm=512, tn=512):
    m, d = x.shape
    n = w.shape[1]
    tm, tn = min(tm, m), min(tn, n)
    return pl.pallas_call(
        _norm_matmul_kernel, out_shape=jax.ShapeDtypeStruct((m, n), out_dtype),
        grid=(m // tm, n // tn),
        in_specs=[pl.BlockSpec((tm, d), lambda i, j: (i, 0)),
                  pl.BlockSpec((1, d), lambda i, j: (0, 0)),
                  pl.BlockSpec((d, tn), lambda i, j: (0, j))],
        out_specs=pl.BlockSpec((tm, tn), lambda i, j: (i, j)),
        scratch_shapes=[pltpu.VMEM((tm, d), BF16)],
        compiler_params=_cp("parallel", "arbitrary"), name="norm_matmul")(x, gain.reshape(1, d), w)


def _matmul_norm_res_kernel(*refs, glu, tn):
    if glu:
        a_ref, w_ref, w2_ref, res_ref, g_ref, o_ref = refs
    else:
        a_ref, w_ref, res_ref, g_ref, o_ref = refs
    j = pl.program_id(1)
    a = a_ref[...]
    y = _dot(a, w_ref[...])
    if glu:
        y = y * _sigmoid(_dot(a, w2_ref[...]))
    o_ref[:, pl.ds(pl.multiple_of(j * tn, tn), tn)] = y

    @pl.when(j == pl.num_programs(1) - 1)
    def _():
        _add_norm_rows_inplace(o_ref, res_ref, g_ref)


def _matmul_norm_res(a, ws, res, gain, tm=512):
    m, k = a.shape
    n = ws[0].shape[1]
    glu = len(ws) == 2
    tm, tn = min(tm, m), min(256 if glu else 512, n)
    w_specs = [pl.BlockSpec((k, tn), lambda i, j: (0, j)) for _ in ws]
    return pl.pallas_call(
        functools.partial(_matmul_norm_res_kernel, glu=glu, tn=tn),
        out_shape=jax.ShapeDtypeStruct((m, n), F32),
        grid=(m // tm, n // tn),
        in_specs=[pl.BlockSpec((tm, k), lambda i, j: (i, 0))] + w_specs + [
            pl.BlockSpec((tm, n), lambda i, j: (i, 0)),
            pl.BlockSpec((1, n), lambda i, j: (0, 0))],
        out_specs=pl.BlockSpec((tm, n), lambda i, j: (i, 0)),
        compiler_params=_cp("parallel", "arbitrary"),
        name="glu_norm_res" if glu else "matmul_norm_res")(a, *ws, res, gain.reshape(1, n))


def _xa_kernel(x_ref, g_in_ref, wq_ref, k_ref, v_ref, wo_ref, g_out_ref, g_next_ref, o_ref, hn_out_ref,
               hn_ref):
    _norm_rows_into(hn_ref, x_ref, g_in_ref)
    q = _dot(hn_ref[...], wq_ref[...]) * (XA_HEAD_DIM ** -0.5)
    heads = []
    for h in range(XA_HEADS):
        sl = slice(h * XA_HEAD_DIM, (h + 1) * XA_HEAD_DIM)
        s = _dot_nt(q[:, sl].astype(BF16), k_ref[0, :, sl])
        s = s - jnp.max(s, axis=-1, keepdims=True)
        p = jnp.exp(s)
        p = p / jnp.sum(p, axis=-1, keepdims=True)
        heads.append(_dot(p.astype(BF16), v_ref[0, :, sl]))
    o = jnp.concatenate(heads, axis=-1).astype(BF16)
    o_ref[...] = _dot(o, wo_ref[...])

    def body(rs, c):
        x_new = x_ref[rs, :] + _rms(o_ref[rs, :], g_out_ref[...])
        o_ref[rs, :] = x_new
        hn_out_ref[rs, :] = _rms(x_new, g_next_ref[...]).astype(hn_out_ref.dtype)
        return c
    _for_row_blocks(o_ref.shape[0], body)


def _cross_attention(x, g_in, wq, mem_kv, wo, g_out, g_next, seq, tm=256):
    m, d = x.shape
    xw = wq.shape[1]
    n_mem = mem_kv.shape[1]
    tm = min(tm, seq)
    tiles_per_seq = seq // tm
    return pl.pallas_call(
        _xa_kernel, out_shape=[jax.ShapeDtypeStruct((m, d), F32), jax.ShapeDtypeStruct((m, d), BF16)],
        grid=(m // tm,),
        in_specs=[pl.BlockSpec((tm, d), lambda i: (i, 0)),
                  pl.BlockSpec((1, d), lambda i: (0, 0)),
                  pl.BlockSpec((d, xw), lambda i: (0, 0)),
                  pl.BlockSpec((1, n_mem, xw), lambda i: (i // tiles_per_seq, 0, 0)),
                  pl.BlockSpec((1, n_mem, xw), lambda i: (i // tiles_per_seq, 0, 1)),
                  pl.BlockSpec((xw, d), lambda i: (0, 0)),
                  pl.BlockSpec((1, d), lambda i: (0, 0)),
                  pl.BlockSpec((1, d), lambda i: (0, 0))],
        out_specs=[pl.BlockSpec((tm, d), lambda i: (i, 0)), pl.BlockSpec((tm, d), lambda i: (i, 0))],
        scratch_shapes=[pltpu.VMEM((tm, d), BF16)],
        compiler_params=_cp("parallel"), name="cross_attention")(
            x, g_in.reshape(1, d), wq, mem_kv, mem_kv, wo, g_out.reshape(1, d), g_next.reshape(1, d))


def _matmul_kernel(a_ref, w_ref, o_ref, *, act):
    y = _dot(a_ref[...], w_ref[...])
    if act == "relu2":
        y = jnp.maximum(y, 0.0)
        y = y * y
    o_ref[...] = y.astype(o_ref.dtype)


def _matmul(a, w, out_dtype, act=None, tm=2048, tn=512):
    m, k = a.shape
    n = w.shape[1]
    tm, tn = min(tm, m), min(tn, n)
    return pl.pallas_call(
        functools.partial(_matmul_kernel, act=act),
        out_shape=jax.ShapeDtypeStruct((m, n), out_dtype),
        grid=(m // tm, n // tn),
        in_specs=[pl.BlockSpec((tm, k), lambda i, j: (i, 0)),
                  pl.BlockSpec((k, tn), lambda i, j: (0, j))],
        out_specs=pl.BlockSpec((tm, tn), lambda i, j: (i, j)),
        compiler_params=_cp("parallel", "arbitrary"), name="matmul")(a, w)


def _matmul_acc_kernel(a_ref, w_ref, o_ref, *, ts):
    @pl.when(pl.program_id(2) == 0)
    def _():
        o_ref[...] = jnp.zeros_like(o_ref)

    a = a_ref[...]
    for n in range(o_ref.shape[1] // ts):
        sl = slice(n * ts, (n + 1) * ts)
        o_ref[:, sl] += _dot(a, w_ref[:, sl])


def _matmul_acc(a, w, tm=2048, tn=1024, tk=2048):
    m, k = a.shape
    n = w.shape[1]
    tm, tn, tk = min(tm, m), min(tn, n), min(tk, k)
    return pl.pallas_call(
        functools.partial(_matmul_acc_kernel, ts=min(256, tn)),
        out_shape=jax.ShapeDtypeStruct((m, n), F32),
        grid=(m // tm, n // tn, k // tk),
        in_specs=[pl.BlockSpec((tm, tk), lambda i, j, kk: (i, kk)),
                  pl.BlockSpec((tk, tn), lambda i, j, kk: (kk, j))],
        out_specs=pl.BlockSpec((tm, tn), lambda i, j, kk: (i, j)),
        compiler_params=_cp("parallel", "parallel", "arbitrary"), name="matmul_acc")(a, w)


def _add_norm_kernel(x_ref, m_ref, gp_ref, gn_ref, o_ref, *hn_refs):
    def body(rs, c):
        x_new = x_ref[rs, :] + _rms(m_ref[rs, :], gp_ref[...])
        o_ref[rs, :] = x_new
        for hn_ref in hn_refs:
            hn_ref[rs, :] = _rms(x_new, gn_ref[...]).astype(hn_ref.dtype)
        return c
    _for_row_blocks(x_ref.shape[0], body)


def _add_norm(x, m_branch, g_post, g_next, tm=256):
    m, d = x.shape
    tm = min(tm, m)
    row = pl.BlockSpec((tm, d), lambda i: (i, 0))
    vec = pl.BlockSpec((1, d), lambda i: (0, 0))
    emit_hn = g_next is not None
    out_shape = [jax.ShapeDtypeStruct((m, d), F32)] + ([jax.ShapeDtypeStruct((m, d), BF16)] if emit_hn else [])
    outs = pl.pallas_call(
        _add_norm_kernel, out_shape=out_shape, grid=(m // tm,),
        in_specs=[row, row, vec, vec], out_specs=[row] * len(out_shape),
        compiler_params=_cp("parallel"), name="add_norm")(
            x, m_branch, g_post.reshape(1, d), (g_next if emit_hn else g_post).reshape(1, d))
    return (outs[0], outs[1]) if emit_hn else (outs[0], None)


def _gla_kernel(q_ref, k_ref, v_ref, g_ref, gk_ref, wup_ref, bup_ref, onorm_ref, tril_ref,
                o_ref, st_ref):
    @pl.when(pl.program_id(2) == 0)
    def _():
        st_ref[...] = jnp.zeros_like(st_ref)

    dk = q_ref.shape[1]
    z = _dot(gk_ref[...], wup_ref[...], HI) + bup_ref[...]
    log_alpha = (jnp.minimum(z, 0.0) - jnp.log1p(jnp.exp(-jnp.abs(z)))) / GLA_GATE_NORMALIZER
    cum = _dot(tril_ref[...], log_alpha, HI)
    cum_last = cum[CHUNK - 1:CHUNK, :]
    k_dec = (k_ref[...].astype(F32) * jnp.exp(cum_last - cum)).astype(BF16)
    st = st_ref[...] * jnp.exp(cum_last) + _dot_tn(v_ref[...], k_dec)
    st_ref[...] = st
    q = (q_ref[...].astype(F32) * dk ** -0.5).astype(BF16)
    o = _dot_nt(q, st.astype(BF16))
    g = g_ref[...].astype(F32)
    o_ref[...] = (_rms(o, onorm_ref[...]) * (g * _sigmoid(g))).astype(o_ref.dtype)


def _gla_scan(proj, gk_lo, w_up, b_up, o_norm, bsz, seq):
    m = proj.shape[0]
    dk_all = w_up.shape[1]
    dkh = dk_all // GLA_HEADS
    dv_all = (proj.shape[1] - 2 * dk_all) // 2
    dvh = dv_all // GLA_HEADS
    nc = seq // CHUNK
    rank_pad = gk_lo.shape[1]
    tril = jnp.tril(jnp.ones((CHUNK, CHUNK), F32))
    row = lambda b, h, c: b * nc + c
    k_off = dk_all // dkh
    v_off = 2 * dk_all // dvh
    g_off = v_off + dv_all // dvh
    return pl.pallas_call(
        _gla_kernel, out_shape=jax.ShapeDtypeStruct((m, dv_all), BF16),
        grid=(bsz, GLA_HEADS, nc),
        in_specs=[pl.BlockSpec((CHUNK, dkh), lambda b, h, c: (row(b, h, c), h)),
                  pl.BlockSpec((CHUNK, dkh), lambda b, h, c: (row(b, h, c), k_off + h)),
                  pl.BlockSpec((CHUNK, dvh), lambda b, h, c: (row(b, h, c), v_off + h)),
                  pl.BlockSpec((CHUNK, dvh), lambda b, h, c: (row(b, h, c), g_off + h)),
                  pl.BlockSpec((CHUNK, rank_pad), lambda b, h, c: (row(b, h, c), 0)),
                  pl.BlockSpec((rank_pad, dkh), lambda b, h, c: (0, h)),
                  pl.BlockSpec((1, dkh), lambda b, h, c: (0, h)),
                  pl.BlockSpec((1, dvh), lambda b, h, c: (0, 0)),
                  pl.BlockSpec((CHUNK, CHUNK), lambda b, h, c: (0, 0))],
        out_specs=pl.BlockSpec((CHUNK, dvh), lambda b, h, c: (row(b, h, c), h)),
        scratch_shapes=[pltpu.VMEM((dvh, dkh), F32)],
        compiler_params=_cp("parallel", "parallel", "arbitrary"), name="gla_scan")(
            proj, proj, proj, proj, gk_lo, w_up, b_up.reshape(1, dk_all), o_norm.reshape(1, dvh), tril)


def _shiftmix_matmul_kernel(hn_ref, prev_ref, mix_ref, w_ref, o_ref, xm_ref, *, tiles_per_seq):
    @pl.when(pl.program_id(2) == 0)
    def _():
        mix = mix_ref[0]
        prev_rows = prev_ref.shape[0]
        last = prev_ref[...].astype(F32)[prev_rows - 1:prev_rows, :]
        first_tile = pl.program_id(0) % tiles_per_seq == 0
        last = jnp.where(first_tile, 0.0, last)

        def body(rs, last):
            hn = hn_ref[rs, :].astype(F32)
            rows = lax.broadcasted_iota(jnp.int32, hn.shape, 0)
            shifted = jnp.where(rows == 0, last, pltpu.roll(hn, 1, axis=0))
            xm_ref[rs, :] = (hn + (shifted - hn) * mix).astype(BF16)
            return hn[hn.shape[0] - 1:, :]

        _for_row_blocks(hn_ref.shape[0], body, last)

    o_ref[0] = _dot(xm_ref[...], w_ref[0]).astype(o_ref.dtype)


def _shiftmix_matmul(hn, mix, w, out_dtype, seq, tm=1024, tn=512):
    m, d = hn.shape
    p_cnt, _, n = w.shape
    tm, tn = min(tm, seq), min(tn, n)
    prev_rows = 2 * SUBLANES
    blk = tm // prev_rows
    return pl.pallas_call(
        functools.partial(_shiftmix_matmul_kernel, tiles_per_seq=seq // tm),
        out_shape=jax.ShapeDtypeStruct((p_cnt, m, n), out_dtype),
        grid=(m // tm, p_cnt, n // tn),
        in_specs=[pl.BlockSpec((tm, d), lambda i, p, j: (i, 0)),
                  pl.BlockSpec((prev_rows, d), lambda i, p, j: (jnp.maximum(i * blk - 1, 0), 0)),
                  pl.BlockSpec((1, 1, d), lambda i, p, j: (p, 0, 0)),
                  pl.BlockSpec((1, d, tn), lambda i, p, j: (p, 0, j))],
        out_specs=pl.BlockSpec((1, tm, tn), lambda i, p, j: (p, i, j)),
        scratch_shapes=[pltpu.VMEM((tm, d), BF16)],
        compiler_params=_cp("parallel", "arbitrary", "arbitrary"), name="shiftmix_matmul")(
            hn, hn, mix.reshape(p_cnt, 1, d), w)


def _lowrank_out_kernel(h_ref, w_ref, b_ref, o_ref):
    p = pl.program_id(0)
    h = h_ref[0]
    act = jnp.where(p == 0, jnp.tanh(h), jnp.where(p == 1, h, _sigmoid(h)))
    o_ref[0] = _dot(act.astype(BF16), w_ref[0]) + b_ref[0]


def _lowrank_out(h, w2, bias, tm=512):
    p_cnt, m, r = h.shape
    d = w2.shape[2]
    tm = min(tm, m)
    return pl.pallas_call(
        _lowrank_out_kernel, out_shape=jax.ShapeDtypeStruct((p_cnt, m, d), F32),
        grid=(p_cnt, m // tm),
        in_specs=[pl.BlockSpec((1, tm, r), lambda p, i: (p, i, 0)),
                  pl.BlockSpec((1, r, d), lambda p, i: (p, 0, 0)),
                  pl.BlockSpec((1, 1, d), lambda p, i: (p, 0, 0))],
        out_specs=pl.BlockSpec((1, tm, d), lambda p, i: (p, i, 0)),
        compiler_params=_cp("arbitrary", "arbitrary"), name="lowrank_out")(h, w2, bias)


def _rwkv_kernel(rkv_ref, lr_ref, par_ref, bd_ref, tril_ref, o_ref, st_ref):
    @pl.when(pl.program_id(2) == 0)
    def _():
        st_ref[...] = jnp.zeros_like(st_ref)

    n = RWKV_HEAD
    heads = st_ref.shape[0]
    r = rkv_ref[0].astype(F32)
    k = rkv_ref[1].astype(F32)
    v = rkv_ref[2].astype(F32)
    w_log = -_softplus(-lr_ref[0]) - 0.5
    log_w = -jnp.exp(w_log)
    a = _sigmoid(lr_ref[1])
    gate = lr_ref[2]
    k_k, k_a, r_k = par_ref[0:1, :], par_ref[1:2, :], par_ref[2:3, :]
    lnx_g, lnx_b = par_ref[3:4, :], par_ref[4:5, :]
    bd = bd_ref[...]

    kk = k * k_k
    kk = kk / jnp.maximum(jnp.sqrt(_dot((kk * kk).astype(BF16), bd)), 1e-12)
    k2 = k * (1.0 + (a - 1.0) * k_a)
    cw = _dot(tril_ref[...], log_w, HI)
    cw_last = cw[CHUNK - 1:CHUNK, :]
    e_neg = jnp.exp(-cw)
    e_end = jnp.exp(cw_last - cw)
    a_t = (-kk * jnp.exp(cw - log_w)).astype(BF16)
    r_t = (r * jnp.exp(cw)).astype(BF16)
    b_vec = kk * a
    b_t = (b_vec * e_neg).astype(BF16)
    k_t = (k2 * e_neg).astype(BF16)
    b_w = (b_vec * e_end).astype(BF16)
    k_w = (k2 * e_end).astype(BF16)
    w_end = jnp.exp(cw_last)
    v_b = v.astype(BF16)

    rows = lax.broadcasted_iota(jnp.int32, (CHUNK, 2 * CHUNK), 0)
    cols = lax.broadcasted_iota(jnp.int32, (CHUNK, 2 * CHUNK), 1) % CHUNK
    strict = rows > cols
    incl = rows >= cols
    hs = range(heads)
    sls = [slice(h * n, (h + 1) * n) for h in hs]
    ar = [jnp.concatenate([a_t[:, sl], r_t[:, sl]], axis=0) for sl in sls]
    bk = [jnp.concatenate([b_t[:, sl], k_t[:, sl]], axis=0) for sl in sls]
    s0 = [st_ref[h] for h in hs]
    gram = [_dot_nt(ar[h], bk[h]) for h in hs]
    proj = [_dot_nt(ar[h], s0[h].astype(BF16)) for h in hs]
    v_h = [v_b[:, sl] for sl in sls]
    low = [jnp.where(strict, gram[h][:CHUNK], 0.0).astype(BF16) for h in hs]
    t_row = rows[:, :CHUNK]
    t_col = lax.broadcasted_iota(jnp.int32, (CHUNK, CHUNK), 1)
    eye = (t_row == t_col).astype(F32)

    def lower_left(b):
        return (t_row // (2 * b) == t_col // (2 * b)) & (t_row % (2 * b) >= b) & (t_col % (2 * b) < b)

    nil = [low[h][:, :CHUNK] for h in hs]
    t_inv = [eye + jnp.where(lower_left(1), nil[h], 0).astype(F32) for h in hs]
    b = 2
    while b < CHUNK:
        mask = lower_left(b)
        off = [jnp.where(mask, nil[h], 0) for h in hs]
        t_b = [t_inv[h].astype(BF16) for h in hs]
        right = [_dot(off[h], t_b[h]).astype(BF16) for h in hs]
        t_inv = [t_inv[h] + _dot(t_b[h], right[h]) for h in hs]
        b *= 2
    rhs = [proj[h][:CHUNK] + _dot(low[h][:, CHUNK:], v_h[h]) for h in hs]
    u = [_dot(t_inv[h].astype(BF16), rhs[h].astype(BF16)) for h in hs]
    uv =[jnp.concatenate([u[h].astype(BF16), v_h[h]], axis=0) for h in hs]
    upper = [jnp.where(incl, gram[h][CHUNK:], 0.0).astype(BF16) for h in hs]
    y_heads = [proj[h][CHUNK:] + _dot(upper[h], uv[h]) for h in hs]
    for h in hs:
        bkw = jnp.concatenate([b_w[:, sls[h]], k_w[:, sls[h]]], axis=0)
        st_ref[h] = s0[h] * w_end[:, sls[h]] + _dot_tn(uv[h], bkw)

    y = jnp.concatenate(y_heads, axis=1)
    inv_n = 1.0 / n
    mu = _dot(y.astype(BF16), bd) * inv_n
    yc = y - mu
    var = _dot((yc * yc).astype(BF16), bd) * inv_n
    y = yc * lax.rsqrt(var + RWKV_LNX_EPS) * lnx_g + lnx_b
    bonus = _dot((r * k2 * r_k).astype(BF16), bd) * v
    o_ref[...] = ((y + bonus) * gate).astype(o_ref.dtype)


def _rwkv_scan(rkv, lr, params, bsz, seq, heads_per_step=8):
    _, m, d = rkv.shape
    n_heads = d // RWKV_HEAD
    hg = min(heads_per_step, n_heads)
    w = hg * RWKV_HEAD
    nc = seq // CHUNK
    lane_head = jnp.arange(w) // RWKV_HEAD
    bd = (lane_head[:, None] == lane_head[None, :]).astype(BF16)
    tril = jnp.tril(jnp.ones((CHUNK, CHUNK), F32))
    n_par = params.shape[0]
    return pl.pallas_call(
        _rwkv_kernel, out_shape=jax.ShapeDtypeStruct((m, d), BF16),
        grid=(bsz, n_heads // hg, nc),
        in_specs=[pl.BlockSpec((3, CHUNK, w), lambda b, g, c: (0, b * nc + c, g)),
                  pl.BlockSpec((3, CHUNK, w), lambda b, g, c: (0, b * nc + c, g)),
                  pl.BlockSpec((n_par, w), lambda b, g, c: (0, g)),
                  pl.BlockSpec((w, w), lambda b, g, c: (0, 0)),
                  pl.BlockSpec((CHUNK, CHUNK), lambda b, g, c: (0, 0))],
        out_specs=pl.BlockSpec((CHUNK, w), lambda b, g, c: (b * nc + c, g)),
        scratch_shapes=[pltpu.VMEM((hg, RWKV_HEAD, RWKV_HEAD), F32)],
        compiler_params=_cp("parallel", "parallel", "arbitrary"), name="rwkv_scan")(
            rkv, lr, params, bd, tril)


def _s5_kernel(h_ref, wb_ref, wc_ref, coef_ref, d_ref, o_ref, x_ref, carry_ref):
    @pl.when(pl.program_id(2) == 0)
    def _():
        carry_ref[...] = jnp.zeros_like(carry_ref)

    tt = h_ref.shape[0]
    ns = carry_ref.shape[2]
    h = h_ref[...]
    x_ref[...] = _dot(h, wb_ref[0])

    def block(i, carry):
        pr, pi = carry
        r0 = pl.multiple_of(i * SUBLANES, SUBLANES)
        xr = x_ref[pl.ds(r0, SUBLANES), 0:ns]
        xi = x_ref[pl.ds(r0, SUBLANES), ns:2 * ns]
        for s_idx, shift in enumerate((1, 2, 4)):
            cr = coef_ref[0, 2 * s_idx]
            ci = coef_ref[0, 2 * s_idx + 1]
            sr = pltpu.roll(xr, shift, axis=0)
            si = pltpu.roll(xi, shift, axis=0)
            xr, xi = xr + cr * sr - ci * si, xi + cr * si + ci * sr
        cr = coef_ref[0, 6]
        ci = coef_ref[0, 7]
        xr, xi = xr + cr * pr - ci * pi, xi + cr * pi + ci * pr
        x_ref[pl.ds(r0, SUBLANES), 0:ns] = xr
        x_ref[pl.ds(r0, SUBLANES), ns:2 * ns] = xi
        return (jnp.broadcast_to(xr[SUBLANES - 1:SUBLANES, :], xr.shape),
                jnp.broadcast_to(xi[SUBLANES - 1:SUBLANES, :], xi.shape))

    pr0 = jnp.broadcast_to(carry_ref[0], (SUBLANES, ns))
    pi0 = jnp.broadcast_to(carry_ref[1], (SUBLANES, ns))
    pr, pi = lax.fori_loop(0, tt // SUBLANES, block, (pr0, pi0))
    carry_ref[0] = pr[0:1, :]
    carry_ref[1] = pi[0:1, :]

    y = _dot(x_ref[...].astype(BF16), wc_ref[0]) + d_ref[...] * h.astype(F32)
    o_ref[...] = (0.5 * y * (1.0 + lax.erf(y * (2.0 ** -0.5)))).astype(o_ref.dtype)


def _s5_scan(hn, lam_re, lam_im, log_dt, b_re, b_im, c_re, c_im, d_skip, bsz, seq, tt=256):
    m, d = hn.shape
    n_groups, n_state, grp = b_re.shape
    sg = min(S5_SLAB_GROUPS, n_groups)
    slabs = n_groups // sg
    ch = sg * grp
    ns = sg * n_state
    tt = min(tt, seq)

    lr = jnp.minimum(lam_re.astype(F32), -1e-4)
    li = lam_im.astype(F32)
    dt = jnp.exp(log_dt.astype(F32))[:, None]
    mag = jnp.exp(lr * dt)
    ab_re = mag * jnp.cos(li * dt)
    ab_im = mag * jnp.sin(li * dt)
    den = lr * lr + li * li
    nr, ni = ab_re - 1.0, ab_im
    z_re = (nr * lr + ni * li) / den
    z_im = (ni * lr - nr * li) / den
    bb_re = z_re[..., None] * b_re - z_im[..., None] * b_im
    bb_im = z_re[..., None] * b_im + z_im[..., None] * b_re

    eye = jnp.eye(sg, dtype=F32)

    def block_diag_in(bb):
        t = bb.reshape(slabs, sg, n_state, grp)
        return jnp.einsum('sgph,gk->sghkp', t, eye).reshape(slabs, ch, ns)

    def block_diag_out(cc):
        t = cc.reshape(slabs, sg, grp, n_state)
        return jnp.einsum('sgqp,gk->sgpkq', t, eye).reshape(slabs, ns, ch)

    w_b = jnp.concatenate([block_diag_in(bb_re), block_diag_in(bb_im)], axis=2).astype(BF16)
    w_c = jnp.concatenate([block_diag_out(c_re.astype(F32)), -block_diag_out(c_im.astype(F32))],
                          axis=1).astype(BF16)

    def powers(e):
        e = e[:, None, None]
        mg = jnp.exp(lr[None] * dt[None] * e)
        ang = li[None] * dt[None] * e
        re = (mg * jnp.cos(ang)).reshape(-1, slabs, ns)
        im = (mg * jnp.sin(ang)).reshape(-1, slabs, ns)
        return re, im

    row = jnp.arange(SUBLANES, dtype=F32)
    coefs = []
    for shift in (1, 2, 4):
        re, im = powers(jnp.full((SUBLANES,), float(shift), F32))
        mask = (row >= shift)[:, None, None]
        coefs += [jnp.where(mask, re, 0.0), jnp.where(mask, im, 0.0)]
    re, im = powers(row + 1.0)
    coefs += [re, im]
    coef = jnp.stack(coefs, axis=0).transpose(2, 0, 1, 3)

    nt = seq // tt
    return pl.pallas_call(
        _s5_kernel, out_shape=jax.ShapeDtypeStruct((m, d), BF16),
        grid=(bsz, slabs, nt),
        in_specs=[pl.BlockSpec((tt, ch), lambda b, s, t: (b * nt + t, s)),
                  pl.BlockSpec((1, ch, 2 * ns), lambda b, s, t: (s, 0, 0)),
                  pl.BlockSpec((1, 2 * ns, ch), lambda b, s, t: (s, 0, 0)),
                  pl.BlockSpec((1, 8, SUBLANES, ns), lambda b, s, t: (s, 0, 0, 0)),
                  pl.BlockSpec((1, ch), lambda b, s, t: (0, s))],
        out_specs=pl.BlockSpec((tt, ch), lambda b, s, t: (b * nt + t, s)),
        scratch_shapes=[pltpu.VMEM((tt, 2 * ns), F32), pltpu.VMEM((2, 1, ns), F32)],
        compiler_params=_cp("parallel", "parallel", "arbitrary"), name="s5_scan")(
            hn, w_b, w_c, coef, d_skip.reshape(1, d))


def _pad_to(x, axis, size):
    pad = size - x.shape[axis]
    if pad == 0:
        return x
    widths = [(0, 0)] * x.ndim
    widths[axis] = (0, pad)
    return jnp.pad(x, widths)


def _round_up(n, k):
    return (n + k - 1) // k * k


def _gla_layer(x, hn, g_post, w_in, w_gk_up, b_gk_up, o_norm, w_out, bsz, seq):
    rank = w_gk_up.shape[0]
    n_main = w_in.shape[1] - rank
    rank_pad = _round_up(rank, LANES)
    proj = _matmul(hn, w_in[:, :n_main].astype(BF16), BF16)
    gk_lo = _matmul(hn, _pad_to(w_in[:, n_main:], 1, rank_pad).astype(BF16), F32)
    o = _gla_scan(proj, gk_lo, _pad_to(w_gk_up, 0, rank_pad), b_gk_up, o_norm, bsz, seq)
    return _matmul_norm_res(o, [w_out.astype(BF16)], x, g_post)


def _rwkv_layer(x, hn, g_post, mix, w_r, w_k, w_v, w_o, w0, w_w1, w_w2, a0, w_a1, w_a2,
                w_g1, w_g2, k_k, k_a, r_k, lnx_g, lnx_b, bsz, seq):
    d = x.shape[1]
    rkv = _shiftmix_matmul(hn, jnp.stack([mix[0], mix[2], mix[3]]),
                           jnp.stack([w_r, w_k, w_v]).astype(BF16), BF16, seq)
    rank_pad = _round_up(max(w_w1.shape[1], w_a1.shape[1], w_g1.shape[1]), LANES)
    w1 = jnp.stack([_pad_to(w, 1, rank_pad) for w in (w_w1, w_a1, w_g1)]).astype(BF16)
    w2 = jnp.stack([_pad_to(w, 0, rank_pad) for w in (w_w2, w_a2, w_g2)]).astype(BF16)
    low = _shiftmix_matmul(hn, jnp.stack([mix[1], mix[4], mix[5]]), w1, F32, seq)
    bias = jnp.stack([w0, a0, jnp.zeros_like(w0)]).reshape(3, 1, d)
    lr = _lowrank_out(low, w2, bias)
    params = _pad_to(jnp.stack([k_k, k_a, r_k, lnx_g, lnx_b]), 0, SUBLANES)
    y = _rwkv_scan(rkv, lr, params, bsz, seq)
    return _matmul_norm_res(y, [w_o.astype(BF16)], x, g_post)


def _s5_layer(x, hn, g_post, lam_re, lam_im, log_dt, b_re, b_im, c_re, c_im, d_skip,
              w_glu1, w_glu2, bsz, seq):
    y = _s5_scan(hn, lam_re, lam_im, log_dt, b_re, b_im, c_re, c_im, d_skip, bsz, seq)
    return _matmul_norm_res(y, [w_glu1.astype(BF16), w_glu2.astype(BF16)], x, g_post)


def kernel(x, mem, norm_gains, mem_norm, mem_w_kv, xa_wq, xa_wo, mlp_w1, mlp_w2, gla_w_in, gla_w_gk_up, gla_b_gk_up, gla_o_norm, gla_w_out, rwkv_mix, rwkv_w_r, rwkv_w_k, rwkv_w_v, rwkv_w_o, rwkv_w0, rwkv_w_w1, rwkv_w_w2, rwkv_a0, rwkv_w_a1, rwkv_w_a2, rwkv_w_g1, rwkv_w_g2, rwkv_k_k, rwkv_k_a, rwkv_r_k, rwkv_lnx_g, rwkv_lnx_b, s5_lam_re, s5_lam_im, s5_log_dt, s5_b_re, s5_b_im, s5_c_re, s5_c_im, s5_d, s5_w_glu1, s5_w_glu2):
    bsz, seq, d = x.shape
    n_mem = mem.shape[1]
    depth = norm_gains.shape[0]
    assert seq % CHUNK == 0 and xa_wq.shape[2] == XA_HEADS * XA_HEAD_DIM

    mem_kv = _norm_matmul(mem.reshape(bsz * n_mem, d), mem_norm, mem_w_kv.astype(BF16), BF16)
    mem_kv = mem_kv.reshape(bsz, n_mem, mem_w_kv.shape[1])

    x = x.reshape(bsz * seq, d)
    hn = _norm(x, norm_gains[0, 0], BF16)
    for i in range(depth):
        kind, j = i % N_MIXERS, i // N_MIXERS
        g = norm_gains[i]
        if kind == 0:
            x = _gla_layer(x, hn, g[1], gla_w_in[j], gla_w_gk_up[j], gla_b_gk_up[j],
                           gla_o_norm[j], gla_w_out[j], bsz, seq)
        elif kind == 1:
            x = _rwkv_layer(x, hn, g[1], rwkv_mix[j], rwkv_w_r[j], rwkv_w_k[j], rwkv_w_v[j],
                            rwkv_w_o[j], rwkv_w0[j], rwkv_w_w1[j], rwkv_w_w2[j], rwkv_a0[j],
                            rwkv_w_a1[j], rwkv_w_a2[j], rwkv_w_g1[j], rwkv_w_g2[j], rwkv_k_k[j],
                            rwkv_k_a[j], rwkv_r_k[j], rwkv_lnx_g[j], rwkv_lnx_b[j], bsz, seq)
        else:
            x = _s5_layer(x, hn, g[1], s5_lam_re[j], s5_lam_im[j], s5_log_dt[j], s5_b_re[j],
                          s5_b_im[j], s5_c_re[j], s5_c_im[j], s5_d[j], s5_w_glu1[j], s5_w_glu2[j],
                          bsz, seq)
        x, hn = _cross_attention(x, g[2], xa_wq[i].astype(BF16), mem_kv, xa_wo[i].astype(BF16), g[3],
                                 g[4], seq)
        hidden = _matmul(hn, mlp_w1[i].astype(BF16), BF16, act="relu2")
        branch = _matmul_acc(hidden, mlp_w2[i].astype(BF16))
        x, hn = _add_norm(x, branch, g[5], norm_gains[i + 1, 0] if i + 1 < depth else None)
    return x.reshape(bsz, seq, d)
```

```python
import functools

import jax
import jax.numpy as jnp
from jax import lax
from jax.experimental import pallas as pl
from jax.experimental.pallas import tpu as pltpu

F32 = jnp.float32
BF16 = jnp.bfloat16
HI = lax.Precision.HIGHEST

NORM_EPS = 1e-6
CHUNK = 64
N_MIXERS = 3
GLA_HEADS = 4
GLA_GATE_NORMALIZER = 16.0
RWKV_HEAD = 64
RWKV_LNX_EPS = 64e-5
XA_HEADS = 4
XA_HEAD_DIM = 128
S5_SLAB_GROUPS = 16
SUBLANES = 8
LANES = 128
VMEM_LIMIT = 60 * 1024 * 1024


def _cp(*sem):
    return pltpu.CompilerParams(dimension_semantics=sem, vmem_limit_bytes=VMEM_LIMIT)


def _rms(x, gain, eps=NORM_EPS):
    ms = jnp.mean(x * x, axis=-1, keepdims=True)
    return x * lax.rsqrt(ms + eps) * gain


def _stat_scratch(rows):
    return pltpu.VMEM((rows, LANES), F32)


def _for_row_blocks(n_rows, body, carry=None, rows=2 * SUBLANES, unroll=1):
    def step(i, c):
        return body(pl.ds(pl.multiple_of(i * rows, rows), rows), c)
    return lax.fori_loop(0, n_rows // rows, step, carry, unroll=unroll)


def _sumsq_lanes(x):
    acc = x[:, 0:LANES] * x[:, 0:LANES]
    for j in range(1, x.shape[1] // LANES):
        blk = x[:, j * LANES:(j + 1) * LANES]
        acc = acc + blk * blk
    return acc


def _finish_scales(s_ref, d):
    ms = jnp.sum(s_ref[...], axis=-1, keepdims=True) * (1.0 / d)
    s_ref[...] = jnp.broadcast_to(lax.rsqrt(ms + NORM_EPS), s_ref.shape)


def _row_scales_into(s_ref, x_ref):
    def body(rs, c):
        s_ref[rs, :] = _sumsq_lanes(x_ref[rs, :])
        return c
    _for_row_blocks(x_ref.shape[0], body, rows=SUBLANES, unroll=2)
    _finish_scales(s_ref, x_ref.shape[1])


def _scaled(x, scale, gain):
    return x * pltpu.repeat(scale, x.shape[1] // LANES, axis=1) * gain


def _norm_rows_into(dst_ref, x_ref, g_ref, s_ref):
    _row_scales_into(s_ref, x_ref)

    def body(rs, c):
        dst_ref[rs, :] = _scaled(x_ref[rs, :], s_ref[rs, :], g_ref[...]).astype(dst_ref.dtype)
        return c
    _for_row_blocks(x_ref.shape[0], body, unroll=2)


def _add_norm_rows(o_ref, m_ref, res_ref, g_ref, s_ref, hn_ref=None, g_next_ref=None):
    _row_scales_into(s_ref, m_ref)

    def body(rs, c):
        x_new = res_ref[rs, :] + _scaled(m_ref[rs, :], s_ref[rs, :], g_ref[...])
        o_ref[rs, :] = x_new
        if hn_ref is not None:
            s_ref[rs, :] = _sumsq_lanes(x_new)
        return c
    _for_row_blocks(o_ref.shape[0], body, rows=SUBLANES, unroll=2)
    if hn_ref is not None:
        _finish_scales(s_ref, o_ref.shape[1])

        def body2(rs, c):
            hn_ref[rs, :] = _scaled(o_ref[rs, :], s_ref[rs, :], g_next_ref[...]).astype(hn_ref.dtype)
            return c
        _for_row_blocks(o_ref.shape[0], body2, unroll=2)


def _dot(a, b, precision=None):
    return jnp.dot(a, b, preferred_element_type=F32, precision=precision)


def _dot_nt(a, b, precision=None):
    return lax.dot_general(a, b, (((1,), (1,)), ((), ())), preferred_element_type=F32,
                           precision=precision)


def _dot_tn(a, b, precision=None):
    return lax.dot_general(a, b, (((0,), (0,)), ((), ())), preferred_element_type=F32,
                           precision=precision)


def _sigmoid(x):
    return 1.0 / (1.0 + jnp.exp(-x))


def _softplus(x):
    return jnp.maximum(x, 0.0) + jnp.log1p(jnp.exp(-jnp.abs(x)))


def _norm_kernel(x_ref, g_ref, o_ref, s_ref):
    _norm_rows_into(o_ref, x_ref, g_ref, s_ref)


def _norm(x, gain, out_dtype=F32, tm=256):
    m, d = x.shape
    tm = min(tm, m)
    return pl.pallas_call(
        _norm_kernel, out_shape=jax.ShapeDtypeStruct((m, d), out_dtype),
        grid=(m // tm,),
        in_specs=[pl.BlockSpec((tm, d), lambda i: (i, 0)), pl.BlockSpec((1, d), lambda i: (0, 0))],
        out_specs=pl.BlockSpec((tm, d), lambda i: (i, 0)),
        scratch_shapes=[_stat_scratch(tm)],
        compiler_params=_cp("parallel"), name="rmsnorm")(x, gain.reshape(1, d))


def _norm_matmul_kernel(x_ref, g_ref, w_ref, o_ref, hn_ref, s_ref):
    @pl.when(pl.program_id(1) == 0)
    def _():
        _norm_rows_into(hn_ref, x_ref, g_ref, s_ref)

    o_ref[...] = _dot(hn_ref[...], w_ref[...]).astype(o_ref.dtype)


def _norm_matmul(x, gain, w, out_dtype, tm=512, tn=512):
    m, d = x.shape
    n = w.shape[1]
    tm, tn = min(tm, m), min(tn, n)
    return pl.pallas_call(
        _norm_matmul_kernel, out_shape=jax.ShapeDtypeStruct((m, n), out_dtype),
        grid=(m // tm, n // tn),
        in_specs=[pl.BlockSpec((tm, d), lambda i, j: (i, 0)),
                  pl.BlockSpec((1, d), lambda i, j: (0, 0)),
                  pl.BlockSpec((d, tn), lambda i, j: (0, j))],
        out_specs=pl.BlockSpec((tm, tn), lambda i, j: (i, j)),
        scratch_shapes=[pltpu.VMEM((tm, d), BF16), _stat_scratch(tm)],
        compiler_params=_cp("parallel", "arbitrary"), name="norm_matmul")(x, gain.reshape(1, d), w)


def _matmul_norm_res_kernel(*refs, glu, tn):
    if glu:
        a_ref, w_ref, w2_ref, res_ref, g_ref, o_ref, s_ref = refs
    else:
        a_ref, w_ref, res_ref, g_ref, o_ref, s_ref = refs
    j = pl.program_id(1)
    a = a_ref[...]
    y = _dot(a, w_ref[...])
    if glu:
        y = y * _sigmoid(_dot(a, w2_ref[...]))
    o_ref[:, pl.ds(pl.multiple_of(j * tn, tn), tn)] = y

    @pl.when(j == pl.num_programs(1) - 1)
    def _():
        _add_norm_rows(o_ref, o_ref, res_ref, g_ref, s_ref)


def _matmul_norm_res(a, ws, res, gain, tm=512):
    m, k = a.shape
    n = ws[0].shape[1]
    glu = len(ws) == 2
    tm, tn = min(tm, m), min(256 if glu else 512, n)
    w_specs = [pl.BlockSpec((k, tn), lambda i, j: (0, j)) for _ in ws]
    return pl.pallas_call(
        functools.partial(_matmul_norm_res_kernel, glu=glu, tn=tn),
        out_shape=jax.ShapeDtypeStruct((m, n), F32),
        grid=(m // tm, n // tn),
        in_specs=[pl.BlockSpec((tm, k), lambda i, j: (i, 0))] + w_specs + [
            pl.BlockSpec((tm, n), lambda i, j: (i, 0)),
            pl.BlockSpec((1, n), lambda i, j: (0, 0))],
        out_specs=pl.BlockSpec((tm, n), lambda i, j: (i, 0)),
        scratch_shapes=[_stat_scratch(tm)],
        compiler_params=_cp("parallel", "arbitrary"),
        name="glu_norm_res" if glu else "matmul_norm_res")(a, *ws, res, gain.reshape(1, n))


def _xa_kernel(x_ref, g_in_ref, wq_ref, k_ref, v_ref, wo_ref, g_out_ref, g_next_ref, o_ref, hn_out_ref,
               hn_ref, s_ref):
    _norm_rows_into(hn_ref, x_ref, g_in_ref, s_ref)
    q = _dot(hn_ref[...], wq_ref[...]) * (XA_HEAD_DIM ** -0.5)
    heads = []
    for h in range(XA_HEADS):
        sl = slice(h * XA_HEAD_DIM, (h + 1) * XA_HEAD_DIM)
        s = _dot_nt(q[:, sl].astype(BF16), k_ref[0, :, sl])
        s = s - jnp.max(s, axis=-1, keepdims=True)
        p = jnp.exp(s)
        p = p / jnp.sum(p, axis=-1, keepdims=True)
        heads.append(_dot(p.astype(BF16), v_ref[0, :, sl]))
    o = jnp.concatenate(heads, axis=-1).astype(BF16)
    o_ref[...] = _dot(o, wo_ref[...])
    _add_norm_rows(o_ref, o_ref, x_ref, g_out_ref, s_ref, hn_out_ref, g_next_ref)


def _cross_attention(x, g_in, wq, mem_kv, wo, g_out, g_next, seq, tm=256):
    m, d = x.shape
    xw = wq.shape[1]
    n_mem = mem_kv.shape[1]
    tm = min(tm, seq)
    tiles_per_seq = seq // tm
    return pl.pallas_call(
        _xa_kernel, out_shape=[jax.ShapeDtypeStruct((m, d), F32), jax.ShapeDtypeStruct((m, d), BF16)],
        grid=(m // tm,),
        in_specs=[pl.BlockSpec((tm, d), lambda i: (i, 0)),
                  pl.BlockSpec((1, d), lambda i: (0, 0)),
                  pl.BlockSpec((d, xw), lambda i: (0, 0)),
                  pl.BlockSpec((1, n_mem, xw), lambda i: (i // tiles_per_seq, 0, 0)),
                  pl.BlockSpec((1, n_mem, xw), lambda i: (i // tiles_per_seq, 0, 1)),
                  pl.BlockSpec((xw, d), lambda i: (0, 0)),
                  pl.BlockSpec((1, d), lambda i: (0, 0)),
                  pl.BlockSpec((1, d), lambda i: (0, 0))],
        out_specs=[pl.BlockSpec((tm, d), lambda i: (i, 0)), pl.BlockSpec((tm, d), lambda i: (i, 0))],
        scratch_shapes=[pltpu.VMEM((tm, d), BF16), _stat_scratch(tm)],
        compiler_params=_cp("parallel"), name="cross_attention")(
            x, g_in.reshape(1, d), wq, mem_kv, mem_kv, wo, g_out.reshape(1, d), g_next.reshape(1, d))


def _matmul_kernel(a_ref, w_ref, o_ref, *, act):
    y = _dot(a_ref[...], w_ref[...])
    if act == "relu2":
        y = jnp.maximum(y, 0.0)
        y = y * y
    o_ref[...] = y.astype(o_ref.dtype)


def _matmul(a, w, out_dtype, act=None, tm=2048, tn=512):
    m, k = a.shape
    n = w.shape[1]
    tm, tn = min(tm, m), min(tn, n)
    return pl.pallas_call(
        functools.partial(_matmul_kernel, act=act),
        out_shape=jax.ShapeDtypeStruct((m, n), out_dtype),
        grid=(m // tm, n // tn),
        in_specs=[pl.BlockSpec((tm, k), lambda i, j: (i, 0)),
                  pl.BlockSpec((k, tn), lambda i, j: (0, j))],
        out_specs=pl.BlockSpec((tm, tn), lambda i, j: (i, j)),
        compiler_params=_cp("parallel", "arbitrary"), name="matmul")(a, w)


def _matmul_acc_kernel(a_ref, w_ref, o_ref, *, ts):
    @pl.when(pl.program_id(2) == 0)
    def _():
        o_ref[...] = jnp.zeros_like(o_ref)

    a = a_ref[...]
    for n in range(o_ref.shape[1] // ts):
        sl = slice(n * ts, (n + 1) * ts)
        o_ref[:, sl] += _dot(a, w_ref[:, sl])


def _matmul_acc(a, w, tm=2048, tn=1024, tk=2048):
    m, k = a.shape
    n = w.shape[1]
    tm, tn, tk = min(tm, m), min(tn, n), min(tk, k)
    return pl.pallas_call(
        functools.partial(_matmul_acc_kernel, ts=min(256, tn)),
        out_shape=jax.ShapeDtypeStruct((m, n), F32),
        grid=(m // tm, n // tn, k // tk),
        in_specs=[pl.BlockSpec((tm, tk), lambda i, j, kk: (i, kk)),
                  pl.BlockSpec((tk, tn), lambda i, j, kk: (kk, j))],
        out_specs=pl.BlockSpec((tm, tn), lambda i, j, kk: (i, j)),
        compiler_params=_cp("parallel", "parallel", "arbitrary"), name="matmul_acc")(a, w)


def _add_norm_kernel(x_ref, m_ref, gp_ref, gn_ref, o_ref, *rest):
    s_ref = rest[-1]
    hn_ref = rest[0] if len(rest) == 2 else None
    _add_norm_rows(o_ref, m_ref, x_ref, gp_ref, s_ref, hn_ref, gn_ref)


def _add_norm(x, m_branch, g_post, g_next, tm=256):
    m, d = x.shape
    tm = min(tm, m)
    row = pl.BlockSpec((tm, d), lambda i: (i, 0))
    vec = pl.BlockSpec((1, d), lambda i: (0, 0))
    emit_hn = g_next is not None
    out_shape = [jax.ShapeDtypeStruct((m, d), F32)] + ([jax.ShapeDtypeStruct((m, d), BF16)] if emit_hn else [])
    outs = pl.pallas_call(
        _add_norm_kernel, out_shape=out_shape, grid=(m // tm,),
        in_specs=[row, row, vec, vec], out_specs=[row] * len(out_shape),
        scratch_shapes=[_stat_scratch(tm)],
        compiler_params=_cp("parallel"), name="add_norm")(
            x, m_branch, g_post.reshape(1, d), (g_next if emit_hn else g_post).reshape(1, d))
    return (outs[0], outs[1]) if emit_hn else (outs[0], None)


def _gla_kernel(q_ref, k_ref, v_ref, g_ref, gk_ref, wup_ref, bup_ref, onorm_ref, tril_ref,
                o_ref, st_ref):
    @pl.when(pl.program_id(1) == 0)
    def _():
        st_ref[...] = jnp.zeros_like(st_ref)

    nb, _, dk = q_ref.shape
    bs = range(nb)
    gk = gk_ref[...].reshape(nb * CHUNK, gk_ref.shape[2])
    z = _dot(gk, wup_ref[...], HI) + bup_ref[...]
    log_alpha = (jnp.minimum(z, 0.0) - jnp.log1p(jnp.exp(-jnp.abs(z)))) / GLA_GATE_NORMALIZER
    cum_all = _dot(tril_ref[...], log_alpha, HI)
    cum = [cum_all[b * CHUNK:(b + 1) * CHUNK] for b in bs]
    cum_last = [cum[b][CHUNK - 1:CHUNK, :] for b in bs]
    k_dec = [(k_ref[b].astype(F32) * jnp.exp(cum_last[b] - cum[b])).astype(BF16) for b in bs]
    st = [st_ref[b] * jnp.exp(cum_last[b]) + _dot_tn(v_ref[b], k_dec[b]) for b in bs]
    for b in bs:
        st_ref[b] = st[b]
    q = [(q_ref[b].astype(F32) * dk ** -0.5).astype(BF16) for b in bs]
    o = [_dot_nt(q[b], st[b].astype(BF16)) for b in bs]
    for b in bs:
        g = g_ref[b].astype(F32)
        o_ref[b] = (_rms(o[b], onorm_ref[...]) * (g * _sigmoid(g))).astype(o_ref.dtype)


def _gla_scan(proj, gk_lo, w_up, b_up, o_norm, bsz, seq):
    dk_all = w_up.shape[1]
    dkh = dk_all // GLA_HEADS
    dv_all = (proj.shape[1] - 2 * dk_all) // 2
    dvh = dv_all // GLA_HEADS
    nc = seq // CHUNK
    rank_pad = gk_lo.shape[1]
    proj = proj.reshape(bsz, seq, proj.shape[1])
    gk_lo = gk_lo.reshape(bsz, seq, rank_pad)
    tril = jnp.kron(jnp.eye(bsz, dtype=F32), jnp.tril(jnp.ones((CHUNK, CHUNK), F32)))
    k_off = dk_all // dkh
    v_off = 2 * dk_all // dvh
    g_off = v_off + dv_all // dvh
    out = pl.pallas_call(
        _gla_kernel, out_shape=jax.ShapeDtypeStruct((bsz, seq, dv_all), BF16),
        grid=(GLA_HEADS, nc),
        in_specs=[pl.BlockSpec((bsz, CHUNK, dkh), lambda h, c: (0, c, h)),
                  pl.BlockSpec((bsz, CHUNK, dkh), lambda h, c: (0, c, k_off + h)),
                  pl.BlockSpec((bsz, CHUNK, dvh), lambda h, c: (0, c, v_off + h)),
                  pl.BlockSpec((bsz, CHUNK, dvh), lambda h, c: (0, c, g_off + h)),
                  pl.BlockSpec((bsz, CHUNK, rank_pad), lambda h, c: (0, c, 0)),
                  pl.BlockSpec((rank_pad, dkh), lambda h, c: (0, h)),
                  pl.BlockSpec((1, dkh), lambda h, c: (0, h)),
                  pl.BlockSpec((1, dvh), lambda h, c: (0, 0)),
                  pl.BlockSpec((bsz * CHUNK, bsz * CHUNK), lambda h, c: (0, 0))],
        out_specs=pl.BlockSpec((bsz, CHUNK, dvh), lambda h, c: (0, c, h)),
        scratch_shapes=[pltpu.VMEM((bsz, dvh, dkh), F32)],
        compiler_params=_cp("parallel", "arbitrary"), name="gla_scan")(
            proj, proj, proj, proj, gk_lo, w_up, b_up.reshape(1, dk_all), o_norm.reshape(1, dvh), tril)
    return out.reshape(bsz * seq, dv_all)


def _shiftmix_matmul_kernel(hn_ref, prev_ref, mix_ref, w_ref, o_ref, xm_ref, *, tiles_per_seq):
    @pl.when(pl.program_id(2) == 0)
    def _():
        mix = mix_ref[0]
        prev_rows = prev_ref.shape[0]
        last = prev_ref[...].astype(F32)[prev_rows - 1:prev_rows, :]
        first_tile = pl.program_id(0) % tiles_per_seq == 0
        last = jnp.where(first_tile, 0.0, last)

        def body(rs, last):
            hn = hn_ref[rs, :].astype(F32)
            rows = lax.broadcasted_iota(jnp.int32, hn.shape, 0)
            shifted = jnp.where(rows == 0, last, pltpu.roll(hn, 1, axis=0))
            xm_ref[rs, :] = (hn + (shifted - hn) * mix).astype(BF16)
            return hn[hn.shape[0] - 1:, :]

        _for_row_blocks(hn_ref.shape[0], body, last)

    o_ref[0] = _dot(xm_ref[...], w_ref[0]).astype(o_ref.dtype)


def _shiftmix_matmul(hn, mix, w, out_dtype, seq, tm=1024, tn=512):
    m, d = hn.shape
    p_cnt, _, n = w.shape
    tm, tn = min(tm, seq), min(tn, n)
    prev_rows = 2 * SUBLANES
    blk = tm // prev_rows
    return pl.pallas_call(
        functools.partial(_shiftmix_matmul_kernel, tiles_per_seq=seq // tm),
        out_shape=jax.ShapeDtypeStruct((p_cnt, m, n), out_dtype),
        grid=(m // tm, p_cnt, n // tn),
        in_specs=[pl.BlockSpec((tm, d), lambda i, p, j: (i, 0)),
                  pl.BlockSpec((prev_rows, d), lambda i, p, j: (jnp.maximum(i * blk - 1, 0), 0)),
                  pl.BlockSpec((1, 1, d), lambda i, p, j: (p, 0, 0)),
                  pl.BlockSpec((1, d, tn), lambda i, p, j: (p, 0, j))],
        out_specs=pl.BlockSpec((1, tm, tn), lambda i, p, j: (p, i, j)),
        scratch_shapes=[pltpu.VMEM((tm, d), BF16)],
        compiler_params=_cp("parallel", "arbitrary", "arbitrary"), name="shiftmix_matmul")(
            hn, hn, mix.reshape(p_cnt, 1, d), w)


def _lowrank_out_kernel(h_ref, w_ref, b_ref, o_ref):
    p = pl.program_id(0)
    h = h_ref[0]
    act = jnp.where(p == 0, jnp.tanh(h), jnp.where(p == 1, h, _sigmoid(h)))
    o_ref[0] = _dot(act.astype(BF16), w_ref[0]) + b_ref[0]


def _lowrank_out(h, w2, bias, tm=512):
    p_cnt, m, r = h.shape
    d = w2.shape[2]
    tm = min(tm, m)
    return pl.pallas_call(
        _lowrank_out_kernel, out_shape=jax.ShapeDtypeStruct((p_cnt, m, d), F32),
        grid=(p_cnt, m // tm),
        in_specs=[pl.BlockSpec((1, tm, r), lambda p, i: (p, i, 0)),
                  pl.BlockSpec((1, r, d), lambda p, i: (p, 0, 0)),
                  pl.BlockSpec((1, 1, d), lambda p, i: (p, 0, 0))],
        out_specs=pl.BlockSpec((1, tm, d), lambda p, i: (p, i, 0)),
        compiler_params=_cp("arbitrary", "arbitrary"), name="lowrank_out")(h, w2, bias)


def _rwkv_kernel(rkv_ref, lr_ref, par_ref, bd_ref, tril_ref, o_ref, st_ref):
    @pl.when(pl.program_id(2) == 0)
    def _():
        st_ref[...] = jnp.zeros_like(st_ref)

    n = RWKV_HEAD
    heads = st_ref.shape[0]
    r = rkv_ref[0].astype(F32)
    k = rkv_ref[1].astype(F32)
    v = rkv_ref[2].astype(F32)
    w_log = -_softplus(-lr_ref[0]) - 0.5
    log_w = -jnp.exp(w_log)
    a = _sigmoid(lr_ref[1])
    gate = lr_ref[2]
    k_k, k_a, r_k = par_ref[0:1, :], par_ref[1:2, :], par_ref[2:3, :]
    lnx_g, lnx_b = par_ref[3:4, :], par_ref[4:5, :]
    bd = bd_ref[...]
    grp = bd.shape[0]

    def head_sums(x):
        x = x.astype(BF16)
        return jnp.concatenate([_dot(x[:, i:i + grp], bd) for i in range(0, x.shape[1], grp)], axis=1)

    kk = k * k_k
    kk = kk / jnp.maximum(jnp.sqrt(head_sums(kk * kk)), 1e-12)
    k2 = k * (1.0 + (a - 1.0) * k_a)
    cw = _dot(tril_ref[...], log_w, HI)
    cw_last = cw[CHUNK - 1:CHUNK, :]
    e_neg = jnp.exp(-cw)
    e_end = jnp.exp(cw_last - cw)
    a_t = (-kk * jnp.exp(cw - log_w)).astype(BF16)
    r_t = (r * jnp.exp(cw)).astype(BF16)
    b_vec = kk * a
    b_t = (b_vec * e_neg).astype(BF16)
    k_t = (k2 * e_neg).astype(BF16)
    b_w = (b_vec * e_end).astype(BF16)
    k_w = (k2 * e_end).astype(BF16)
    w_end = jnp.exp(cw_last)
    v_b = v.astype(BF16)

    rows = lax.broadcasted_iota(jnp.int32, (CHUNK, 2 * CHUNK), 0)
    cols = lax.broadcasted_iota(jnp.int32, (CHUNK, 2 * CHUNK), 1) % CHUNK
    strict = rows > cols
    incl = rows >= cols
    hs = range(heads)
    sls = [slice(h * n, (h + 1) * n) for h in hs]
    ar = [jnp.concatenate([a_t[:, sl], r_t[:, sl]], axis=0) for sl in sls]
    bk = [jnp.concatenate([b_t[:, sl], k_t[:, sl]], axis=0) for sl in sls]
    s0 = [st_ref[h] for h in hs]
    gram = [_dot_nt(ar[h], bk[h]) for h in hs]
    proj = [_dot_nt(ar[h], s0[h].astype(BF16)) for h in hs]
    v_h = [v_b[:, sl] for sl in sls]
    low = [jnp.where(strict, gram[h][:CHUNK], 0.0).astype(BF16) for h in hs]
    t_row = rows[:, :CHUNK]
    t_col = lax.broadcasted_iota(jnp.int32, (CHUNK, CHUNK), 1)
    eye = (t_row == t_col).astype(F32)

    def lower_left(b):
        return (t_row // (2 * b) == t_col // (2 * b)) & (t_row % (2 * b) >= b) & (t_col % (2 * b) < b)

    nil = [low[h][:, :CHUNK] for h in hs]
    t_inv = [eye + jnp.where(lower_left(1), nil[h], 0).astype(F32) for h in hs]
    b = 2
    while b < CHUNK:
        mask = lower_left(b)
        off = [jnp.where(mask, nil[h], 0) for h in hs]
        t_b = [t_inv[h].astype(BF16) for h in hs]
        right = [_dot(off[h], t_b[h]).astype(BF16) for h in hs]
        t_inv = [t_inv[h] + _dot(t_b[h], right[h]) for h in hs]
        b *= 2
    rhs = [proj[h][:CHUNK] + _dot(low[h][:, CHUNK:], v_h[h]) for h in hs]
    u = [_dot(t_inv[h].astype(BF16), rhs[h].astype(BF16)) for h in hs]
    uv =[jnp.concatenate([u[h].astype(BF16), v_h[h]], axis=0) for h in hs]
    upper = [jnp.where(incl, gram[h][CHUNK:], 0.0).astype(BF16) for h in hs]
    y_heads = [proj[h][CHUNK:] + _dot(upper[h], uv[h]) for h in hs]
    for h in hs:
        bkw = jnp.concatenate([b_w[:, sls[h]], k_w[:, sls[h]]], axis=0)
        st_ref[h] = s0[h] * w_end[:, sls[h]] + _dot_tn(uv[h], bkw)

    y = jnp.concatenate(y_heads, axis=1)
    inv_n = 1.0 / n
    mu = head_sums(y) * inv_n
    yc = y - mu
    var = head_sums(yc * yc) * inv_n
    y = yc * lax.rsqrt(var + RWKV_LNX_EPS) * lnx_g + lnx_b
    bonus = head_sums(r * k2 * r_k) * v
    o_ref[...] = ((y + bonus) * gate).astype(o_ref.dtype)


def _rwkv_scan(rkv, lr, params, bsz, seq, heads_per_step=16):
    _, m, d = rkv.shape
    n_heads = d // RWKV_HEAD
    hg = min(heads_per_step, n_heads)
    w = hg * RWKV_HEAD
    nc = seq // CHUNK
    grp = min(2 * LANES, w)
    lane_head = jnp.arange(grp) // RWKV_HEAD
    bd = (lane_head[:, None] == lane_head[None, :]).astype(BF16)
    tril = jnp.tril(jnp.ones((CHUNK, CHUNK), F32))
    n_par = params.shape[0]
    return pl.pallas_call(
        _rwkv_kernel, out_shape=jax.ShapeDtypeStruct((m, d), BF16),
        grid=(bsz, n_heads // hg, nc),
        in_specs=[pl.BlockSpec((3, CHUNK, w), lambda b, g, c: (0, b * nc + c, g)),
                  pl.BlockSpec((3, CHUNK, w), lambda b, g, c: (0, b * nc + c, g)),
                  pl.BlockSpec((n_par, w), lambda b, g, c: (0, g)),
                  pl.BlockSpec((grp, grp), lambda b, g, c: (0, 0)),
                  pl.BlockSpec((CHUNK, CHUNK), lambda b, g, c: (0, 0))],
        out_specs=pl.BlockSpec((CHUNK, w), lambda b, g, c: (b * nc + c, g)),
        scratch_shapes=[pltpu.VMEM((hg, RWKV_HEAD, RWKV_HEAD), F32)],
        compiler_params=_cp("parallel", "parallel", "arbitrary"), name="rwkv_scan")(
            rkv, lr, params, bd, tril)


def _cmul_add(acc_r, acc_i, cr, ci, xr, xi):
    return acc_r + cr * xr - ci * xi, acc_i + cr * xi + ci * xr


def _s5_kernel(h_ref, perm_ref, unperm_ref, wb_ref, wc_ref, coef_ref, pow_ref, d_ref, o_ref, x_ref, carry_ref):
    @pl.when(pl.program_id(2) == 0)
    def _():
        carry_ref[...] = jnp.zeros_like(carry_ref)

    tt = h_ref.shape[0]
    ns = carry_ref.shape[2]
    seg = tt // SUBLANES
    h = _dot(perm_ref[...], h_ref[...]).astype(BF16)
    x_ref[...] = _dot(h, wb_ref[0])
    a_r, a_i = coef_ref[0, 8], coef_ref[0, 9]

    def step_rows(k, c_r, c_i, x_r, x_i):
        rows = pl.ds(pl.multiple_of(k * SUBLANES, SUBLANES), SUBLANES)
        xr, xi = _cmul_add(x_ref[rows, 0:ns], x_ref[rows, ns:2 * ns], c_r, c_i, x_r, x_i)
        x_ref[rows, 0:ns] = xr
        x_ref[rows, ns:2 * ns] = xi
        return xr, xi

    zeros = jnp.zeros((SUBLANES, ns), F32)
    end_r, end_i = lax.fori_loop(0, seg, lambda k, c: step_rows(k, a_r, a_i, *c), (zeros, zeros))

    for s_idx, shift in enumerate((1, 2, 4)):
        end_r, end_i = _cmul_add(end_r, end_i, coef_ref[0, 2 * s_idx], coef_ref[0, 2 * s_idx + 1],
                                 pltpu.roll(end_r, shift, axis=0), pltpu.roll(end_i, shift, axis=0))
    in_r = jnp.broadcast_to(carry_ref[0], (SUBLANES, ns))
    in_i = jnp.broadcast_to(carry_ref[1], (SUBLANES, ns))
    end_r, end_i = _cmul_add(end_r, end_i, coef_ref[0, 6], coef_ref[0, 7], in_r, in_i)
    carry_ref[0] = end_r[SUBLANES - 1:SUBLANES, :]
    carry_ref[1] = end_i[SUBLANES - 1:SUBLANES, :]
    first = lax.broadcasted_iota(jnp.int32, (SUBLANES, ns), 0) == 0
    in_r = jnp.where(first, in_r, pltpu.roll(end_r, 1, axis=0))
    in_i = jnp.where(first, in_i, pltpu.roll(end_i, 1, axis=0))

    def correct(k, c):
        p_r = jnp.broadcast_to(pow_ref[0, 0, pl.ds(k, 1), :], (SUBLANES, ns))
        p_i = jnp.broadcast_to(pow_ref[0, 1, pl.ds(k, 1), :], (SUBLANES, ns))
        step_rows(k, p_r, p_i, in_r, in_i)
        return c

    lax.fori_loop(0, seg, correct, 0, unroll=2)

    y = _dot(x_ref[...].astype(BF16), wc_ref[0]) + d_ref[...] * h.astype(F32)
    y = (0.5 * y * (1.0 + lax.erf(y * (2.0 ** -0.5)))).astype(BF16)
    o_ref[...] = _dot(unperm_ref[...], y).astype(o_ref.dtype)


def _s5_scan(hn, lam_re, lam_im, log_dt, b_re, b_im, c_re, c_im, d_skip, bsz, seq, tt=1024):
    m, d = hn.shape
    n_groups, n_state, grp = b_re.shape
    sg = min(S5_SLAB_GROUPS, n_groups)
    slabs = n_groups // sg
    ch = sg * grp
    ns = sg * n_state
    tt = min(tt, seq)

    lr = jnp.minimum(lam_re.astype(F32), -1e-4)
    li = lam_im.astype(F32)
    dt = jnp.exp(log_dt.astype(F32))[:, None]
    mag = jnp.exp(lr * dt)
    ab_re = mag * jnp.cos(li * dt)
    ab_im = mag * jnp.sin(li * dt)
    den = lr * lr + li * li
    nr, ni = ab_re - 1.0, ab_im
    z_re = (nr * lr + ni * li) / den
    z_im = (ni * lr - nr * li) / den
    bb_re = z_re[..., None] * b_re - z_im[..., None] * b_im
    bb_im = z_re[..., None] * b_im + z_im[..., None] * b_re

    eye = jnp.eye(sg, dtype=F32)

    def block_diag_in(bb):
        t = bb.reshape(slabs, sg, n_state, grp)
        return jnp.einsum('sgph,gk->sghkp', t, eye).reshape(slabs, ch, ns)

    def block_diag_out(cc):
        t = cc.reshape(slabs, sg, grp, n_state)
        return jnp.einsum('sgqp,gk->sgpkq', t, eye).reshape(slabs, ns, ch)

    w_b = jnp.concatenate([block_diag_in(bb_re), block_diag_in(bb_im)], axis=2).astype(BF16)
    w_c = jnp.concatenate([block_diag_out(c_re.astype(F32)), -block_diag_out(c_im.astype(F32))],
                          axis=1).astype(BF16)

    def powers(e):
        e = e[:, None, None]
        mg = jnp.exp(lr[None] * dt[None] * e)
        ang = li[None] * dt[None] * e
        re = (mg * jnp.cos(ang)).reshape(-1, slabs, ns)
        im = (mg * jnp.sin(ang)).reshape(-1, slabs, ns)
        return re, im

    seg = tt // SUBLANES
    row = jnp.arange(SUBLANES, dtype=F32)
    coefs = []
    for shift in (1, 2, 4):
        re, im = powers(jnp.full((SUBLANES,), float(seg * shift), F32))
        mask = (row >= shift)[:, None, None]
        coefs += [jnp.where(mask, re, 0.0), jnp.where(mask, im, 0.0)]
    coefs += list(powers(seg * (row + 1.0)))
    coefs += list(powers(jnp.ones((SUBLANES,), F32)))
    coef = jnp.stack(coefs, axis=0).transpose(2, 0, 1, 3)
    pw = jnp.stack(powers(jnp.arange(1, seg + 1, dtype=F32)), axis=0).transpose(2, 0, 1, 3)
    r_idx = jnp.arange(tt)
    src_time = (r_idx % SUBLANES) * seg + r_idx // SUBLANES
    perm = (src_time[:, None] == jnp.arange(tt)[None, :]).astype(BF16)

    nt = seq // tt
    return pl.pallas_call(
        _s5_kernel, out_shape=jax.ShapeDtypeStruct((m, d), BF16),
        grid=(bsz, slabs, nt),
        in_specs=[pl.BlockSpec((tt, ch), lambda b, s, t: (b * nt + t, s)),
                  pl.BlockSpec((tt, tt), lambda b, s, t: (0, 0)),
                  pl.BlockSpec((tt, tt), lambda b, s, t: (0, 0)),
                  pl.BlockSpec((1, ch, 2 * ns), lambda b, s, t: (s, 0, 0)),
                  pl.BlockSpec((1, 2 * ns, ch), lambda b, s, t: (s, 0, 0)),
                  pl.BlockSpec((1, 10, SUBLANES, ns), lambda b, s, t: (s, 0, 0, 0)),
                  pl.BlockSpec((1, 2, seg, ns), lambda b, s, t: (s, 0, 0, 0)),
                  pl.BlockSpec((1, ch), lambda b, s, t: (0, s))],
        out_specs=pl.BlockSpec((tt, ch), lambda b, s, t: (b * nt + t, s)),
        scratch_shapes=[pltpu.VMEM((tt, 2 * ns), F32), pltpu.VMEM((2, 1, ns), F32)],
        compiler_params=_cp("parallel", "parallel", "arbitrary"), name="s5_scan")(
            hn, perm, perm.T, w_b, w_c, coef, pw, d_skip.reshape(1, d))


def _pad_to(x, axis, size):
    pad = size - x.shape[axis]
    if pad == 0:
        return x
    widths = [(0, 0)] * x.ndim
    widths[axis] = (0, pad)
    return jnp.pad(x, widths)


def _round_up(n, k):
    return (n + k - 1) // k * k


def _gla_layer(x, hn, g_post, w_in, w_gk_up, b_gk_up, o_norm, w_out, bsz, seq):
    rank = w_gk_up.shape[0]
    n_main = w_in.shape[1] - rank
    rank_pad = _round_up(rank, LANES)
    proj = _matmul(hn, w_in[:, :n_main].astype(BF16), BF16)
    gk_lo = _matmul(hn, _pad_to(w_in[:, n_main:], 1, rank_pad).astype(BF16), F32)
    o = _gla_scan(proj, gk_lo, _pad_to(w_gk_up, 0, rank_pad), b_gk_up, o_norm, bsz, seq)
    return _matmul_norm_res(o, [w_out.astype(BF16)], x, g_post)


def _rwkv_layer(x, hn, g_post, mix, w_r, w_k, w_v, w_o, w0, w_w1, w_w2, a0, w_a1, w_a2,
                w_g1, w_g2, k_k, k_a, r_k, lnx_g, lnx_b, bsz, seq):
    d = x.shape[1]
    rkv = _shiftmix_matmul(hn, jnp.stack([mix[0], mix[2], mix[3]]),
                           jnp.stack([w_r, w_k, w_v]).astype(BF16), BF16, seq)
    rank_pad = _round_up(max(w_w1.shape[1], w_a1.shape[1], w_g1.shape[1]), LANES)
    w1 = jnp.stack([_pad_to(w, 1, rank_pad) for w in (w_w1, w_a1, w_g1)]).astype(BF16)
    w2 = jnp.stack([_pad_to(w, 0, rank_pad) for w in (w_w2, w_a2, w_g2)]).astype(BF16)
    low = _shiftmix_matmul(hn, jnp.stack([mix[1], mix[4], mix[5]]), w1, F32, seq)
    bias = jnp.stack([w0, a0, jnp.zeros_like(w0)]).reshape(3, 1, d)
    lr = _lowrank_out(low, w2, bias)
    params = _pad_to(jnp.stack([k_k, k_a, r_k, lnx_g, lnx_b]), 0, SUBLANES)
    y = _rwkv_scan(rkv, lr, params, bsz, seq)
    return _matmul_norm_res(y, [w_o.astype(BF16)], x, g_post)


def _s5_layer(x, hn, g_post, lam_re, lam_im, log_dt, b_re, b_im, c_re, c_im, d_skip,
              w_glu1, w_glu2, bsz, seq):
    y = _s5_scan(hn, lam_re, lam_im, log_dt, b_re, b_im, c_re, c_im, d_skip, bsz, seq)
    return _matmul_norm_res(y, [w_glu1.astype(BF16), w_glu2.astype(BF16)], x, g_post)


def kernel(x, mem, norm_gains, mem_norm, mem_w_kv, xa_wq, xa_wo, mlp_w1, mlp_w2, gla_w_in, gla_w_gk_up, gla_b_gk_up, gla_o_norm, gla_w_out, rwkv_mix, rwkv_w_r, rwkv_w_k, rwkv_w_v, rwkv_w_o, rwkv_w0, rwkv_w_w1, rwkv_w_w2, rwkv_a0, rwkv_w_a1, rwkv_w_a2, rwkv_w_g1, rwkv_w_g2, rwkv_k_k, rwkv_k_a, rwkv_r_k, rwkv_lnx_g, rwkv_lnx_b, s5_lam_re, s5_lam_im, s5_log_dt, s5_b_re, s5_b_im, s5_c_re, s5_c_im, s5_d, s5_w_glu1, s5_w_glu2):
    bsz, seq, d = x.shape
    n_mem = mem.shape[1]
    depth = norm_gains.shape[0]
    assert seq % CHUNK == 0 and xa_wq.shape[2] == XA_HEADS * XA_HEAD_DIM

    mem_kv = _norm_matmul(mem.reshape(bsz * n_mem, d), mem_norm, mem_w_kv.astype(BF16), BF16)
    mem_kv = mem_kv.reshape(bsz, n_mem, mem_w_kv.shape[1])

    x = x.reshape(bsz * seq, d)
    hn = _norm(x, norm_gains[0, 0], BF16)
    for i in range(depth):
        kind, j = i % N_MIXERS, i // N_MIXERS
        g = norm_gains[i]
        if kind == 0:
            x = _gla_layer(x, hn, g[1], gla_w_in[j], gla_w_gk_up[j], gla_b_gk_up[j],
                           gla_o_norm[j], gla_w_out[j], bsz, seq)
        elif kind == 1:
            x = _rwkv_layer(x, hn, g[1], rwkv_mix[j], rwkv_w_r[j], rwkv_w_k[j], rwkv_w_v[j],
                            rwkv_w_o[j], rwkv_w0[j], rwkv_w_w1[j], rwkv_w_w2[j], rwkv_a0[j],
                            rwkv_w_a1[j], rwkv_w_a2[j], rwkv_w_g1[j], rwkv_w_g2[j], rwkv_k_k[j],
                            rwkv_k_a[j], rwkv_r_k[j], rwkv_lnx_g[j], rwkv_lnx_b[j], bsz, seq)
        else:
            x = _s5_layer(x, hn, g[1], s5_lam_re[j], s5_lam_im[j], s5_log_dt[j], s5_b_re[j],
                          s5_b_im[j], s5_c_re[j], s5_c_im[j], s5_d[j], s5_w_glu1[j], s5_w_glu2[j],
                          bsz, seq)
        x, hn = _cross_attention(x, g[2], xa_wq[i].astype(BF16), mem_kv, xa_wo[i].astype(BF16), g[3],
                                 g[4], seq)
        hidden = _matmul(hn, mlp_w1[i].astype(BF16), BF16, act="relu2")
        branch = _matmul_acc(hidden, mlp_w2[i].astype(BF16))
        x, hn = _add_norm(x, branch, g[5], norm_gains[i + 1, 0] if i + 1 < depth else None)
    return x.reshape(bsz, seq, d)
```

```python
import functools

import jax
import jax.numpy as jnp
from jax import lax
from jax.experimental import pallas as pl
from jax.experimental.pallas import tpu as pltpu

F32 = jnp.float32
BF16 = jnp.bfloat16
HI = lax.Precision.HIGHEST

NORM_EPS = 1e-6
CHUNK = 64
N_MIXERS = 3
GLA_HEADS = 4
GLA_GATE_NORMALIZER = 16.0
RWKV_HEAD = 64
RWKV_LNX_EPS = 64e-5
XA_HEADS = 4
XA_HEAD_DIM = 128
S5_SLAB_GROUPS = 16
SUBLANES = 8
LANES = 128
VMEM_LIMIT = 60 * 1024 * 1024


def _cp(*sem):
    return pltpu.CompilerParams(dimension_semantics=sem, vmem_limit_bytes=VMEM_LIMIT)


def _rms(x, gain, eps=NORM_EPS):
    ms = jnp.mean(x * x, axis=-1, keepdims=True)
    return x * lax.rsqrt(ms + eps) * gain


def _stat_scratch(rows):
    return pltpu.VMEM((rows, LANES), F32)


def _for_row_blocks(n_rows, body, carry=None, rows=2 * SUBLANES, unroll=1):
    def step(i, c):
        return body(pl.ds(pl.multiple_of(i * rows, rows), rows), c)
    return lax.fori_loop(0, n_rows // rows, step, carry, unroll=unroll)


def _sumsq_lanes(x):
    acc = x[:, 0:LANES] * x[:, 0:LANES]
    for j in range(1, x.shape[1] // LANES):
        blk = x[:, j * LANES:(j + 1) * LANES]
        acc = acc + blk * blk
    return acc


def _finish_scales(s_ref, d):
    ms = jnp.sum(s_ref[...], axis=-1, keepdims=True) * (1.0 / d)
    s_ref[...] = jnp.broadcast_to(lax.rsqrt(ms + NORM_EPS), s_ref.shape)


def _row_scales_into(s_ref, x_ref):
    def body(rs, c):
        s_ref[rs, :] = _sumsq_lanes(x_ref[rs, :])
        return c
    _for_row_blocks(x_ref.shape[0], body, rows=SUBLANES, unroll=2)
    _finish_scales(s_ref, x_ref.shape[1])


def _scaled(x, scale, gain):
    return x * jnp.tile(scale, (1, x.shape[1] // LANES)) * gain


def _norm_rows_into(dst_ref, x_ref, g_ref, s_ref):
    _row_scales_into(s_ref, x_ref)

    def body(rs, c):
        dst_ref[rs, :] = _scaled(x_ref[rs, :], s_ref[rs, :], g_ref[...]).astype(dst_ref.dtype)
        return c
    _for_row_blocks(x_ref.shape[0], body, unroll=2)


def _add_norm_rows(o_ref, m_ref, res_ref, g_ref, s_ref, hn_ref=None, g_next_ref=None):
    _row_scales_into(s_ref, m_ref)

    def body(rs, c):
        x_new = res_ref[rs, :] + _scaled(m_ref[rs, :], s_ref[rs, :], g_ref[...])
        o_ref[rs, :] = x_new
        if hn_ref is not None:
            s_ref[rs, :] = _sumsq_lanes(x_new)
        return c
    _for_row_blocks(o_ref.shape[0], body, rows=SUBLANES, unroll=2)
    if hn_ref is not None:
        _finish_scales(s_ref, o_ref.shape[1])

        def body2(rs, c):
            hn_ref[rs, :] = _scaled(o_ref[rs, :], s_ref[rs, :], g_next_ref[...]).astype(hn_ref.dtype)
            return c
        _for_row_blocks(o_ref.shape[0], body2, unroll=2)


def _dot(a, b, precision=None):
    return jnp.dot(a, b, preferred_element_type=F32, precision=precision)


def _dot_nt(a, b, precision=None):
    return lax.dot_general(a, b, (((1,), (1,)), ((), ())), preferred_element_type=F32,
                           precision=precision)


def _dot_tn(a, b, precision=None):
    return lax.dot_general(a, b, (((0,), (0,)), ((), ())), preferred_element_type=F32,
                           precision=precision)


def _sigmoid(x):
    return 1.0 / (1.0 + jnp.exp(-x))


def _softplus(x):
    return jnp.maximum(x, 0.0) + jnp.log1p(jnp.exp(-jnp.abs(x)))


def _norm_kernel(x_ref, g_ref, o_ref, s_ref):
    _norm_rows_into(o_ref, x_ref, g_ref, s_ref)


def _norm(x, gain, out_dtype=F32, tm=256):
    m, d = x.shape
    tm = min(tm, m)
    return pl.pallas_call(
        _norm_kernel, out_shape=jax.ShapeDtypeStruct((m, d), out_dtype),
        grid=(m // tm,),
        in_specs=[pl.BlockSpec((tm, d), lambda i: (i, 0)), pl.BlockSpec((1, d), lambda i: (0, 0))],
        out_specs=pl.BlockSpec((tm, d), lambda i: (i, 0)),
        scratch_shapes=[_stat_scratch(tm)],
        compiler_params=_cp("parallel"), name="rmsnorm")(x, gain.reshape(1, d))


def _norm_matmul_kernel(x_ref, g_ref, w_ref, o_ref, hn_ref, s_ref):
    @pl.when(pl.program_id(1) == 0)
    def _():
        _norm_rows_into(hn_ref, x_ref, g_ref, s_ref)

    o_ref[...] = _dot(hn_ref[...], w_ref[...]).astype(o_ref.dtype)


def _norm_matmul(x, gain, w, out_dtype, tm=512, tn=512):
    m, d = x.shape
    n = w.shape[1]
    tm, tn = min(tm, m), min(tn, n)
    return pl.pallas_call(
        _norm_matmul_kernel, out_shape=jax.ShapeDtypeStruct((m, n), out_dtype),
        grid=(m // tm, n // tn),
        in_specs=[pl.BlockSpec((tm, d), lambda i, j: (i, 0)),
                  pl.BlockSpec((1, d), lambda i, j: (0, 0)),
                  pl.BlockSpec((d, tn), lambda i, j: (0, j))],
        out_specs=pl.BlockSpec((tm, tn), lambda i, j: (i, j)),
        scratch_shapes=[pltpu.VMEM((tm, d), BF16), _stat_scratch(tm)],
        compiler_params=_cp("parallel", "arbitrary"), name="norm_matmul")(x, gain.reshape(1, d), w)


def _matmul_norm_res_kernel(*refs, glu, tn):
    if glu:
        a_ref, w_ref, w2_ref, res_ref, g_ref, o_ref, s_ref = refs
    else:
        a_ref, w_ref, res_ref, g_ref, o_ref, s_ref = refs
    j = pl.program_id(1)
    a = a_ref[...]
    y = _dot(a, w_ref[...])
    if glu:
        y = y * _sigmoid(_dot(a, w2_ref[...]))
    o_ref[:, pl.ds(pl.multiple_of(j * tn, tn), tn)] = y

    @pl.when(j == pl.num_programs(1) - 1)
    def _():
        _add_norm_rows(o_ref, o_ref, res_ref, g_ref, s_ref)


def _matmul_norm_res(a, ws, res, gain, tm=512):
    m, k = a.shape
    n = ws[0].shape[1]
    glu = len(ws) == 2
    tm, tn = min(tm, m), min(256 if glu else 512, n)
    w_specs = [pl.BlockSpec((k, tn), lambda i, j: (0, j)) for _ in ws]
    return pl.pallas_call(
        functools.partial(_matmul_norm_res_kernel, glu=glu, tn=tn),
        out_shape=jax.ShapeDtypeStruct((m, n), F32),
        grid=(m // tm, n // tn),
        in_specs=[pl.BlockSpec((tm, k), lambda i, j: (i, 0))] + w_specs + [
            pl.BlockSpec((tm, n), lambda i, j: (i, 0)),
            pl.BlockSpec((1, n), lambda i, j: (0, 0))],
        out_specs=pl.BlockSpec((tm, n), lambda i, j: (i, 0)),
        scratch_shapes=[_stat_scratch(tm)],
        compiler_params=_cp("parallel", "arbitrary"),
        name="glu_norm_res" if glu else "matmul_norm_res")(a, *ws, res, gain.reshape(1, n))


def _xa_kernel(x_ref, g_in_ref, wq_ref, k_ref, v_ref, wo_ref, g_out_ref, g_next_ref, o_ref, hn_out_ref,
               hn_ref, s_ref):
    _norm_rows_into(hn_ref, x_ref, g_in_ref, s_ref)
    q = _dot(hn_ref[...], wq_ref[...]) * (XA_HEAD_DIM ** -0.5)
    heads = []
    for h in range(XA_HEADS):
        sl = slice(h * XA_HEAD_DIM, (h + 1) * XA_HEAD_DIM)
        s = _dot_nt(q[:, sl].astype(BF16), k_ref[0, :, sl])
        s = s - jnp.max(s, axis=-1, keepdims=True)
        p = jnp.exp(s)
        p = p / jnp.sum(p, axis=-1, keepdims=True)
        heads.append(_dot(p.astype(BF16), v_ref[0, :, sl]))
    o = jnp.concatenate(heads, axis=-1).astype(BF16)
    o_ref[...] = _dot(o, wo_ref[...])
    _add_norm_rows(o_ref, o_ref, x_ref, g_out_ref, s_ref, hn_out_ref, g_next_ref)


def _cross_attention(x, g_in, wq, mem_kv, wo, g_out, g_next, seq, tm=256):
    m, d = x.shape
    xw = wq.shape[1]
    n_mem = mem_kv.shape[1]
    tm = min(tm, seq)
    tiles_per_seq = seq // tm
    return pl.pallas_call(
        _xa_kernel, out_shape=[jax.ShapeDtypeStruct((m, d), F32), jax.ShapeDtypeStruct((m, d), BF16)],
        grid=(m // tm,),
        in_specs=[pl.BlockSpec((tm, d), lambda i: (i, 0)),
                  pl.BlockSpec((1, d), lambda i: (0, 0)),
                  pl.BlockSpec((d, xw), lambda i: (0, 0)),
                  pl.BlockSpec((1, n_mem, xw), lambda i: (i // tiles_per_seq, 0, 0)),
                  pl.BlockSpec((1, n_mem, xw), lambda i: (i // tiles_per_seq, 0, 1)),
                  pl.BlockSpec((xw, d), lambda i: (0, 0)),
                  pl.BlockSpec((1, d), lambda i: (0, 0)),
                  pl.BlockSpec((1, d), lambda i: (0, 0))],
        out_specs=[pl.BlockSpec((tm, d), lambda i: (i, 0)), pl.BlockSpec((tm, d), lambda i: (i, 0))],
        scratch_shapes=[pltpu.VMEM((tm, d), BF16), _stat_scratch(tm)],
        compiler_params=_cp("parallel"), name="cross_attention")(
            x, g_in.reshape(1, d), wq, mem_kv, mem_kv, wo, g_out.reshape(1, d), g_next.reshape(1, d))


def _matmul_kernel(a_ref, w_ref, o_ref, *, act):
    y = _dot(a_ref[...], w_ref[...].astype(BF16))
    if act == "relu2":
        y = jnp.maximum(y, 0.0)
        y = y * y
    o_ref[...] = y.astype(o_ref.dtype)


def _layer_weight_spec(w, layer, block, index_map):
    if w.ndim == 2:
        return pl.BlockSpec(block, index_map)
    return pl.BlockSpec((None,) + block, lambda *idx: (layer,) + index_map(*idx))


def _matmul(a, w, out_dtype, act=None, layer=None, n_out=None, tm=2048, tn=512):
    m, k = a.shape
    n = w.shape[-1] if n_out is None else n_out
    tm, tn = min(tm, m), min(tn, n)
    return pl.pallas_call(
        functools.partial(_matmul_kernel, act=act),
        out_shape=jax.ShapeDtypeStruct((m, n), out_dtype),
        grid=(m // tm, n // tn),
        in_specs=[pl.BlockSpec((tm, k), lambda i, j: (i, 0), pipeline_mode=pl.Buffered(1)),
                  _layer_weight_spec(w, layer, (k, tn), lambda i, j: (0, j))],
        out_specs=pl.BlockSpec((tm, tn), lambda i, j: (i, j)),
        compiler_params=_cp("parallel", "arbitrary"), name="matmul")(a, w)


def _matmul_acc_kernel(a_ref, w_ref, o_ref, *, ts):
    @pl.when(pl.program_id(2) == 0)
    def _():
        o_ref[...] = jnp.zeros_like(o_ref)

    a = a_ref[...]
    for n in range(o_ref.shape[1] // ts):
        sl = slice(n * ts, (n + 1) * ts)
        o_ref[:, sl] += _dot(a, w_ref[:, sl].astype(BF16))


def _matmul_acc(a, w, layer=None, tm=2048, tn=1024, tk=2048):
    m, k = a.shape
    n = w.shape[-1]
    tm, tn, tk = min(tm, m), min(tn, n), min(tk, k)
    return pl.pallas_call(
        functools.partial(_matmul_acc_kernel, ts=min(256, tn)),
        out_shape=jax.ShapeDtypeStruct((m, n), F32),
        grid=(m // tm, n // tn, k // tk),
        in_specs=[pl.BlockSpec((tm, tk), lambda i, j, kk: (i, kk)),
                  _layer_weight_spec(w, layer, (tk, tn), lambda i, j, kk: (kk, j))],
        out_specs=pl.BlockSpec((tm, tn), lambda i, j, kk: (i, j)),
        compiler_params=_cp("parallel", "parallel", "arbitrary"), name="matmul_acc")(a, w)


def _add_norm_kernel(x_ref, m_ref, gp_ref, gn_ref, o_ref, *rest):
    s_ref = rest[-1]
    hn_ref = rest[0] if len(rest) == 2 else None
    _add_norm_rows(o_ref, m_ref, x_ref, gp_ref, s_ref, hn_ref, gn_ref)


def _add_norm(x, m_branch, g_post, g_next, tm=256):
    m, d = x.shape
    tm = min(tm, m)
    row = pl.BlockSpec((tm, d), lambda i: (i, 0))
    vec = pl.BlockSpec((1, d), lambda i: (0, 0))
    emit_hn = g_next is not None
    out_shape = [jax.ShapeDtypeStruct((m, d), F32)] + ([jax.ShapeDtypeStruct((m, d), BF16)] if emit_hn else [])
    outs = pl.pallas_call(
        _add_norm_kernel, out_shape=out_shape, grid=(m // tm,),
        in_specs=[row, row, vec, vec], out_specs=[row] * len(out_shape),
        scratch_shapes=[_stat_scratch(tm)],
        compiler_params=_cp("parallel"), name="add_norm")(
            x, m_branch, g_post.reshape(1, d), (g_next if emit_hn else g_post).reshape(1, d))
    return (outs[0], outs[1]) if emit_hn else (outs[0], None)


def _gla_kernel(q_ref, k_ref, v_ref, g_ref, gk_ref, wup_ref, bup_ref, onorm_ref, tril_ref,
                o_ref, st_ref):
    @pl.when(pl.program_id(1) == 0)
    def _():
        st_ref[...] = jnp.zeros_like(st_ref)

    nb, _, dk = q_ref.shape
    bs = range(nb)
    gk = gk_ref[...].reshape(nb * CHUNK, gk_ref.shape[2])
    z = _dot(gk, wup_ref[...], HI) + bup_ref[...]
    log_alpha = (jnp.minimum(z, 0.0) - jnp.log1p(jnp.exp(-jnp.abs(z)))) / GLA_GATE_NORMALIZER
    cum_all = _dot(tril_ref[...], log_alpha, HI)
    cum = [cum_all[b * CHUNK:(b + 1) * CHUNK] for b in bs]
    cum_last = [cum[b][CHUNK - 1:CHUNK, :] for b in bs]
    k_dec = [(k_ref[b].astype(F32) * jnp.exp(cum_last[b] - cum[b])).astype(BF16) for b in bs]
    st = [st_ref[b] * jnp.exp(cum_last[b]) + _dot_tn(v_ref[b], k_dec[b]) for b in bs]
    for b in bs:
        st_ref[b] = st[b]
    q = [(q_ref[b].astype(F32) * dk ** -0.5).astype(BF16) for b in bs]
    o = [_dot_nt(q[b], st[b].astype(BF16)) for b in bs]
    for b in bs:
        g = g_ref[b].astype(F32)
        o_ref[b] = (_rms(o[b], onorm_ref[...]) * (g * _sigmoid(g))).astype(o_ref.dtype)


def _gla_scan(proj, gk_lo, w_up, b_up, o_norm, bsz, seq):
    dk_all = w_up.shape[1]
    dkh = dk_all // GLA_HEADS
    dv_all = (proj.shape[1] - 2 * dk_all) // 2
    dvh = dv_all // GLA_HEADS
    nc = seq // CHUNK
    rank_pad = gk_lo.shape[1]
    proj = proj.reshape(bsz, seq, proj.shape[1])
    gk_lo = gk_lo.reshape(bsz, seq, rank_pad)
    tril = jnp.kron(jnp.eye(bsz, dtype=F32), jnp.tril(jnp.ones((CHUNK, CHUNK), F32)))
    k_off = dk_all // dkh
    v_off = 2 * dk_all // dvh
    g_off = v_off + dv_all // dvh
    out = pl.pallas_call(
        _gla_kernel, out_shape=jax.ShapeDtypeStruct((bsz, seq, dv_all), BF16),
        grid=(GLA_HEADS, nc),
        in_specs=[pl.BlockSpec((bsz, CHUNK, dkh), lambda h, c: (0, c, h)),
                  pl.BlockSpec((bsz, CHUNK, dkh), lambda h, c: (0, c, k_off + h)),
                  pl.BlockSpec((bsz, CHUNK, dvh), lambda h, c: (0, c, v_off + h)),
                  pl.BlockSpec((bsz, CHUNK, dvh), lambda h, c: (0, c, g_off + h)),
                  pl.BlockSpec((bsz, CHUNK, rank_pad), lambda h, c: (0, c, 0)),
                  pl.BlockSpec((rank_pad, dkh), lambda h, c: (0, h)),
                  pl.BlockSpec((1, dkh), lambda h, c: (0, h)),
                  pl.BlockSpec((1, dvh), lambda h, c: (0, 0)),
                  pl.BlockSpec((bsz * CHUNK, bsz * CHUNK), lambda h, c: (0, 0))],
        out_specs=pl.BlockSpec((bsz, CHUNK, dvh), lambda h, c: (0, c, h)),
        scratch_shapes=[pltpu.VMEM((bsz, dvh, dkh), F32)],
        compiler_params=_cp("parallel", "arbitrary"), name="gla_scan")(
            proj, proj, proj, proj, gk_lo, w_up, b_up.reshape(1, dk_all), o_norm.reshape(1, dvh), tril)
    return out.reshape(bsz * seq, dv_all)


def _shiftmix_matmul_kernel(hn_ref, prev_ref, mix_ref, w_ref, o_ref, xm_ref, *, tiles_per_seq):
    @pl.when(pl.program_id(2) == 0)
    def _():
        mix = mix_ref[0]
        prev_rows = prev_ref.shape[0]
        last = prev_ref[...].astype(F32)[prev_rows - 1:prev_rows, :]
        first_tile = pl.program_id(0) % tiles_per_seq == 0
        last = jnp.where(first_tile, 0.0, last)

        def body(rs, last):
            hn = hn_ref[rs, :].astype(F32)
            rows = lax.broadcasted_iota(jnp.int32, hn.shape, 0)
            shifted = jnp.where(rows == 0, last, pltpu.roll(hn, 1, axis=0))
            xm_ref[rs, :] = (hn + (shifted - hn) * mix).astype(BF16)
            return hn[hn.shape[0] - 1:, :]

        _for_row_blocks(hn_ref.shape[0], body, last)

    o_ref[0] = _dot(xm_ref[...], w_ref[0]).astype(o_ref.dtype)


def _shiftmix_matmul(hn, mix, w, out_dtype, seq, tm=2048, tn=512):
    m, d = hn.shape
    p_cnt, _, n = w.shape
    tm, tn = min(tm, seq), min(tn, n)
    prev_rows = 2 * SUBLANES
    blk = tm // prev_rows
    return pl.pallas_call(
        functools.partial(_shiftmix_matmul_kernel, tiles_per_seq=seq // tm),
        out_shape=jax.ShapeDtypeStruct((p_cnt, m, n), out_dtype),
        grid=(m // tm, p_cnt, n // tn),
        in_specs=[pl.BlockSpec((tm, d), lambda i, p, j: (i, 0), pipeline_mode=pl.Buffered(1)),
                  pl.BlockSpec((prev_rows, d), lambda i, p, j: (jnp.maximum(i * blk - 1, 0), 0)),
                  pl.BlockSpec((1, 1, d), lambda i, p, j: (p, 0, 0)),
                  pl.BlockSpec((1, d, tn), lambda i, p, j: (p, 0, j))],
        out_specs=pl.BlockSpec((1, tm, tn), lambda i, p, j: (p, i, j)),
        scratch_shapes=[pltpu.VMEM((tm, d), BF16)],
        compiler_params=_cp("parallel", "arbitrary", "arbitrary"), name="shiftmix_matmul")(
            hn, hn, mix.reshape(p_cnt, 1, d), w)


def _lowrank_out_kernel(h_ref, w_ref, b_ref, o_ref):
    p = pl.program_id(0)
    h = h_ref[0]
    act = jnp.where(p == 0, jnp.tanh(h), jnp.where(p == 1, h, _sigmoid(h)))
    o_ref[0] = _dot(act.astype(BF16), w_ref[0]) + b_ref[0]


def _lowrank_out(h, w2, bias, tm=512):
    p_cnt, m, r = h.shape
    d = w2.shape[2]
    tm = min(tm, m)
    return pl.pallas_call(
        _lowrank_out_kernel, out_shape=jax.ShapeDtypeStruct((p_cnt, m, d), F32),
        grid=(p_cnt, m // tm),
        in_specs=[pl.BlockSpec((1, tm, r), lambda p, i: (p, i, 0)),
                  pl.BlockSpec((1, r, d), lambda p, i: (p, 0, 0)),
                  pl.BlockSpec((1, 1, d), lambda p, i: (p, 0, 0))],
        out_specs=pl.BlockSpec((1, tm, d), lambda p, i: (p, i, 0)),
        compiler_params=_cp("arbitrary", "arbitrary"), name="lowrank_out")(h, w2, bias)


def _rwkv_kernel(rkv_ref, lr_ref, par_ref, bd_ref, tril_ref, o_ref, st_ref):
    @pl.when(pl.program_id(2) == 0)
    def _():
        st_ref[...] = jnp.zeros_like(st_ref)

    n = RWKV_HEAD
    heads = st_ref.shape[0]
    r = rkv_ref[0].astype(F32)
    k = rkv_ref[1].astype(F32)
    v = rkv_ref[2].astype(F32)
    w_log = -_softplus(-lr_ref[0]) - 0.5
    log_w = -jnp.exp(w_log)
    a = _sigmoid(lr_ref[1])
    gate = lr_ref[2]
    k_k, k_a, r_k = par_ref[0:1, :], par_ref[1:2, :], par_ref[2:3, :]
    lnx_g, lnx_b = par_ref[3:4, :], par_ref[4:5, :]
    bd = bd_ref[...]
    grp = bd.shape[0]

    def head_sums(x):
        x = x.astype(BF16)
        return jnp.concatenate([_dot(x[:, i:i + grp], bd) for i in range(0, x.shape[1], grp)], axis=1)

    kk = k * k_k
    kk = kk / jnp.maximum(jnp.sqrt(head_sums(kk * kk)), 1e-12)
    k2 = k * (1.0 + (a - 1.0) * k_a)
    cw = _dot(tril_ref[...], log_w, HI)
    cw_last = cw[CHUNK - 1:CHUNK, :]
    e_neg = jnp.exp(-cw)
    e_end = jnp.exp(cw_last - cw)
    a_t = (-kk * jnp.exp(cw - log_w)).astype(BF16)
    r_t = (r * jnp.exp(cw)).astype(BF16)
    b_vec = kk * a
    b_t = (b_vec * e_neg).astype(BF16)
    k_t = (k2 * e_neg).astype(BF16)
    b_w = (b_vec * e_end).astype(BF16)
    k_w = (k2 * e_end).astype(BF16)
    w_end = jnp.exp(cw_last)
    v_b = v.astype(BF16)

    rows = lax.broadcasted_iota(jnp.int32, (CHUNK, 2 * CHUNK), 0)
    cols = lax.broadcasted_iota(jnp.int32, (CHUNK, 2 * CHUNK), 1) % CHUNK
    strict = rows > cols
    incl = rows >= cols
    hs = range(heads)
    sls = [slice(h * n, (h + 1) * n) for h in hs]
    ar = [jnp.concatenate([a_t[:, sl], r_t[:, sl]], axis=0) for sl in sls]
    bk = [jnp.concatenate([b_t[:, sl], k_t[:, sl]], axis=0) for sl in sls]
    s0 = [st_ref[h] for h in hs]
    gram = [_dot_nt(ar[h], bk[h]) for h in hs]
    proj = [_dot_nt(ar[h], s0[h].astype(BF16)) for h in hs]
    v_h = [v_b[:, sl] for sl in sls]
    low = [jnp.where(strict, gram[h][:CHUNK], 0.0).astype(BF16) for h in hs]
    t_row = rows[:, :CHUNK]
    t_col = lax.broadcasted_iota(jnp.int32, (CHUNK, CHUNK), 1)
    eye = (t_row == t_col).astype(F32)

    def lower_left(b):
        return (t_row // (2 * b) == t_col // (2 * b)) & (t_row % (2 * b) >= b) & (t_col % (2 * b) < b)

    nil = [low[h][:, :CHUNK] for h in hs]
    t_inv = [eye + jnp.where(lower_left(1), nil[h], 0).astype(F32) for h in hs]
    b = 2
    while b < CHUNK:
        mask = lower_left(b)
        off = [jnp.where(mask, nil[h], 0) for h in hs]
        t_b = [t_inv[h].astype(BF16) for h in hs]
        right = [_dot(off[h], t_b[h]).astype(BF16) for h in hs]
        t_inv = [t_inv[h] + _dot(t_b[h], right[h]) for h in hs]
        b *= 2
    rhs = [proj[h][:CHUNK] + _dot(low[h][:, CHUNK:], v_h[h]) for h in hs]
    u = [_dot(t_inv[h].astype(BF16), rhs[h].astype(BF16)) for h in hs]
    uv =[jnp.concatenate([u[h].astype(BF16), v_h[h]], axis=0) for h in hs]
    upper = [jnp.where(incl, gram[h][CHUNK:], 0.0).astype(BF16) for h in hs]
    y_heads = [proj[h][CHUNK:] + _dot(upper[h], uv[h]) for h in hs]
    for h in hs:
        bkw = jnp.concatenate([b_w[:, sls[h]], k_w[:, sls[h]]], axis=0)
        st_ref[h] = s0[h] * w_end[:, sls[h]] + _dot_tn(uv[h], bkw)

    y = jnp.concatenate(y_heads, axis=1)
    inv_n = 1.0 / n
    mu = head_sums(y) * inv_n
    yc = y - mu
    var = head_sums(yc * yc) * inv_n
    y = yc * lax.rsqrt(var + RWKV_LNX_EPS) * lnx_g + lnx_b
    bonus = head_sums(r * k2 * r_k) * v
    o_ref[...] = ((y + bonus) * gate).astype(o_ref.dtype)


def _rwkv_scan(rkv, lr, params, bsz, seq, heads_per_step=32):
    _, m, d = rkv.shape
    n_heads = d // RWKV_HEAD
    hg = min(heads_per_step, n_heads)
    w = hg * RWKV_HEAD
    nc = seq // CHUNK
    grp = min(2 * LANES, w)
    lane_head = jnp.arange(grp) // RWKV_HEAD
    bd = (lane_head[:, None] == lane_head[None, :]).astype(BF16)
    tril = jnp.tril(jnp.ones((CHUNK, CHUNK), F32))
    n_par = params.shape[0]
    return pl.pallas_call(
        _rwkv_kernel, out_shape=jax.ShapeDtypeStruct((m, d), BF16),
        grid=(bsz, n_heads // hg, nc),
        in_specs=[pl.BlockSpec((3, CHUNK, w), lambda b, g, c: (0, b * nc + c, g)),
                  pl.BlockSpec((3, CHUNK, w), lambda b, g, c: (0, b * nc + c, g)),
                  pl.BlockSpec((n_par, w), lambda b, g, c: (0, g)),
                  pl.BlockSpec((grp, grp), lambda b, g, c: (0, 0)),
                  pl.BlockSpec((CHUNK, CHUNK), lambda b, g, c: (0, 0))],
        out_specs=pl.BlockSpec((CHUNK, w), lambda b, g, c: (b * nc + c, g)),
        scratch_shapes=[pltpu.VMEM((hg, RWKV_HEAD, RWKV_HEAD), F32)],
        compiler_params=_cp("parallel", "parallel", "arbitrary"), name="rwkv_scan")(
            rkv, lr, params, bd, tril)


def _cmul_add(acc_r, acc_i, cr, ci, xr, xi):
    return acc_r + cr * xr - ci * xi, acc_i + cr * xi + ci * xr


def _s5_kernel(h_ref, perm_ref, unperm_ref, wb_ref, wc_ref, coef_ref, pow_ref, d_ref, o_ref, x_ref, carry_ref):
    @pl.when(pl.program_id(2) == 0)
    def _():
        carry_ref[...] = jnp.zeros_like(carry_ref)

    tt = h_ref.shape[0]
    ns = carry_ref.shape[2]
    seg = tt // SUBLANES
    h = _dot(perm_ref[...], h_ref[...]).astype(BF16)
    x_ref[...] = _dot(h, wb_ref[0])
    a_r, a_i = coef_ref[0, 8], coef_ref[0, 9]

    def step_rows(k, c_r, c_i, x_r, x_i):
        rows = pl.ds(pl.multiple_of(k * SUBLANES, SUBLANES), SUBLANES)
        xr, xi = _cmul_add(x_ref[rows, 0:ns], x_ref[rows, ns:2 * ns], c_r, c_i, x_r, x_i)
        x_ref[rows, 0:ns] = xr
        x_ref[rows, ns:2 * ns] = xi
        return xr, xi

    zeros = jnp.zeros((SUBLANES, ns), F32)
    end_r, end_i = lax.fori_loop(0, seg, lambda k, c: step_rows(k, a_r, a_i, *c), (zeros, zeros))

    for s_idx, shift in enumerate((1, 2, 4)):
        end_r, end_i = _cmul_add(end_r, end_i, coef_ref[0, 2 * s_idx], coef_ref[0, 2 * s_idx + 1],
                                 pltpu.roll(end_r, shift, axis=0), pltpu.roll(end_i, shift, axis=0))
    in_r = jnp.broadcast_to(carry_ref[0], (SUBLANES, ns))
    in_i = jnp.broadcast_to(carry_ref[1], (SUBLANES, ns))
    end_r, end_i = _cmul_add(end_r, end_i, coef_ref[0, 6], coef_ref[0, 7], in_r, in_i)
    carry_ref[0] = end_r[SUBLANES - 1:SUBLANES, :]
    carry_ref[1] = end_i[SUBLANES - 1:SUBLANES, :]
    first = lax.broadcasted_iota(jnp.int32, (SUBLANES, ns), 0) == 0
    in_r = jnp.where(first, in_r, pltpu.roll(end_r, 1, axis=0))
    in_i = jnp.where(first, in_i, pltpu.roll(end_i, 1, axis=0))

    def correct(k, c):
        p_r = jnp.broadcast_to(pow_ref[0, 0, pl.ds(k, 1), :], (SUBLANES, ns))
        p_i = jnp.broadcast_to(pow_ref[0, 1, pl.ds(k, 1), :], (SUBLANES, ns))
        step_rows(k, p_r, p_i, in_r, in_i)
        return c

    lax.fori_loop(0, seg, correct, 0, unroll=2)

    y = _dot(x_ref[...].astype(BF16), wc_ref[0]) + d_ref[...] * h.astype(F32)
    y = (0.5 * y * (1.0 + lax.erf(y * (2.0 ** -0.5)))).astype(BF16)
    o_ref[...] = _dot(unperm_ref[...], y).astype(o_ref.dtype)


def _s5_scan(hn, lam_re, lam_im, log_dt, b_re, b_im, c_re, c_im, d_skip, bsz, seq, tt=1024):
    m, d = hn.shape
    n_groups, n_state, grp = b_re.shape
    sg = min(S5_SLAB_GROUPS, n_groups)
    slabs = n_groups // sg
    ch = sg * grp
    ns = sg * n_state
    tt = min(tt, seq)

    lr = jnp.minimum(lam_re.astype(F32), -1e-4)
    li = lam_im.astype(F32)
    dt = jnp.exp(log_dt.astype(F32))[:, None]
    mag = jnp.exp(lr * dt)
    ab_re = mag * jnp.cos(li * dt)
    ab_im = mag * jnp.sin(li * dt)
    den = lr * lr + li * li
    nr, ni = ab_re - 1.0, ab_im
    z_re = (nr * lr + ni * li) / den
    z_im = (ni * lr - nr * li) / den
    bb_re = z_re[..., None] * b_re - z_im[..., None] * b_im
    bb_im = z_re[..., None] * b_im + z_im[..., None] * b_re

    eye = jnp.eye(sg, dtype=F32)

    def block_diag_in(bb):
        t = bb.reshape(slabs, sg, n_state, grp)
        return jnp.einsum('sgph,gk->sghkp', t, eye).reshape(slabs, ch, ns)

    def block_diag_out(cc):
        t = cc.reshape(slabs, sg, grp, n_state)
        return jnp.einsum('sgqp,gk->sgpkq', t, eye).reshape(slabs, ns, ch)

    w_b = jnp.concatenate([block_diag_in(bb_re), block_diag_in(bb_im)], axis=2).astype(BF16)
    w_c = jnp.concatenate([block_diag_out(c_re.astype(F32)), -block_diag_out(c_im.astype(F32))],
                          axis=1).astype(BF16)

    def powers(e):
        e = e[:, None, None]
        mg = jnp.exp(lr[None] * dt[None] * e)
        ang = li[None] * dt[None] * e
        re = (mg * jnp.cos(ang)).reshape(-1, slabs, ns)
        im = (mg * jnp.sin(ang)).reshape(-1, slabs, ns)
        return re, im

    seg = tt // SUBLANES
    row = jnp.arange(SUBLANES, dtype=F32)
    coefs = []
    for shift in (1, 2, 4):
        re, im = powers(jnp.full((SUBLANES,), float(seg * shift), F32))
        mask = (row >= shift)[:, None, None]
        coefs += [jnp.where(mask, re, 0.0), jnp.where(mask, im, 0.0)]
    coefs += list(powers(seg * (row + 1.0)))
    coefs += list(powers(jnp.ones((SUBLANES,), F32)))
    coef = jnp.stack(coefs, axis=0).transpose(2, 0, 1, 3)
    pw = jnp.stack(powers(jnp.arange(1, seg + 1, dtype=F32)), axis=0).transpose(2, 0, 1, 3)
    r_idx = jnp.arange(tt)
    src_time = (r_idx % SUBLANES) * seg + r_idx // SUBLANES
    perm = (src_time[:, None] == jnp.arange(tt)[None, :]).astype(BF16)

    nt = seq // tt
    return pl.pallas_call(
        _s5_kernel, out_shape=jax.ShapeDtypeStruct((m, d), BF16),
        grid=(bsz, slabs, nt),
        in_specs=[pl.BlockSpec((tt, ch), lambda b, s, t: (b * nt + t, s)),
                  pl.BlockSpec((tt, tt), lambda b, s, t: (0, 0)),
                  pl.BlockSpec((tt, tt), lambda b, s, t: (0, 0)),
                  pl.BlockSpec((1, ch, 2 * ns), lambda b, s, t: (s, 0, 0)),
                  pl.BlockSpec((1, 2 * ns, ch), lambda b, s, t: (s, 0, 0)),
                  pl.BlockSpec((1, 10, SUBLANES, ns), lambda b, s, t: (s, 0, 0, 0)),
                  pl.BlockSpec((1, 2, seg, ns), lambda b, s, t: (s, 0, 0, 0)),
                  pl.BlockSpec((1, ch), lambda b, s, t: (0, s))],
        out_specs=pl.BlockSpec((tt, ch), lambda b, s, t: (b * nt + t, s)),
        scratch_shapes=[pltpu.VMEM((tt, 2 * ns), F32), pltpu.VMEM((2, 1, ns), F32)],
        compiler_params=_cp("parallel", "parallel", "arbitrary"), name="s5_scan")(
            hn, perm, perm.T, w_b, w_c, coef, pw, d_skip.reshape(1, d))


def _pad_to(x, axis, size):
    pad = size - x.shape[axis]
    if pad == 0:
        return x
    widths = [(0, 0)] * x.ndim
    widths[axis] = (0, pad)
    return jnp.pad(x, widths)


def _round_up(n, k):
    return (n + k - 1) // k * k


def _gla_layer(x, hn, g_post, w_in_all, layer, w_gk_up, b_gk_up, o_norm, w_out, bsz, seq):
    w_in = w_in_all[layer]
    rank = w_gk_up.shape[0]
    n_main = w_in.shape[1] - rank
    rank_pad = _round_up(rank, LANES)
    proj = _matmul(hn, w_in_all, BF16, layer=layer, n_out=n_main)
    gk_lo = _matmul(hn, _pad_to(w_in[:, n_main:], 1, rank_pad).astype(BF16), F32)
    o = _gla_scan(proj, gk_lo, _pad_to(w_gk_up, 0, rank_pad), b_gk_up, o_norm, bsz, seq)
    return _matmul_norm_res(o, [w_out.astype(BF16)], x, g_post)


def _rwkv_layer(x, hn, g_post, mix, w_r, w_k, w_v, w_o, w0, w_w1, w_w2, a0, w_a1, w_a2,
                w_g1, w_g2, k_k, k_a, r_k, lnx_g, lnx_b, bsz, seq):
    d = x.shape[1]
    rkv = _shiftmix_matmul(hn, jnp.stack([mix[0], mix[2], mix[3]]),
                           jnp.stack([w_r, w_k, w_v]).astype(BF16), BF16, seq)
    rank_pad = _round_up(max(w_w1.shape[1], w_a1.shape[1], w_g1.shape[1]), LANES)
    w1 = jnp.stack([_pad_to(w, 1, rank_pad) for w in (w_w1, w_a1, w_g1)]).astype(BF16)
    w2 = jnp.stack([_pad_to(w, 0, rank_pad) for w in (w_w2, w_a2, w_g2)]).astype(BF16)
    low = _shiftmix_matmul(hn, jnp.stack([mix[1], mix[4], mix[5]]), w1, F32, seq)
    bias = jnp.stack([w0, a0, jnp.zeros_like(w0)]).reshape(3, 1, d)
    lr = _lowrank_out(low, w2, bias)
    params = _pad_to(jnp.stack([k_k, k_a, r_k, lnx_g, lnx_b]), 0, SUBLANES)
    y = _rwkv_scan(rkv, lr, params, bsz, seq)
    return _matmul_norm_res(y, [w_o.astype(BF16)], x, g_post)


def _s5_layer(x, hn, g_post, lam_re, lam_im, log_dt, b_re, b_im, c_re, c_im, d_skip,
              w_glu1, w_glu2, bsz, seq):
    y = _s5_scan(hn, lam_re, lam_im, log_dt, b_re, b_im, c_re, c_im, d_skip, bsz, seq)
    return _matmul_norm_res(y, [w_glu1.astype(BF16), w_glu2.astype(BF16)], x, g_post)


def kernel(x, mem, norm_gains, mem_norm, mem_w_kv, xa_wq, xa_wo, mlp_w1, mlp_w2, gla_w_in, gla_w_gk_up, gla_b_gk_up, gla_o_norm, gla_w_out, rwkv_mix, rwkv_w_r, rwkv_w_k, rwkv_w_v, rwkv_w_o, rwkv_w0, rwkv_w_w1, rwkv_w_w2, rwkv_a0, rwkv_w_a1, rwkv_w_a2, rwkv_w_g1, rwkv_w_g2, rwkv_k_k, rwkv_k_a, rwkv_r_k, rwkv_lnx_g, rwkv_lnx_b, s5_lam_re, s5_lam_im, s5_log_dt, s5_b_re, s5_b_im, s5_c_re, s5_c_im, s5_d, s5_w_glu1, s5_w_glu2):
    bsz, seq, d = x.shape
    n_mem = mem.shape[1]
    depth = norm_gains.shape[0]
    assert seq % CHUNK == 0 and xa_wq.shape[2] == XA_HEADS * XA_HEAD_DIM

    mem_kv = _norm_matmul(mem.reshape(bsz * n_mem, d), mem_norm, mem_w_kv.astype(BF16), BF16)
    mem_kv = mem_kv.reshape(bsz, n_mem, mem_w_kv.shape[1])

    x = x.reshape(bsz * seq, d)
    hn = _norm(x, norm_gains[0, 0], BF16)
    for i in range(depth):
        kind, j = i % N_MIXERS, i // N_MIXERS
        g = norm_gains[i]
        if kind == 0:
            x = _gla_layer(x, hn, g[1], gla_w_in, j, gla_w_gk_up[j], gla_b_gk_up[j],
                           gla_o_norm[j], gla_w_out[j], bsz, seq)
        elif kind == 1:
            x = _rwkv_layer(x, hn, g[1], rwkv_mix[j], rwkv_w_r[j], rwkv_w_k[j], rwkv_w_v[j],
                            rwkv_w_o[j], rwkv_w0[j], rwkv_w_w1[j], rwkv_w_w2[j], rwkv_a0[j],
                            rwkv_w_a1[j], rwkv_w_a2[j], rwkv_w_g1[j], rwkv_w_g2[j], rwkv_k_k[j],
                            rwkv_k_a[j], rwkv_r_k[j], rwkv_lnx_g[j], rwkv_lnx_b[j], bsz, seq)
        else:
            x = _s5_layer(x, hn, g[1], s5_lam_re[j], s5_lam_im[j], s5_log_dt[j], s5_b_re[j],
                          s5_b_im[j], s5_c_re[j], s5_c_im[j], s5_d[j], s5_w_glu1[j], s5_w_glu2[j],
                          bsz, seq)
        x, hn = _cross_attention(x, g[2], xa_wq[i].astype(BF16), mem_kv, xa_wo[i].astype(BF16), g[3],
                                 g[4], seq)
        hidden = _matmul(hn, mlp_w1, BF16, act="relu2", layer=i)
        branch = _matmul_acc(hidden, mlp_w2, layer=i)
        x, hn = _add_norm(x, branch, g[5], norm_gains[i + 1, 0] if i + 1 < depth else None)
    return x.reshape(bsz, seq, d)
```

```python
import functools

import jax
import jax.numpy as jnp
from jax import lax
from jax.experimental import pallas as pl
from jax.experimental.pallas import tpu as pltpu

F32 = jnp.float32
BF16 = jnp.bfloat16
HI = lax.Precision.HIGHEST

NORM_EPS = 1e-6
CHUNK = 64
N_MIXERS = 3
GLA_HEADS = 4
GLA_GATE_NORMALIZER = 16.0
RWKV_HEAD = 64
RWKV_LNX_EPS = 64e-5
XA_HEADS = 4
XA_HEAD_DIM = 128
S5_SLAB_GROUPS = 16
SUBLANES = 8
LANES = 128
VMEM_LIMIT = 60 * 1024 * 1024


def _cp(*sem):
    return pltpu.CompilerParams(dimension_semantics=sem, vmem_limit_bytes=VMEM_LIMIT)


def _rms(x, gain, eps=NORM_EPS):
    ms = jnp.mean(x * x, axis=-1, keepdims=True)
    return x * lax.rsqrt(ms + eps) * gain


def _stat_scratch(rows):
    return pltpu.VMEM((rows, LANES), F32)


def _for_row_blocks(n_rows, body, carry=None, rows=2 * SUBLANES, unroll=1):
    def step(i, c):
        return body(pl.ds(pl.multiple_of(i * rows, rows), rows), c)
    return lax.fori_loop(0, n_rows // rows, step, carry, unroll=unroll)


def _sumsq_lanes(x):
    acc = x[:, 0:LANES] * x[:, 0:LANES]
    for j in range(1, x.shape[1] // LANES):
        blk = x[:, j * LANES:(j + 1) * LANES]
        acc = acc + blk * blk
    return acc


def _finish_scales(s_ref, d):
    ms = jnp.sum(s_ref[...], axis=-1, keepdims=True) * (1.0 / d)
    s_ref[...] = jnp.broadcast_to(lax.rsqrt(ms + NORM_EPS), s_ref.shape)


def _rows_of(ref):
    return lambda rs: ref[rs, :]


def _row_scales_into(s_ref, load, d):
    def body(rs, c):
        s_ref[rs, :] = _sumsq_lanes(load(rs))
        return c
    _for_row_blocks(s_ref.shape[0], body, rows=SUBLANES, unroll=2)
    _finish_scales(s_ref, d)


def _scaled(x, scale, gain):
    return x * jnp.tile(scale, (1, x.shape[1] // LANES)) * gain


def _norm_rows_into(dst_ref, load, g_ref, s_ref):
    _row_scales_into(s_ref, load, dst_ref.shape[1])

    def body(rs, c):
        dst_ref[rs, :] = _scaled(load(rs), s_ref[rs, :], g_ref[...]).astype(dst_ref.dtype)
        return c
    _for_row_blocks(dst_ref.shape[0], body, unroll=2)


def _add_norm_rows(o_ref, load_m, load_res, g_ref, s_ref, hn_ref=None, g_next_ref=None):
    _row_scales_into(s_ref, load_m, o_ref.shape[1])

    def body(rs, c):
        x_new = load_res(rs) + _scaled(load_m(rs), s_ref[rs, :], g_ref[...])
        o_ref[rs, :] = x_new
        if hn_ref is not None:
            s_ref[rs, :] = _sumsq_lanes(x_new)
        return c
    _for_row_blocks(o_ref.shape[0], body, rows=SUBLANES, unroll=2)
    if hn_ref is not None:
        _finish_scales(s_ref, o_ref.shape[1])

        def body2(rs, c):
            hn_ref[rs, :] = _scaled(o_ref[rs, :], s_ref[rs, :], g_next_ref[...]).astype(hn_ref.dtype)
            return c
        _for_row_blocks(o_ref.shape[0], body2, unroll=2)


def _dot(a, b, precision=None):
    return jnp.dot(a, b, preferred_element_type=F32, precision=precision)


def _dot_nt(a, b, precision=None):
    return lax.dot_general(a, b, (((1,), (1,)), ((), ())), preferred_element_type=F32,
                           precision=precision)


def _dot_tn(a, b, precision=None):
    return lax.dot_general(a, b, (((0,), (0,)), ((), ())), preferred_element_type=F32,
                           precision=precision)


def _sigmoid(x):
    return 1.0 / (1.0 + jnp.exp(-x))


def _softplus(x):
    return jnp.maximum(x, 0.0) + jnp.log1p(jnp.exp(-jnp.abs(x)))


def _norm_kernel(x_ref, g_ref, o_ref, s_ref):
    _norm_rows_into(o_ref, _rows_of(x_ref), g_ref, s_ref)


def _norm(x, gain, out_dtype=F32, tm=256):
    m, d = x.shape
    tm = min(tm, m)
    return pl.pallas_call(
        _norm_kernel, out_shape=jax.ShapeDtypeStruct((m, d), out_dtype),
        grid=(m // tm,),
        in_specs=[pl.BlockSpec((tm, d), lambda i: (i, 0)), pl.BlockSpec((1, d), lambda i: (0, 0))],
        out_specs=pl.BlockSpec((tm, d), lambda i: (i, 0)),
        scratch_shapes=[_stat_scratch(tm)],
        compiler_params=_cp("parallel"), name="rmsnorm")(x, gain.reshape(1, d))


def _norm_matmul_kernel(x_ref, g_ref, w_ref, o_ref, hn_ref, s_ref):
    @pl.when(pl.program_id(1) == 0)
    def _():
        _norm_rows_into(hn_ref, _rows_of(x_ref), g_ref, s_ref)

    o_ref[...] = _dot(hn_ref[...], w_ref[...]).astype(o_ref.dtype)


def _norm_matmul(x, gain, w, out_dtype, tm=512, tn=512):
    m, d = x.shape
    n = w.shape[1]
    tm, tn = min(tm, m), min(tn, n)
    return pl.pallas_call(
        _norm_matmul_kernel, out_shape=jax.ShapeDtypeStruct((m, n), out_dtype),
        grid=(m // tm, n // tn),
        in_specs=[pl.BlockSpec((tm, d), lambda i, j: (i, 0)),
                  pl.BlockSpec((1, d), lambda i, j: (0, 0)),
                  pl.BlockSpec((d, tn), lambda i, j: (0, j))],
        out_specs=pl.BlockSpec((tm, tn), lambda i, j: (i, j)),
        scratch_shapes=[pltpu.VMEM((tm, d), BF16), _stat_scratch(tm)],
        compiler_params=_cp("parallel", "arbitrary"), name="norm_matmul")(x, gain.reshape(1, d), w)


def _matmul_norm_kernel(*refs, glu, tn):
    if glu:
        a_ref, w_ref, w2_ref, g_ref, o_ref, acc_ref, s_ref = refs
    else:
        a_ref, w_ref, g_ref, o_ref, acc_ref, s_ref = refs
    j = pl.program_id(1)
    a = a_ref[...]
    y = _dot(a, w_ref[...])
    if glu:
        y = y * _sigmoid(_dot(a, w2_ref[...]))
    acc_ref[:, pl.ds(pl.multiple_of(j * tn, tn), tn)] = y

    @pl.when(j == pl.num_programs(1) - 1)
    def _():
        _norm_rows_into(o_ref, _rows_of(acc_ref), g_ref, s_ref)


def _matmul_norm(a, ws, gain, tm=1024):
    m, k = a.shape
    n = ws[0].shape[1]
    glu = len(ws) == 2
    tm, tn = min(tm, m), min(256 if glu else 512, n)
    w_specs = [pl.BlockSpec((k, tn), lambda i, j: (0, j)) for _ in ws]
    return pl.pallas_call(
        functools.partial(_matmul_norm_kernel, glu=glu, tn=tn),
        out_shape=jax.ShapeDtypeStruct((m, n), BF16),
        grid=(m // tm, n // tn),
        in_specs=[pl.BlockSpec((tm, k), lambda i, j: (i, 0))] + w_specs + [
            pl.BlockSpec((1, n), lambda i, j: (0, 0))],
        out_specs=pl.BlockSpec((tm, n), lambda i, j: (i, 0)),
        scratch_shapes=[pltpu.VMEM((tm, n), F32), _stat_scratch(tm)],
        compiler_params=_cp("parallel", "arbitrary"),
        name="glu_norm" if glu else "matmul_norm")(a, *ws, gain.reshape(1, n))


def _xa_kernel(x_ref, br_ref, g_in_ref, wq_ref, k_ref, v_ref, wo_ref, g_out_ref, g_next_ref, o_ref, hn_out_ref,
               hn_ref, s_ref):
    def x1(rs):
        return x_ref[rs, :] + br_ref[rs, :].astype(F32)

    _norm_rows_into(hn_ref, x1, g_in_ref, s_ref)
    q = _dot(hn_ref[...], wq_ref[...]) * (XA_HEAD_DIM ** -0.5)
    heads = []
    for h in range(XA_HEADS):
        sl = slice(h * XA_HEAD_DIM, (h + 1) * XA_HEAD_DIM)
        s = _dot_nt(q[:, sl].astype(BF16), k_ref[0, :, sl])
        s = s - jnp.max(s, axis=-1, keepdims=True)
        p = jnp.exp(s)
        p = p / jnp.sum(p, axis=-1, keepdims=True)
        heads.append(_dot(p.astype(BF16), v_ref[0, :, sl]))
    o = jnp.concatenate(heads, axis=-1).astype(BF16)
    o_ref[...] = _dot(o, wo_ref[...])
    _add_norm_rows(o_ref, _rows_of(o_ref), x1, g_out_ref, s_ref, hn_out_ref, g_next_ref)


def _cross_attention(x, branch, g_in, wq, mem_kv, wo, g_out, g_next, seq, tm=256):
    m, d = x.shape
    xw = wq.shape[1]
    n_mem = mem_kv.shape[1]
    tm = min(tm, seq)
    tiles_per_seq = seq // tm
    return pl.pallas_call(
        _xa_kernel, out_shape=[jax.ShapeDtypeStruct((m, d), F32), jax.ShapeDtypeStruct((m, d), BF16)],
        grid=(m // tm,),
        in_specs=[pl.BlockSpec((tm, d), lambda i: (i, 0)),
                  pl.BlockSpec((tm, d), lambda i: (i, 0)),
                  pl.BlockSpec((1, d), lambda i: (0, 0)),
                  pl.BlockSpec((d, xw), lambda i: (0, 0)),
                  pl.BlockSpec((1, n_mem, xw), lambda i: (i // tiles_per_seq, 0, 0)),
                  pl.BlockSpec((1, n_mem, xw), lambda i: (i // tiles_per_seq, 0, 1)),
                  pl.BlockSpec((xw, d), lambda i: (0, 0)),
                  pl.BlockSpec((1, d), lambda i: (0, 0)),
                  pl.BlockSpec((1, d), lambda i: (0, 0))],
        out_specs=[pl.BlockSpec((tm, d), lambda i: (i, 0)), pl.BlockSpec((tm, d), lambda i: (i, 0))],
        scratch_shapes=[pltpu.VMEM((tm, d), BF16), _stat_scratch(tm)],
        compiler_params=_cp("parallel"), name="cross_attention")(
            x, branch, g_in.reshape(1, d), wq, mem_kv, mem_kv, wo, g_out.reshape(1, d), g_next.reshape(1, d))


def _matmul_kernel(a_ref, w_ref, o_ref, *, act):
    y = _dot(a_ref[...], w_ref[...].astype(BF16))
    if act == "relu2":
        y = jnp.maximum(y, 0.0)
        y = y * y
    o_ref[...] = y.astype(o_ref.dtype)


def _layer_weight_spec(w, layer, block, index_map):
    if w.ndim == 2:
        return pl.BlockSpec(block, index_map)
    return pl.BlockSpec((None,) + block, lambda *idx: (layer,) + index_map(*idx))


def _matmul(a, w, out_dtype, act=None, layer=None, n_out=None, tm=2048, tn=512):
    m, k = a.shape
    n = w.shape[-1] if n_out is None else n_out
    tm, tn = min(tm, m), min(tn, n)
    return pl.pallas_call(
        functools.partial(_matmul_kernel, act=act),
        out_shape=jax.ShapeDtypeStruct((m, n), out_dtype),
        grid=(m // tm, n // tn),
        in_specs=[pl.BlockSpec((tm, k), lambda i, j: (i, 0), pipeline_mode=pl.Buffered(1)),
                  _layer_weight_spec(w, layer, (k, tn), lambda i, j: (0, j))],
        out_specs=pl.BlockSpec((tm, tn), lambda i, j: (i, j)),
        compiler_params=_cp("parallel", "arbitrary"), name="matmul")(a, w)


def _matmul_acc_kernel(a_ref, w_ref, o_ref, *, ts):
    @pl.when(pl.program_id(2) == 0)
    def _():
        o_ref[...] = jnp.zeros_like(o_ref)

    a = a_ref[...]
    for n in range(o_ref.shape[1] // ts):
        sl = slice(n * ts, (n + 1) * ts)
        o_ref[:, sl] += _dot(a, w_ref[:, sl].astype(BF16))


def _matmul_acc(a, w, layer=None, tm=2048, tn=1024, tk=2048):
    m, k = a.shape
    n = w.shape[-1]
    tm, tn, tk = min(tm, m), min(tn, n), min(tk, k)
    return pl.pallas_call(
        functools.partial(_matmul_acc_kernel, ts=min(256, tn)),
        out_shape=jax.ShapeDtypeStruct((m, n), F32),
        grid=(m // tm, n // tn, k // tk),
        in_specs=[pl.BlockSpec((tm, tk), lambda i, j, kk: (i, kk)),
                  _layer_weight_spec(w, layer, (tk, tn), lambda i, j, kk: (kk, j))],
        out_specs=pl.BlockSpec((tm, tn), lambda i, j, kk: (i, j)),
        compiler_params=_cp("parallel", "parallel", "arbitrary"), name="matmul_acc")(a, w)


def _add_norm_kernel(x_ref, m_ref, gp_ref, gn_ref, o_ref, *rest):
    s_ref = rest[-1]
    hn_ref = rest[0] if len(rest) == 2 else None
    _add_norm_rows(o_ref, _rows_of(m_ref), _rows_of(x_ref), gp_ref, s_ref, hn_ref, gn_ref)


def _add_norm(x, m_branch, g_post, g_next, tm=256):
    m, d = x.shape
    tm = min(tm, m)
    row = pl.BlockSpec((tm, d), lambda i: (i, 0))
    vec = pl.BlockSpec((1, d), lambda i: (0, 0))
    emit_hn = g_next is not None
    out_shape = [jax.ShapeDtypeStruct((m, d), F32)] + ([jax.ShapeDtypeStruct((m, d), BF16)] if emit_hn else [])
    outs = pl.pallas_call(
        _add_norm_kernel, out_shape=out_shape, grid=(m // tm,),
        in_specs=[row, row, vec, vec], out_specs=[row] * len(out_shape),
        scratch_shapes=[_stat_scratch(tm)],
        compiler_params=_cp("parallel"), name="add_norm")(
            x, m_branch, g_post.reshape(1, d), (g_next if emit_hn else g_post).reshape(1, d))
    return (outs[0], outs[1]) if emit_hn else (outs[0], None)


def _gla_kernel(q_ref, k_ref, v_ref, g_ref, gk_ref, wup_ref, bup_ref, onorm_ref, tril_ref,
                o_ref, st_ref):
    @pl.when(pl.program_id(1) == 0)
    def _():
        st_ref[...] = jnp.zeros_like(st_ref)

    nb, _, dk = q_ref.shape
    bs = range(nb)
    gk = gk_ref[...].reshape(nb * CHUNK, gk_ref.shape[2])
    z = _dot(gk, wup_ref[...], HI) + bup_ref[...]
    log_alpha = (jnp.minimum(z, 0.0) - jnp.log1p(jnp.exp(-jnp.abs(z)))) / GLA_GATE_NORMALIZER
    cum_all = _dot(tril_ref[...], log_alpha, HI)
    cum = [cum_all[b * CHUNK:(b + 1) * CHUNK] for b in bs]
    cum_last = [cum[b][CHUNK - 1:CHUNK, :] for b in bs]
    k_dec = [(k_ref[b].astype(F32) * jnp.exp(cum_last[b] - cum[b])).astype(BF16) for b in bs]
    st = [st_ref[b] * jnp.exp(cum_last[b]) + _dot_tn(v_ref[b], k_dec[b]) for b in bs]
    for b in bs:
        st_ref[b] = st[b]
    q = [(q_ref[b].astype(F32) * dk ** -0.5).astype(BF16) for b in bs]
    o = [_dot_nt(q[b], st[b].astype(BF16)) for b in bs]
    for b in bs:
        g = g_ref[b].astype(F32)
        o_ref[b] = (_rms(o[b], onorm_ref[...]) * (g * _sigmoid(g))).astype(o_ref.dtype)


def _gla_scan(proj, gk_lo, w_up, b_up, o_norm, bsz, seq):
    dk_all = w_up.shape[1]
    dkh = dk_all // GLA_HEADS
    dv_all = (proj.shape[1] - 2 * dk_all) // 2
    dvh = dv_all // GLA_HEADS
    nc = seq // CHUNK
    rank_pad = gk_lo.shape[1]
    proj = proj.reshape(bsz, seq, proj.shape[1])
    gk_lo = gk_lo.reshape(bsz, seq, rank_pad)
    tril = jnp.kron(jnp.eye(bsz, dtype=F32), jnp.tril(jnp.ones((CHUNK, CHUNK), F32)))
    k_off = dk_all // dkh
    v_off = 2 * dk_all // dvh
    g_off = v_off + dv_all // dvh
    out = pl.pallas_call(
        _gla_kernel, out_shape=jax.ShapeDtypeStruct((bsz, seq, dv_all), BF16),
        grid=(GLA_HEADS, nc),
        in_specs=[pl.BlockSpec((bsz, CHUNK, dkh), lambda h, c: (0, c, h)),
                  pl.BlockSpec((bsz, CHUNK, dkh), lambda h, c: (0, c, k_off + h)),
                  pl.BlockSpec((bsz, CHUNK, dvh), lambda h, c: (0, c, v_off + h)),
                  pl.BlockSpec((bsz, CHUNK, dvh), lambda h, c: (0, c, g_off + h)),
                  pl.BlockSpec((bsz, CHUNK, rank_pad), lambda h, c: (0, c, 0)),
                  pl.BlockSpec((rank_pad, dkh), lambda h, c: (0, h)),
                  pl.BlockSpec((1, dkh), lambda h, c: (0, h)),
                  pl.BlockSpec((1, dvh), lambda h, c: (0, 0)),
                  pl.BlockSpec((bsz * CHUNK, bsz * CHUNK), lambda h, c: (0, 0))],
        out_specs=pl.BlockSpec((bsz, CHUNK, dvh), lambda h, c: (0, c, h)),
        scratch_shapes=[pltpu.VMEM((bsz, dvh, dkh), F32)],
        compiler_params=_cp("parallel", "arbitrary"), name="gla_scan")(
            proj, proj, proj, proj, gk_lo, w_up, b_up.reshape(1, dk_all), o_norm.reshape(1, dvh), tril)
    return out.reshape(bsz * seq, dv_all)


def _shiftmix_matmul_kernel(hn_ref, prev_ref, mix_ref, w_ref, o_ref, xm_ref, *, tiles_per_seq):
    @pl.when(pl.program_id(2) == 0)
    def _():
        mix = mix_ref[0]
        prev_rows = prev_ref.shape[0]
        last = prev_ref[...].astype(F32)[prev_rows - 1:prev_rows, :]
        first_tile = pl.program_id(0) % tiles_per_seq == 0
        last = jnp.where(first_tile, 0.0, last)

        def body(rs, last):
            hn = hn_ref[rs, :].astype(F32)
            rows = lax.broadcasted_iota(jnp.int32, hn.shape, 0)
            shifted = jnp.where(rows == 0, last, pltpu.roll(hn, 1, axis=0))
            xm_ref[rs, :] = (hn + (shifted - hn) * mix).astype(BF16)
            return hn[hn.shape[0] - 1:, :]

        _for_row_blocks(hn_ref.shape[0], body, last)

    o_ref[0] = _dot(xm_ref[...], w_ref[0]).astype(o_ref.dtype)


def _shiftmix_matmul(hn, mix, w, out_dtype, seq, tm=2048, tn=512):
    m, d = hn.shape
    p_cnt, _, n = w.shape
    tm, tn = min(tm, seq), min(tn, n)
    prev_rows = 2 * SUBLANES
    blk = tm // prev_rows
    return pl.pallas_call(
        functools.partial(_shiftmix_matmul_kernel, tiles_per_seq=seq // tm),
        out_shape=jax.ShapeDtypeStruct((p_cnt, m, n), out_dtype),
        grid=(m // tm, p_cnt, n // tn),
        in_specs=[pl.BlockSpec((tm, d), lambda i, p, j: (i, 0), pipeline_mode=pl.Buffered(1)),
                  pl.BlockSpec((prev_rows, d), lambda i, p, j: (jnp.maximum(i * blk - 1, 0), 0)),
                  pl.BlockSpec((1, 1, d), lambda i, p, j: (p, 0, 0)),
                  pl.BlockSpec((1, d, tn), lambda i, p, j: (p, 0, j))],
        out_specs=pl.BlockSpec((1, tm, tn), lambda i, p, j: (p, i, j)),
        scratch_shapes=[pltpu.VMEM((tm, d), BF16)],
        compiler_params=_cp("parallel", "arbitrary", "arbitrary"), name="shiftmix_matmul")(
            hn, hn, mix.reshape(p_cnt, 1, d), w)


def _lowrank_out_kernel(h_ref, w_ref, b_ref, o_ref):
    p = pl.program_id(0)
    h = h_ref[0]
    act = jnp.where(p == 0, jnp.tanh(h), jnp.where(p == 1, h, _sigmoid(h)))
    o_ref[0] = _dot(act.astype(BF16), w_ref[0]) + b_ref[0]


def _lowrank_out(h, w2, bias, tm=512):
    p_cnt, m, r = h.shape
    d = w2.shape[2]
    tm = min(tm, m)
    return pl.pallas_call(
        _lowrank_out_kernel, out_shape=jax.ShapeDtypeStruct((p_cnt, m, d), F32),
        grid=(p_cnt, m // tm),
        in_specs=[pl.BlockSpec((1, tm, r), lambda p, i: (p, i, 0)),
                  pl.BlockSpec((1, r, d), lambda p, i: (p, 0, 0)),
                  pl.BlockSpec((1, 1, d), lambda p, i: (p, 0, 0))],
        out_specs=pl.BlockSpec((1, tm, d), lambda p, i: (p, i, 0)),
        compiler_params=_cp("arbitrary", "arbitrary"), name="lowrank_out")(h, w2, bias)


def _rwkv_kernel(rkv_ref, lr_ref, par_ref, bd_ref, tril_ref, o_ref, st_ref):
    @pl.when(pl.program_id(2) == 0)
    def _():
        st_ref[...] = jnp.zeros_like(st_ref)

    n = RWKV_HEAD
    heads = st_ref.shape[0]
    r = rkv_ref[0].astype(F32)
    k = rkv_ref[1].astype(F32)
    v = rkv_ref[2].astype(F32)
    w_log = -_softplus(-lr_ref[0]) - 0.5
    log_w = -jnp.exp(w_log)
    a = _sigmoid(lr_ref[1])
    gate = lr_ref[2]
    k_k, k_a, r_k = par_ref[0:1, :], par_ref[1:2, :], par_ref[2:3, :]
    lnx_g, lnx_b = par_ref[3:4, :], par_ref[4:5, :]
    bd = bd_ref[...]
    grp = bd.shape[0]

    def head_sums(x):
        x = x.astype(BF16)
        return jnp.concatenate([_dot(x[:, i:i + grp], bd) for i in range(0, x.shape[1], grp)], axis=1)

    kk = k * k_k
    kk = kk / jnp.maximum(jnp.sqrt(head_sums(kk * kk)), 1e-12)
    k2 = k * (1.0 + (a - 1.0) * k_a)
    cw = _dot(tril_ref[...], log_w, HI)
    cw_last = cw[CHUNK - 1:CHUNK, :]
    e_neg = jnp.exp(-cw)
    e_end = jnp.exp(cw_last - cw)
    a_t = (-kk * jnp.exp(cw - log_w)).astype(BF16)
    r_t = (r * jnp.exp(cw)).astype(BF16)
    b_vec = kk * a
    b_t = (b_vec * e_neg).astype(BF16)
    k_t = (k2 * e_neg).astype(BF16)
    b_w = (b_vec * e_end).astype(BF16)
    k_w = (k2 * e_end).astype(BF16)
    w_end = jnp.exp(cw_last)
    v_b = v.astype(BF16)

    rows = lax.broadcasted_iota(jnp.int32, (CHUNK, 2 * CHUNK), 0)
    cols = lax.broadcasted_iota(jnp.int32, (CHUNK, 2 * CHUNK), 1) % CHUNK
    strict = rows > cols
    incl = rows >= cols
    hs = range(heads)
    sls = [slice(h * n, (h + 1) * n) for h in hs]
    ar = [jnp.concatenate([a_t[:, sl], r_t[:, sl]], axis=0) for sl in sls]
    bk = [jnp.concatenate([b_t[:, sl], k_t[:, sl]], axis=0) for sl in sls]
    s0 = [st_ref[h] for h in hs]
    gram = [_dot_nt(ar[h], bk[h]) for h in hs]
    proj = [_dot_nt(ar[h], s0[h].astype(BF16)) for h in hs]
    v_h = [v_b[:, sl] for sl in sls]
    low = [jnp.where(strict, gram[h][:CHUNK], 0.0).astype(BF16) for h in hs]
    t_row = rows[:, :CHUNK]
    t_col = lax.broadcasted_iota(jnp.int32, (CHUNK, CHUNK), 1)
    eye = (t_row == t_col).astype(F32)

    def lower_left(b):
        return (t_row // (2 * b) == t_col // (2 * b)) & (t_row % (2 * b) >= b) & (t_col % (2 * b) < b)

    nil = [low[h][:, :CHUNK] for h in hs]
    t_inv = [eye + jnp.where(lower_left(1), nil[h], 0).astype(F32) for h in hs]
    b = 2
    while b < CHUNK:
        mask = lower_left(b)
        off = [jnp.where(mask, nil[h], 0) for h in hs]
        t_b = [t_inv[h].astype(BF16) for h in hs]
        right = [_dot(off[h], t_b[h]).astype(BF16) for h in hs]
        t_inv = [t_inv[h] + _dot(t_b[h], right[h]) for h in hs]
        b *= 2
    rhs = [proj[h][:CHUNK] + _dot(low[h][:, CHUNK:], v_h[h]) for h in hs]
    u = [_dot(t_inv[h].astype(BF16), rhs[h].astype(BF16)) for h in hs]
    uv =[jnp.concatenate([u[h].astype(BF16), v_h[h]], axis=0) for h in hs]
    upper = [jnp.where(incl, gram[h][CHUNK:], 0.0).astype(BF16) for h in hs]
    y_heads = [proj[h][CHUNK:] + _dot(upper[h], uv[h]) for h in hs]
    for h in hs:
        bkw = jnp.concatenate([b_w[:, sls[h]], k_w[:, sls[h]]], axis=0)
        st_ref[h] = s0[h] * w_end[:, sls[h]] + _dot_tn(uv[h], bkw)

    y = jnp.concatenate(y_heads, axis=1)
    inv_n = 1.0 / n
    mu = head_sums(y) * inv_n
    yc = y - mu
    var = head_sums(yc * yc) * inv_n
    y = yc * lax.rsqrt(var + RWKV_LNX_EPS) * lnx_g + lnx_b
    bonus = head_sums(r * k2 * r_k) * v
    o_ref[...] = ((y + bonus) * gate).astype(o_ref.dtype)


def _rwkv_scan(rkv, lr, params, bsz, seq, heads_per_step=32):
    _, m, d = rkv.shape
    n_heads = d // RWKV_HEAD
    hg = min(heads_per_step, n_heads)
    w = hg * RWKV_HEAD
    nc = seq // CHUNK
    grp = min(2 * LANES, w)
    lane_head = jnp.arange(grp) // RWKV_HEAD
    bd = (lane_head[:, None] == lane_head[None, :]).astype(BF16)
    tril = jnp.tril(jnp.ones((CHUNK, CHUNK), F32))
    n_par = params.shape[0]
    return pl.pallas_call(
        _rwkv_kernel, out_shape=jax.ShapeDtypeStruct((m, d), BF16),
        grid=(bsz, n_heads // hg, nc),
        in_specs=[pl.BlockSpec((3, CHUNK, w), lambda b, g, c: (0, b * nc + c, g)),
                  pl.BlockSpec((3, CHUNK, w), lambda b, g, c: (0, b * nc + c, g)),
                  pl.BlockSpec((n_par, w), lambda b, g, c: (0, g)),
                  pl.BlockSpec((grp, grp), lambda b, g, c: (0, 0)),
                  pl.BlockSpec((CHUNK, CHUNK), lambda b, g, c: (0, 0))],
        out_specs=pl.BlockSpec((CHUNK, w), lambda b, g, c: (b * nc + c, g)),
        scratch_shapes=[pltpu.VMEM((hg, RWKV_HEAD, RWKV_HEAD), F32)],
        compiler_params=_cp("parallel", "parallel", "arbitrary"), name="rwkv_scan")(
            rkv, lr, params, bd, tril)


def _cmul_add(acc_r, acc_i, cr, ci, xr, xi):
    return acc_r + cr * xr - ci * xi, acc_i + cr * xi + ci * xr


def _s5_kernel(h_ref, perm_ref, unperm_ref, wb_ref, wc_ref, coef_ref, pow_ref, d_ref, o_ref, x_ref, carry_ref):
    @pl.when(pl.program_id(2) == 0)
    def _():
        carry_ref[...] = jnp.zeros_like(carry_ref)

    tt = h_ref.shape[0]
    ns = carry_ref.shape[2]
    seg = tt // SUBLANES
    h = _dot(perm_ref[...], h_ref[...]).astype(BF16)
    x_ref[...] = _dot(h, wb_ref[0])
    a_r, a_i = coef_ref[0, 8], coef_ref[0, 9]

    def step_rows(k, c_r, c_i, x_r, x_i):
        rows = pl.ds(pl.multiple_of(k * SUBLANES, SUBLANES), SUBLANES)
        xr, xi = _cmul_add(x_ref[rows, 0:ns], x_ref[rows, ns:2 * ns], c_r, c_i, x_r, x_i)
        x_ref[rows, 0:ns] = xr
        x_ref[rows, ns:2 * ns] = xi
        return xr, xi

    zeros = jnp.zeros((SUBLANES, ns), F32)
    end_r, end_i = lax.fori_loop(0, seg, lambda k, c: step_rows(k, a_r, a_i, *c), (zeros, zeros))

    for s_idx, shift in enumerate((1, 2, 4)):
        end_r, end_i = _cmul_add(end_r, end_i, coef_ref[0, 2 * s_idx], coef_ref[0, 2 * s_idx + 1],
                                 pltpu.roll(end_r, shift, axis=0), pltpu.roll(end_i, shift, axis=0))
    in_r = jnp.broadcast_to(carry_ref[0], (SUBLANES, ns))
    in_i = jnp.broadcast_to(carry_ref[1], (SUBLANES, ns))
    end_r, end_i = _cmul_add(end_r, end_i, coef_ref[0, 6], coef_ref[0, 7], in_r, in_i)
    carry_ref[0] = end_r[SUBLANES - 1:SUBLANES, :]
    carry_ref[1] = end_i[SUBLANES - 1:SUBLANES, :]
    first = lax.broadcasted_iota(jnp.int32, (SUBLANES, ns), 0) == 0
    in_r = jnp.where(first, in_r, pltpu.roll(end_r, 1, axis=0))
    in_i = jnp.where(first, in_i, pltpu.roll(end_i, 1, axis=0))

    def correct(k, c):
        p_r = jnp.broadcast_to(pow_ref[0, 0, pl.ds(k, 1), :], (SUBLANES, ns))
        p_i = jnp.broadcast_to(pow_ref[0, 1, pl.ds(k, 1), :], (SUBLANES, ns))
        step_rows(k, p_r, p_i, in_r, in_i)
        return c

    lax.fori_loop(0, seg, correct, 0, unroll=2)

    y = _dot(x_ref[...].astype(BF16), wc_ref[0]) + d_ref[...] * h.astype(F32)
    y = (0.5 * y * (1.0 + lax.erf(y * (2.0 ** -0.5)))).astype(BF16)
    o_ref[...] = _dot(unperm_ref[...], y).astype(o_ref.dtype)


def _s5_scan(hn, lam_re, lam_im, log_dt, b_re, b_im, c_re, c_im, d_skip, bsz, seq, tt=1024):
    m, d = hn.shape
    n_groups, n_state, grp = b_re.shape
    sg = min(S5_SLAB_GROUPS, n_groups)
    slabs = n_groups // sg
    ch = sg * grp
    ns = sg * n_state
    tt = min(tt, seq)

    lr = jnp.minimum(lam_re.astype(F32), -1e-4)
    li = lam_im.astype(F32)
    dt = jnp.exp(log_dt.astype(F32))[:, None]
    mag = jnp.exp(lr * dt)
    ab_re = mag * jnp.cos(li * dt)
    ab_im = mag * jnp.sin(li * dt)
    den = lr * lr + li * li
    nr, ni = ab_re - 1.0, ab_im
    z_re = (nr * lr + ni * li) / den
    z_im = (ni * lr - nr * li) / den
    bb_re = z_re[..., None] * b_re - z_im[..., None] * b_im
    bb_im = z_re[..., None] * b_im + z_im[..., None] * b_re

    eye = jnp.eye(sg, dtype=F32)

    def block_diag_in(bb):
        t = bb.reshape(slabs, sg, n_state, grp)
        return jnp.einsum('sgph,gk->sghkp', t, eye).reshape(slabs, ch, ns)

    def block_diag_out(cc):
        t = cc.reshape(slabs, sg, grp, n_state)
        return jnp.einsum('sgqp,gk->sgpkq', t, eye).reshape(slabs, ns, ch)

    w_b = jnp.concatenate([block_diag_in(bb_re), block_diag_in(bb_im)], axis=2).astype(BF16)
    w_c = jnp.concatenate([block_diag_out(c_re.astype(F32)), -block_diag_out(c_im.astype(F32))],
                          axis=1).astype(BF16)

    def powers(e):
        e = e[:, None, None]
        mg = jnp.exp(lr[None] * dt[None] * e)
        ang = li[None] * dt[None] * e
        re = (mg * jnp.cos(ang)).reshape(-1, slabs, ns)
        im = (mg * jnp.sin(ang)).reshape(-1, slabs, ns)
        return re, im

    seg = tt // SUBLANES
    row = jnp.arange(SUBLANES, dtype=F32)
    coefs = []
    for shift in (1, 2, 4):
        re, im = powers(jnp.full((SUBLANES,), float(seg * shift), F32))
        mask = (row >= shift)[:, None, None]
        coefs += [jnp.where(mask, re, 0.0), jnp.where(mask, im, 0.0)]
    coefs += list(powers(seg * (row + 1.0)))
    coefs += list(powers(jnp.ones((SUBLANES,), F32)))
    coef = jnp.stack(coefs, axis=0).transpose(2, 0, 1, 3)
    pw = jnp.stack(powers(jnp.arange(1, seg + 1, dtype=F32)), axis=0).transpose(2, 0, 1, 3)
    r_idx = jnp.arange(tt)
    src_time = (r_idx % SUBLANES) * seg + r_idx // SUBLANES
    perm = (src_time[:, None] == jnp.arange(tt)[None, :]).astype(BF16)

    nt = seq // tt
    return pl.pallas_call(
        _s5_kernel, out_shape=jax.ShapeDtypeStruct((m, d), BF16),
        grid=(bsz, slabs, nt),
        in_specs=[pl.BlockSpec((tt, ch), lambda b, s, t: (b * nt + t, s)),
                  pl.BlockSpec((tt, tt), lambda b, s, t: (0, 0)),
                  pl.BlockSpec((tt, tt), lambda b, s, t: (0, 0)),
                  pl.BlockSpec((1, ch, 2 * ns), lambda b, s, t: (s, 0, 0)),
                  pl.BlockSpec((1, 2 * ns, ch), lambda b, s, t: (s, 0, 0)),
                  pl.BlockSpec((1, 10, SUBLANES, ns), lambda b, s, t: (s, 0, 0, 0)),
                  pl.BlockSpec((1, 2, seg, ns), lambda b, s, t: (s, 0, 0, 0)),
                  pl.BlockSpec((1, ch), lambda b, s, t: (0, s))],
        out_specs=pl.BlockSpec((tt, ch), lambda b, s, t: (b * nt + t, s)),
        scratch_shapes=[pltpu.VMEM((tt, 2 * ns), F32), pltpu.VMEM((2, 1, ns), F32)],
        compiler_params=_cp("parallel", "parallel", "arbitrary"), name="s5_scan")(
            hn, perm, perm.T, w_b, w_c, coef, pw, d_skip.reshape(1, d))


def _pad_to(x, axis, size):
    pad = size - x.shape[axis]
    if pad == 0:
        return x
    widths = [(0, 0)] * x.ndim
    widths[axis] = (0, pad)
    return jnp.pad(x, widths)


def _round_up(n, k):
    return (n + k - 1) // k * k


def _gla_layer(hn, g_post, w_in_all, layer, w_gk_up, b_gk_up, o_norm, w_out, bsz, seq):
    w_in = w_in_all[layer]
    rank = w_gk_up.shape[0]
    n_main = w_in.shape[1] - rank
    rank_pad = _round_up(rank, LANES)
    proj = _matmul(hn, w_in_all, BF16, layer=layer, n_out=n_main)
    gk_lo = _matmul(hn, _pad_to(w_in[:, n_main:], 1, rank_pad), F32)
    o = _gla_scan(proj, gk_lo, _pad_to(w_gk_up, 0, rank_pad), b_gk_up, o_norm, bsz, seq)
    return _matmul_norm(o, [w_out.astype(BF16)], g_post)


def _rwkv_layer(hn, g_post, mix, w_r, w_k, w_v, w_o, w0, w_w1, w_w2, a0, w_a1, w_a2,
                w_g1, w_g2, k_k, k_a, r_k, lnx_g, lnx_b, bsz, seq):
    d = hn.shape[1]
    rkv = _shiftmix_matmul(hn, jnp.stack([mix[0], mix[2], mix[3]]),
                           jnp.stack([w_r, w_k, w_v]).astype(BF16), BF16, seq)
    rank_pad = _round_up(max(w_w1.shape[1], w_a1.shape[1], w_g1.shape[1]), LANES)
    w1 = jnp.stack([_pad_to(w, 1, rank_pad) for w in (w_w1, w_a1, w_g1)]).astype(BF16)
    w2 = jnp.stack([_pad_to(w, 0, rank_pad) for w in (w_w2, w_a2, w_g2)]).astype(BF16)
    low = _shiftmix_matmul(hn, jnp.stack([mix[1], mix[4], mix[5]]), w1, F32, seq)
    bias = jnp.stack([w0, a0, jnp.zeros_like(w0)]).reshape(3, 1, d)
    lr = _lowrank_out(low, w2, bias)
    params = _pad_to(jnp.stack([k_k, k_a, r_k, lnx_g, lnx_b]), 0, SUBLANES)
    y = _rwkv_scan(rkv, lr, params, bsz, seq)
    return _matmul_norm(y, [w_o.astype(BF16)], g_post)


def _s5_layer(hn, g_post, lam_re, lam_im, log_dt, b_re, b_im, c_re, c_im, d_skip,
              w_glu1, w_glu2, bsz, seq):
    y = _s5_scan(hn, lam_re, lam_im, log_dt, b_re, b_im, c_re, c_im, d_skip, bsz, seq)
    return _matmul_norm(y, [w_glu1.astype(BF16), w_glu2.astype(BF16)], g_post)


def kernel(x, mem, norm_gains, mem_norm, mem_w_kv, xa_wq, xa_wo, mlp_w1, mlp_w2, gla_w_in, gla_w_gk_up, gla_b_gk_up, gla_o_norm, gla_w_out, rwkv_mix, rwkv_w_r, rwkv_w_k, rwkv_w_v, rwkv_w_o, rwkv_w0, rwkv_w_w1, rwkv_w_w2, rwkv_a0, rwkv_w_a1, rwkv_w_a2, rwkv_w_g1, rwkv_w_g2, rwkv_k_k, rwkv_k_a, rwkv_r_k, rwkv_lnx_g, rwkv_lnx_b, s5_lam_re, s5_lam_im, s5_log_dt, s5_b_re, s5_b_im, s5_c_re, s5_c_im, s5_d, s5_w_glu1, s5_w_glu2):
    bsz, seq, d = x.shape
    n_mem = mem.shape[1]
    depth = norm_gains.shape[0]
    assert seq % CHUNK == 0 and xa_wq.shape[2] == XA_HEADS * XA_HEAD_DIM

    mem_kv = _norm_matmul(mem.reshape(bsz * n_mem, d), mem_norm, mem_w_kv.astype(BF16), BF16)
    mem_kv = mem_kv.reshape(bsz, n_mem, mem_w_kv.shape[1])

    x = x.reshape(bsz * seq, d)
    hn = _norm(x, norm_gains[0, 0], BF16)
    for i in range(depth):
        kind, j = i % N_MIXERS, i // N_MIXERS
        g = norm_gains[i]
        if kind == 0:
            branch = _gla_layer(hn, g[1], gla_w_in, j, gla_w_gk_up[j], gla_b_gk_up[j],
                           gla_o_norm[j], gla_w_out[j], bsz, seq)
        elif kind == 1:
            branch = _rwkv_layer(hn, g[1], rwkv_mix[j], rwkv_w_r[j], rwkv_w_k[j], rwkv_w_v[j],
                            rwkv_w_o[j], rwkv_w0[j], rwkv_w_w1[j], rwkv_w_w2[j], rwkv_a0[j],
                            rwkv_w_a1[j], rwkv_w_a2[j], rwkv_w_g1[j], rwkv_w_g2[j], rwkv_k_k[j],
                            rwkv_k_a[j], rwkv_r_k[j], rwkv_lnx_g[j], rwkv_lnx_b[j], bsz, seq)
        else:
            branch = _s5_layer(hn, g[1], s5_lam_re[j], s5_lam_im[j], s5_log_dt[j], s5_b_re[j],
                          s5_b_im[j], s5_c_re[j], s5_c_im[j], s5_d[j], s5_w_glu1[j], s5_w_glu2[j],
                          bsz, seq)
        x, hn = _cross_attention(x, branch, g[2], xa_wq[i].astype(BF16), mem_kv, xa_wo[i].astype(BF16), g[3],
                                 g[4], seq)
        hidden = _matmul(hn, mlp_w1, BF16, act="relu2", layer=i)
        branch = _matmul_acc(hidden, mlp_w2, layer=i)
        x, hn = _add_norm(x, branch, g[5], norm_gains[i + 1, 0] if i + 1 < depth else None)
    return x.reshape(bsz, seq, d)
```

```python
import functools

import jax
import jax.numpy as jnp
from jax import lax
from jax.experimental import pallas as pl
from jax.experimental.pallas import tpu as pltpu

F32 = jnp.float32
BF16 = jnp.bfloat16

NORM_EPS = 1e-6
CHUNK = 64
N_MIXERS = 3
GLA_HEADS = 4
GLA_GATE_NORMALIZER = 16.0
RWKV_HEAD = 64
RWKV_LNX_EPS = 64e-5
XA_HEADS = 4
XA_HEAD_DIM = 128
S5_SLAB_GROUPS = 16
SUBLANES = 8
LANES = 128
VMEM_LIMIT = 60 * 1024 * 1024


def _cp(*sem):
    return pltpu.CompilerParams(dimension_semantics=sem, vmem_limit_bytes=VMEM_LIMIT)


def _rms(x, gain, eps=NORM_EPS):
    ms = jnp.mean(x * x, axis=-1, keepdims=True)
    return x * lax.rsqrt(ms + eps) * gain


def _stat_scratch(rows):
    return pltpu.VMEM((rows, LANES), F32)


def _for_row_blocks(n_rows, body, carry=None, rows=2 * SUBLANES, unroll=1):
    def step(i, c):
        return body(pl.ds(pl.multiple_of(i * rows, rows), rows), c)
    return lax.fori_loop(0, n_rows // rows, step, carry, unroll=unroll)


def _sumsq_lanes(x):
    acc = x[:, 0:LANES] * x[:, 0:LANES]
    for j in range(1, x.shape[1] // LANES):
        blk = x[:, j * LANES:(j + 1) * LANES]
        acc = acc + blk * blk
    return acc


def _finish_scales(s_ref, d):
    ms = jnp.sum(s_ref[...], axis=-1, keepdims=True) * (1.0 / d)
    s_ref[...] = jnp.broadcast_to(lax.rsqrt(ms + NORM_EPS), s_ref.shape)


def _rows_of(ref):
    return lambda rs: ref[rs, :]


def _row_scales_into(s_ref, load, d):
    def body(rs, c):
        s_ref[rs, :] = _sumsq_lanes(load(rs))
        return c
    _for_row_blocks(s_ref.shape[0], body, rows=SUBLANES, unroll=2)
    _finish_scales(s_ref, d)


def _scaled(x, scale, gain):
    return x * jnp.tile(scale, (1, x.shape[1] // LANES)) * gain


def _norm_rows_into(dst_ref, load, g_ref, s_ref):
    _row_scales_into(s_ref, load, dst_ref.shape[1])

    def body(rs, c):
        dst_ref[rs, :] = _scaled(load(rs), s_ref[rs, :], g_ref[...]).astype(dst_ref.dtype)
        return c
    _for_row_blocks(dst_ref.shape[0], body, unroll=2)


def _add_norm_rows(o_ref, load_m, load_res, g_ref, s_ref, hn_ref=None, g_next_ref=None):
    _row_scales_into(s_ref, load_m, o_ref.shape[1])

    def body(rs, c):
        x_new = load_res(rs) + _scaled(load_m(rs), s_ref[rs, :], g_ref[...])
        o_ref[rs, :] = x_new
        if hn_ref is not None:
            s_ref[rs, :] = _sumsq_lanes(x_new)
        return c
    _for_row_blocks(o_ref.shape[0], body, rows=SUBLANES, unroll=2)
    if hn_ref is not None:
        _finish_scales(s_ref, o_ref.shape[1])

        def body2(rs, c):
            hn_ref[rs, :] = _scaled(o_ref[rs, :], s_ref[rs, :], g_next_ref[...]).astype(hn_ref.dtype)
            return c
        _for_row_blocks(o_ref.shape[0], body2, unroll=2)


def _dot(a, b, precision=None):
    return jnp.dot(a, b, preferred_element_type=F32, precision=precision)


def _dot_nt(a, b, precision=None):
    return lax.dot_general(a, b, (((1,), (1,)), ((), ())), preferred_element_type=F32,
                           precision=precision)


def _dot_tn(a, b, precision=None):
    return lax.dot_general(a, b, (((0,), (0,)), ((), ())), preferred_element_type=F32,
                           precision=precision)


def _split_bf16(x, pieces):
    out = []
    for _ in range(pieces - 1):
        out.append(x.astype(BF16))
        x = x - out[-1].astype(F32)
    return out + [x.astype(BF16)]


def _dot_ones(ones, x):
    hi, mid, lo = _split_bf16(x, 3)
    return _dot(ones, hi) + (_dot(ones, mid) + _dot(ones, lo))


def _dot_split(a, b):
    a_hi, a_lo = _split_bf16(a, 2)
    b_hi, b_lo = _split_bf16(b, 2)
    return _dot(a_hi, b_hi) + (_dot(a_hi, b_lo) + _dot(a_lo, b_hi))


def _sigmoid(x):
    return 1.0 / (1.0 + jnp.exp(-x))


def _softplus(x):
    return jnp.maximum(x, 0.0) + jnp.log1p(jnp.exp(-jnp.abs(x)))


def _norm_kernel(x_ref, g_ref, o_ref, s_ref):
    _norm_rows_into(o_ref, _rows_of(x_ref), g_ref, s_ref)


def _norm(x, gain, out_dtype=F32, tm=256):
    m, d = x.shape
    tm = min(tm, m)
    return pl.pallas_call(
        _norm_kernel, out_shape=jax.ShapeDtypeStruct((m, d), out_dtype),
        grid=(m // tm,),
        in_specs=[pl.BlockSpec((tm, d), lambda i: (i, 0)), pl.BlockSpec((1, d), lambda i: (0, 0))],
        out_specs=pl.BlockSpec((tm, d), lambda i: (i, 0)),
        scratch_shapes=[_stat_scratch(tm)],
        compiler_params=_cp("parallel"), name="rmsnorm")(x, gain.reshape(1, d))


def _norm_matmul_kernel(x_ref, g_ref, w_ref, o_ref, hn_ref, s_ref):
    @pl.when(pl.program_id(1) == 0)
    def _():
        _norm_rows_into(hn_ref, _rows_of(x_ref), g_ref, s_ref)

    o_ref[...] = _dot(hn_ref[...], w_ref[...]).astype(o_ref.dtype)


def _norm_matmul(x, gain, w, out_dtype, tm=512, tn=512):
    m, d = x.shape
    n = w.shape[1]
    tm, tn = min(tm, m), min(tn, n)
    return pl.pallas_call(
        _norm_matmul_kernel, out_shape=jax.ShapeDtypeStruct((m, n), out_dtype),
        grid=(m // tm, n // tn),
        in_specs=[pl.BlockSpec((tm, d), lambda i, j: (i, 0)),
                  pl.BlockSpec((1, d), lambda i, j: (0, 0)),
                  pl.BlockSpec((d, tn), lambda i, j: (0, j))],
        out_specs=pl.BlockSpec((tm, tn), lambda i, j: (i, j)),
        scratch_shapes=[pltpu.VMEM((tm, d), BF16), _stat_scratch(tm)],
        compiler_params=_cp("parallel", "arbitrary"), name="norm_matmul")(x, gain.reshape(1, d), w)


def _matmul_norm_kernel(*refs, glu, tn):
    if glu:
        a_ref, w_ref, w2_ref, g_ref, o_ref, acc_ref, s_ref = refs
    else:
        a_ref, w_ref, g_ref, o_ref, acc_ref, s_ref = refs
    j = pl.program_id(1)
    a = a_ref[...]
    y = _dot(a, w_ref[...])
    if glu:
        y = y * _sigmoid(_dot(a, w2_ref[...]))
    acc_ref[:, pl.ds(pl.multiple_of(j * tn, tn), tn)] = y

    @pl.when(j == pl.num_programs(1) - 1)
    def _():
        _norm_rows_into(o_ref, _rows_of(acc_ref), g_ref, s_ref)


def _matmul_norm(a, ws, gain, tm=1024):
    m, k = a.shape
    n = ws[0].shape[1]
    glu = len(ws) == 2
    tm, tn = min(tm, m), min(256 if glu else 512, n)
    w_specs = [pl.BlockSpec((k, tn), lambda i, j: (0, j)) for _ in ws]
    return pl.pallas_call(
        functools.partial(_matmul_norm_kernel, glu=glu, tn=tn),
        out_shape=jax.ShapeDtypeStruct((m, n), BF16),
        grid=(m // tm, n // tn),
        in_specs=[pl.BlockSpec((tm, k), lambda i, j: (i, 0))] + w_specs + [
            pl.BlockSpec((1, n), lambda i, j: (0, 0))],
        out_specs=pl.BlockSpec((tm, n), lambda i, j: (i, 0)),
        scratch_shapes=[pltpu.VMEM((tm, n), F32), _stat_scratch(tm)],
        compiler_params=_cp("parallel", "arbitrary"),
        name="glu_norm" if glu else "matmul_norm")(a, *ws, gain.reshape(1, n))


def _xa_kernel(x_ref, br_ref, g_in_ref, wq_ref, k_ref, v_ref, wo_ref, g_out_ref, g_next_ref, o_ref, hn_out_ref,
               hn_ref, s_ref):
    def x1(rs):
        return x_ref[rs, :] + br_ref[rs, :].astype(F32)

    _norm_rows_into(hn_ref, x1, g_in_ref, s_ref)
    q = _dot(hn_ref[...], wq_ref[...]) * (XA_HEAD_DIM ** -0.5)
    heads = []
    for h in range(XA_HEADS):
        sl = slice(h * XA_HEAD_DIM, (h + 1) * XA_HEAD_DIM)
        s = _dot_nt(q[:, sl].astype(BF16), k_ref[0, :, sl])
        s = s - jnp.max(s, axis=-1, keepdims=True)
        p = jnp.exp(s)
        p = p / jnp.sum(p, axis=-1, keepdims=True)
        heads.append(_dot(p.astype(BF16), v_ref[0, :, sl]))
    o = jnp.concatenate(heads, axis=-1).astype(BF16)
    o_ref[...] = _dot(o, wo_ref[...])
    _add_norm_rows(o_ref, _rows_of(o_ref), x1, g_out_ref, s_ref, hn_out_ref, g_next_ref)


def _cross_attention(x, branch, g_in, wq, mem_kv, wo, g_out, g_next, seq, tm=256):
    m, d = x.shape
    xw = wq.shape[1]
    n_mem = mem_kv.shape[1]
    tm = min(tm, seq)
    tiles_per_seq = seq // tm
    return pl.pallas_call(
        _xa_kernel, out_shape=[jax.ShapeDtypeStruct((m, d), F32), jax.ShapeDtypeStruct((m, d), BF16)],
        grid=(m // tm,),
        in_specs=[pl.BlockSpec((tm, d), lambda i: (i, 0)),
                  pl.BlockSpec((tm, d), lambda i: (i, 0)),
                  pl.BlockSpec((1, d), lambda i: (0, 0)),
                  pl.BlockSpec((d, xw), lambda i: (0, 0)),
                  pl.BlockSpec((1, n_mem, xw), lambda i: (i // tiles_per_seq, 0, 0)),
                  pl.BlockSpec((1, n_mem, xw), lambda i: (i // tiles_per_seq, 0, 1)),
                  pl.BlockSpec((xw, d), lambda i: (0, 0)),
                  pl.BlockSpec((1, d), lambda i: (0, 0)),
                  pl.BlockSpec((1, d), lambda i: (0, 0))],
        out_specs=[pl.BlockSpec((tm, d), lambda i: (i, 0)), pl.BlockSpec((tm, d), lambda i: (i, 0))],
        scratch_shapes=[pltpu.VMEM((tm, d), BF16), _stat_scratch(tm)],
        compiler_params=_cp("parallel"), name="cross_attention")(
            x, branch, g_in.reshape(1, d), wq, mem_kv, mem_kv, wo, g_out.reshape(1, d), g_next.reshape(1, d))


def _matmul_kernel(a_ref, w_ref, o_ref, *, act, col0, w_cols):
    w = w_ref[...]
    if w_cols is not None:
        tn = w.shape[1]
        cols = col0 + pl.program_id(1) * tn + lax.broadcasted_iota(jnp.int32, (1, tn), 1)
        w = jnp.where(cols < w_cols, w, 0.0)
    y = _dot(a_ref[...], w.astype(BF16))
    if act == "relu2":
        y = jnp.maximum(y, 0.0)
        y = y * y
    o_ref[...] = y.astype(o_ref.dtype)


def _layer_weight_spec(w, layer, block, index_map):
    if w.ndim == 2:
        return pl.BlockSpec(block, index_map)
    return pl.BlockSpec((None,) + block, lambda *idx: (layer,) + index_map(*idx))


def _matmul(a, w, out_dtype, act=None, layer=None, col0=0, n_out=None, tm=2048, tn=512):
    m, k = a.shape
    n = w.shape[-1] if n_out is None else n_out
    tm, tn = min(tm, m), min(tn, n)
    assert col0 % tn == 0
    ragged = col0 + n > w.shape[-1]
    return pl.pallas_call(
        functools.partial(_matmul_kernel, act=act, col0=col0, w_cols=w.shape[-1] if ragged else None),
        out_shape=jax.ShapeDtypeStruct((m, n), out_dtype),
        grid=(m // tm, n // tn),
        in_specs=[pl.BlockSpec((tm, k), lambda i, j: (i, 0), pipeline_mode=pl.Buffered(1)),
                  _layer_weight_spec(w, layer, (k, tn), lambda i, j: (0, col0 // tn + j))],
        out_specs=pl.BlockSpec((tm, tn), lambda i, j: (i, j)),
        compiler_params=_cp("parallel", "arbitrary"), name="matmul")(a, w)


def _matmul_acc_kernel(a_ref, w_ref, o_ref, *, ts):
    @pl.when(pl.program_id(2) == 0)
    def _():
        o_ref[...] = jnp.zeros_like(o_ref)

    a = a_ref[...]
    for n in range(o_ref.shape[1] // ts):
        sl = slice(n * ts, (n + 1) * ts)
        o_ref[:, sl] += _dot(a, w_ref[:, sl].astype(BF16))


def _matmul_acc(a, w, layer=None, tm=2048, tn=1024, tk=2048):
    m, k = a.shape
    n = w.shape[-1]
    tm, tn, tk = min(tm, m), min(tn, n), min(tk, k)
    return pl.pallas_call(
        functools.partial(_matmul_acc_kernel, ts=min(256, tn)),
        out_shape=jax.ShapeDtypeStruct((m, n), F32),
        grid=(m // tm, n // tn, k // tk),
        in_specs=[pl.BlockSpec((tm, tk), lambda i, j, kk: (i, kk)),
                  _layer_weight_spec(w, layer, (tk, tn), lambda i, j, kk: (kk, j))],
        out_specs=pl.BlockSpec((tm, tn), lambda i, j, kk: (i, j)),
        compiler_params=_cp("parallel", "parallel", "arbitrary"), name="matmul_acc")(a, w)


def _add_norm_kernel(x_ref, m_ref, gp_ref, gn_ref, o_ref, *rest):
    s_ref = rest[-1]
    hn_ref = rest[0] if len(rest) == 2 else None
    _add_norm_rows(o_ref, _rows_of(m_ref), _rows_of(x_ref), gp_ref, s_ref, hn_ref, gn_ref)


def _add_norm(x, m_branch, g_post, g_next, tm=256):
    m, d = x.shape
    tm = min(tm, m)
    row = pl.BlockSpec((tm, d), lambda i: (i, 0))
    vec = pl.BlockSpec((1, d), lambda i: (0, 0))
    emit_hn = g_next is not None
    out_shape = [jax.ShapeDtypeStruct((m, d), F32)] + ([jax.ShapeDtypeStruct((m, d), BF16)] if emit_hn else [])
    outs = pl.pallas_call(
        _add_norm_kernel, out_shape=out_shape, grid=(m // tm,),
        in_specs=[row, row, vec, vec], out_specs=[row] * len(out_shape),
        scratch_shapes=[_stat_scratch(tm)],
        compiler_params=_cp("parallel"), name="add_norm")(
            x, m_branch, g_post.reshape(1, d), (g_next if emit_hn else g_post).reshape(1, d))
    return (outs[0], outs[1]) if emit_hn else (outs[0], None)


def _gla_kernel(q_ref, k_ref, v_ref, g_ref, gk_ref, wup_ref, bup_ref, onorm_ref, tril_ref,
                o_ref, st_ref):
    @pl.when(pl.program_id(1) == 0)
    def _():
        st_ref[...] = jnp.zeros_like(st_ref)

    nb, _, dk = q_ref.shape
    bs = range(nb)
    gk = gk_ref[...].reshape(nb * CHUNK, gk_ref.shape[2])
    z = _dot_split(gk, wup_ref[...]) + bup_ref[...]
    log_alpha = (jnp.minimum(z, 0.0) - jnp.log1p(jnp.exp(-jnp.abs(z)))) / GLA_GATE_NORMALIZER
    cum_all = _dot_ones(tril_ref[...], log_alpha)
    cum = [cum_all[b * CHUNK:(b + 1) * CHUNK] for b in bs]
    cum_last = [cum[b][CHUNK - 1:CHUNK, :] for b in bs]
    k_dec = [(k_ref[b].astype(F32) * jnp.exp(cum_last[b] - cum[b])).astype(BF16) for b in bs]
    st = [st_ref[b] * jnp.exp(cum_last[b]) + _dot_tn(v_ref[b], k_dec[b]) for b in bs]
    for b in bs:
        st_ref[b] = st[b]
    q = [(q_ref[b].astype(F32) * dk ** -0.5).astype(BF16) for b in bs]
    o = [_dot_nt(q[b], st[b].astype(BF16)) for b in bs]
    for b in bs:
        g = g_ref[b].astype(F32)
        o_ref[b] = (_rms(o[b], onorm_ref[...]) * (g * _sigmoid(g))).astype(o_ref.dtype)


def _gla_scan(proj, gk_lo, w_up, b_up, o_norm, bsz, seq):
    dk_all = w_up.shape[1]
    dkh = dk_all // GLA_HEADS
    dv_all = (proj.shape[1] - 2 * dk_all) // 2
    dvh = dv_all // GLA_HEADS
    nc = seq // CHUNK
    rank_pad = gk_lo.shape[1]
    proj = proj.reshape(bsz, seq, proj.shape[1])
    gk_lo = gk_lo.reshape(bsz, seq, rank_pad)
    tril = jnp.kron(jnp.eye(bsz, dtype=F32), jnp.tril(jnp.ones((CHUNK, CHUNK), F32))).astype(BF16)
    k_off = dk_all // dkh
    v_off = 2 * dk_all // dvh
    g_off = v_off + dv_all // dvh
    out = pl.pallas_call(
        _gla_kernel, out_shape=jax.ShapeDtypeStruct((bsz, seq, dv_all), BF16),
        grid=(GLA_HEADS, nc),
        in_specs=[pl.BlockSpec((bsz, CHUNK, dkh), lambda h, c: (0, c, h)),
                  pl.BlockSpec((bsz, CHUNK, dkh), lambda h, c: (0, c, k_off + h)),
                  pl.BlockSpec((bsz, CHUNK, dvh), lambda h, c: (0, c, v_off + h)),
                  pl.BlockSpec((bsz, CHUNK, dvh), lambda h, c: (0, c, g_off + h)),
                  pl.BlockSpec((bsz, CHUNK, rank_pad), lambda h, c: (0, c, 0)),
                  pl.BlockSpec((rank_pad, dkh), lambda h, c: (0, h)),
                  pl.BlockSpec((1, dkh), lambda h, c: (0, h)),
                  pl.BlockSpec((1, dvh), lambda h, c: (0, 0)),
                  pl.BlockSpec((bsz * CHUNK, bsz * CHUNK), lambda h, c: (0, 0))],
        out_specs=pl.BlockSpec((bsz, CHUNK, dvh), lambda h, c: (0, c, h)),
        scratch_shapes=[pltpu.VMEM((bsz, dvh, dkh), F32)],
        compiler_params=_cp("parallel", "arbitrary"), name="gla_scan")(
            proj, proj, proj, proj, gk_lo, w_up, b_up.reshape(1, dk_all), o_norm.reshape(1, dvh), tril)
    return out.reshape(bsz * seq, dv_all)


def _shiftmix_matmul_kernel(hn_ref, prev_ref, mix_ref, w_ref, o_ref, xm_ref, *, tiles_per_seq):
    @pl.when(pl.program_id(2) == 0)
    def _():
        mix = mix_ref[0]
        prev_rows = prev_ref.shape[0]
        last = prev_ref[...].astype(F32)[prev_rows - 1:prev_rows, :]
        first_tile = pl.program_id(0) % tiles_per_seq == 0
        last = jnp.where(first_tile, 0.0, last)

        def body(rs, last):
            hn = hn_ref[rs, :].astype(F32)
            rows = lax.broadcasted_iota(jnp.int32, hn.shape, 0)
            shifted = jnp.where(rows == 0, last, pltpu.roll(hn, 1, axis=0))
            xm_ref[rs, :] = (hn + (shifted - hn) * mix).astype(BF16)
            return hn[hn.shape[0] - 1:, :]

        _for_row_blocks(hn_ref.shape[0], body, last)

    o_ref[0] = _dot(xm_ref[...], w_ref[0]).astype(o_ref.dtype)


def _shiftmix_matmul(hn, mix, w, out_dtype, seq, tm=2048, tn=512):
    m, d = hn.shape
    p_cnt, _, n = w.shape
    tm, tn = min(tm, seq), min(tn, n)
    prev_rows = 2 * SUBLANES
    blk = tm // prev_rows
    return pl.pallas_call(
        functools.partial(_shiftmix_matmul_kernel, tiles_per_seq=seq // tm),
        out_shape=jax.ShapeDtypeStruct((p_cnt, m, n), out_dtype),
        grid=(m // tm, p_cnt, n // tn),
        in_specs=[pl.BlockSpec((tm, d), lambda i, p, j: (i, 0), pipeline_mode=pl.Buffered(1)),
                  pl.BlockSpec((prev_rows, d), lambda i, p, j: (jnp.maximum(i * blk - 1, 0), 0)),
                  pl.BlockSpec((1, 1, d), lambda i, p, j: (p, 0, 0)),
                  pl.BlockSpec((1, d, tn), lambda i, p, j: (p, 0, j))],
        out_specs=pl.BlockSpec((1, tm, tn), lambda i, p, j: (p, i, j)),
        scratch_shapes=[pltpu.VMEM((tm, d), BF16)],
        compiler_params=_cp("parallel", "arbitrary", "arbitrary"), name="shiftmix_matmul")(
            hn, hn, mix.reshape(p_cnt, 1, d), w)


def _lowrank_out_kernel(h_ref, w_ref, b_ref, o_ref):
    p = pl.program_id(0)
    h = h_ref[0]
    act = jnp.where(p == 0, jnp.tanh(h), jnp.where(p == 1, h, _sigmoid(h)))
    o_ref[0] = _dot(act.astype(BF16), w_ref[0]) + b_ref[0]


def _lowrank_out(h, w2, bias, tm=512):
    p_cnt, m, r = h.shape
    d = w2.shape[2]
    tm = min(tm, m)
    return pl.pallas_call(
        _lowrank_out_kernel, out_shape=jax.ShapeDtypeStruct((p_cnt, m, d), F32),
        grid=(p_cnt, m // tm),
        in_specs=[pl.BlockSpec((1, tm, r), lambda p, i: (p, i, 0)),
                  pl.BlockSpec((1, r, d), lambda p, i: (p, 0, 0)),
                  pl.BlockSpec((1, 1, d), lambda p, i: (p, 0, 0))],
        out_specs=pl.BlockSpec((1, tm, d), lambda p, i: (p, i, 0)),
        compiler_params=_cp("arbitrary", "arbitrary"), name="lowrank_out")(h, w2, bias)


def _rwkv_kernel(rkv_ref, lr_ref, par_ref, bd_ref, tril_ref, o_ref, st_ref):
    @pl.when(pl.program_id(2) == 0)
    def _():
        st_ref[...] = jnp.zeros_like(st_ref)

    n = RWKV_HEAD
    heads = st_ref.shape[0]
    r = rkv_ref[0].astype(F32)
    k = rkv_ref[1].astype(F32)
    v = rkv_ref[2].astype(F32)
    w_log = -_softplus(-lr_ref[0]) - 0.5
    log_w = -jnp.exp(w_log)
    a = _sigmoid(lr_ref[1])
    gate = lr_ref[2]
    k_k, k_a, r_k = par_ref[0:1, :], par_ref[1:2, :], par_ref[2:3, :]
    lnx_g, lnx_b = par_ref[3:4, :], par_ref[4:5, :]
    bd = bd_ref[...]
    grp = bd.shape[0]

    def head_sums(x):
        x = x.astype(BF16)
        return jnp.concatenate([_dot(x[:, i:i + grp], bd) for i in range(0, x.shape[1], grp)], axis=1)

    kk = k * k_k
    kk = kk / jnp.maximum(jnp.sqrt(head_sums(kk * kk)), 1e-12)
    k2 = k * (1.0 + (a - 1.0) * k_a)
    cw = _dot_ones(tril_ref[...], log_w)
    cw_last = cw[CHUNK - 1:CHUNK, :]
    e_neg = jnp.exp(-cw)
    e_end = jnp.exp(cw_last - cw)
    a_t = (-kk * jnp.exp(cw - log_w)).astype(BF16)
    r_t = (r * jnp.exp(cw)).astype(BF16)
    b_vec = kk * a
    b_t = (b_vec * e_neg).astype(BF16)
    k_t = (k2 * e_neg).astype(BF16)
    b_w = (b_vec * e_end).astype(BF16)
    k_w = (k2 * e_end).astype(BF16)
    w_end = jnp.exp(cw_last)
    v_b = v.astype(BF16)

    rows = lax.broadcasted_iota(jnp.int32, (CHUNK, 2 * CHUNK), 0)
    cols = lax.broadcasted_iota(jnp.int32, (CHUNK, 2 * CHUNK), 1) % CHUNK
    strict = rows > cols
    incl = rows >= cols
    hs = range(heads)
    sls = [slice(h * n, (h + 1) * n) for h in hs]
    ar = [jnp.concatenate([a_t[:, sl], r_t[:, sl]], axis=0) for sl in sls]
    bk = [jnp.concatenate([b_t[:, sl], k_t[:, sl]], axis=0) for sl in sls]
    s0 = [st_ref[h] for h in hs]
    gram = [_dot_nt(ar[h], bk[h]) for h in hs]
    proj = [_dot_nt(ar[h], s0[h].astype(BF16)) for h in hs]
    v_h = [v_b[:, sl] for sl in sls]
    low = [jnp.where(strict, gram[h][:CHUNK], 0.0).astype(BF16) for h in hs]
    t_row = rows[:, :CHUNK]
    t_col = lax.broadcasted_iota(jnp.int32, (CHUNK, CHUNK), 1)
    eye = (t_row == t_col).astype(F32)

    def lower_left(b):
        return (t_row // (2 * b) == t_col // (2 * b)) & (t_row % (2 * b) >= b) & (t_col % (2 * b) < b)

    nil = [low[h][:, :CHUNK] for h in hs]
    t_inv = [eye + jnp.where(lower_left(1), nil[h], 0).astype(F32) for h in hs]
    b = 2
    while b < CHUNK:
        mask = lower_left(b)
        off = [jnp.where(mask, nil[h], 0) for h in hs]
        t_b = [t_inv[h].astype(BF16) for h in hs]
        right = [_dot(off[h], t_b[h]).astype(BF16) for h in hs]
        t_inv = [t_inv[h] + _dot(t_b[h], right[h]) for h in hs]
        b *= 2
    rhs = [proj[h][:CHUNK] + _dot(low[h][:, CHUNK:], v_h[h]) for h in hs]
    u = [_dot(t_inv[h].astype(BF16), rhs[h].astype(BF16)) for h in hs]
    uv =[jnp.concatenate([u[h].astype(BF16), v_h[h]], axis=0) for h in hs]
    upper = [jnp.where(incl, gram[h][CHUNK:], 0.0).astype(BF16) for h in hs]
    y_heads = [proj[h][CHUNK:] + _dot(upper[h], uv[h]) for h in hs]
    for h in hs:
        bkw = jnp.concatenate([b_w[:, sls[h]], k_w[:, sls[h]]], axis=0)
        st_ref[h] = s0[h] * w_end[:, sls[h]] + _dot_tn(uv[h], bkw)

    y = jnp.concatenate(y_heads, axis=1)
    inv_n = 1.0 / n
    mu = head_sums(y) * inv_n
    yc = y - mu
    var = head_sums(yc * yc) * inv_n
    y = yc * lax.rsqrt(var + RWKV_LNX_EPS) * lnx_g + lnx_b
    bonus = head_sums(r * k2 * r_k) * v
    o_ref[...] = ((y + bonus) * gate).astype(o_ref.dtype)


def _rwkv_scan(rkv, lr, params, bsz, seq, heads_per_step=32):
    _, m, d = rkv.shape
    n_heads = d // RWKV_HEAD
    hg = min(heads_per_step, n_heads)
    w = hg * RWKV_HEAD
    nc = seq // CHUNK
    grp = min(2 * LANES, w)
    lane_head = jnp.arange(grp) // RWKV_HEAD
    bd = (lane_head[:, None] == lane_head[None, :]).astype(BF16)
    tril = jnp.tril(jnp.ones((CHUNK, CHUNK), BF16))
    n_par = params.shape[0]
    return pl.pallas_call(
        _rwkv_kernel, out_shape=jax.ShapeDtypeStruct((m, d), BF16),
        grid=(bsz, n_heads // hg, nc),
        in_specs=[pl.BlockSpec((3, CHUNK, w), lambda b, g, c: (0, b * nc + c, g)),
                  pl.BlockSpec((3, CHUNK, w), lambda b, g, c: (0, b * nc + c, g)),
                  pl.BlockSpec((n_par, w), lambda b, g, c: (0, g)),
                  pl.BlockSpec((grp, grp), lambda b, g, c: (0, 0)),
                  pl.BlockSpec((CHUNK, CHUNK), lambda b, g, c: (0, 0))],
        out_specs=pl.BlockSpec((CHUNK, w), lambda b, g, c: (b * nc + c, g)),
        scratch_shapes=[pltpu.VMEM((hg, RWKV_HEAD, RWKV_HEAD), F32)],
        compiler_params=_cp("parallel", "parallel", "arbitrary"), name="rwkv_scan")(
            rkv, lr, params, bd, tril)


def _cmul_add(acc_r, acc_i, cr, ci, xr, xi):
    return acc_r + cr * xr - ci * xi, acc_i + cr * xi + ci * xr


def _s5_kernel(h_ref, perm_ref, unperm_ref, wb_ref, wc_ref, coef_ref, pow_ref, d_ref, o_ref, x_ref, carry_ref):
    @pl.when(pl.program_id(2) == 0)
    def _():
        carry_ref[...] = jnp.zeros_like(carry_ref)

    tt = h_ref.shape[0]
    ns = carry_ref.shape[2]
    seg = tt // SUBLANES
    h = _dot(perm_ref[...], h_ref[...]).astype(BF16)
    x_ref[...] = _dot(h, wb_ref[0])
    a_r, a_i = coef_ref[0, 8], coef_ref[0, 9]

    def step_rows(k, c_r, c_i, x_r, x_i):
        rows = pl.ds(pl.multiple_of(k * SUBLANES, SUBLANES), SUBLANES)
        xr, xi = _cmul_add(x_ref[rows, 0:ns], x_ref[rows, ns:2 * ns], c_r, c_i, x_r, x_i)
        x_ref[rows, 0:ns] = xr
        x_ref[rows, ns:2 * ns] = xi
        return xr, xi

    zeros = jnp.zeros((SUBLANES, ns), F32)
    end_r, end_i = lax.fori_loop(0, seg, lambda k, c: step_rows(k, a_r, a_i, *c), (zeros, zeros))

    for s_idx, shift in enumerate((1, 2, 4)):
        end_r, end_i = _cmul_add(end_r, end_i, coef_ref[0, 2 * s_idx], coef_ref[0, 2 * s_idx + 1],
                                 pltpu.roll(end_r, shift, axis=0), pltpu.roll(end_i, shift, axis=0))
    in_r = jnp.broadcast_to(carry_ref[0], (SUBLANES, ns))
    in_i = jnp.broadcast_to(carry_ref[1], (SUBLANES, ns))
    end_r, end_i = _cmul_add(end_r, end_i, coef_ref[0, 6], coef_ref[0, 7], in_r, in_i)
    carry_ref[0] = end_r[SUBLANES - 1:SUBLANES, :]
    carry_ref[1] = end_i[SUBLANES - 1:SUBLANES, :]
    first = lax.broadcasted_iota(jnp.int32, (SUBLANES, ns), 0) == 0
    in_r = jnp.where(first, in_r, pltpu.roll(end_r, 1, axis=0))
    in_i = jnp.where(first, in_i, pltpu.roll(end_i, 1, axis=0))

    def correct(k, c):
        p_r = jnp.broadcast_to(pow_ref[0, 0, pl.ds(k, 1), :], (SUBLANES, ns))
        p_i = jnp.broadcast_to(pow_ref[0, 1, pl.ds(k, 1), :], (SUBLANES, ns))
        step_rows(k, p_r, p_i, in_r, in_i)
        return c

    lax.fori_loop(0, seg, correct, 0, unroll=2)

    y = _dot(x_ref[...].astype(BF16), wc_ref[0]) + d_ref[...] * h.astype(F32)
    y = (0.5 * y * (1.0 + lax.erf(y * (2.0 ** -0.5)))).astype(BF16)
    o_ref[...] = _dot(unperm_ref[...], y).astype(o_ref.dtype)


def _s5_scan(hn, lam_re, lam_im, log_dt, b_re, b_im, c_re, c_im, d_skip, bsz, seq, tt=1024):
    m, d = hn.shape
    n_groups, n_state, grp = b_re.shape
    sg = min(S5_SLAB_GROUPS, n_groups)
    slabs = n_groups // sg
    ch = sg * grp
    ns = sg * n_state
    tt = min(tt, seq)

    lr = jnp.minimum(lam_re.astype(F32), -1e-4)
    li = lam_im.astype(F32)
    dt = jnp.exp(log_dt.astype(F32))[:, None]
    mag = jnp.exp(lr * dt)
    ab_re = mag * jnp.cos(li * dt)
    ab_im = mag * jnp.sin(li * dt)
    den = lr * lr + li * li
    nr, ni = ab_re - 1.0, ab_im
    z_re = (nr * lr + ni * li) / den
    z_im = (ni * lr - nr * li) / den
    bb_re = z_re[..., None] * b_re - z_im[..., None] * b_im
    bb_im = z_re[..., None] * b_im + z_im[..., None] * b_re

    eye = jnp.eye(sg, dtype=F32)

    def block_diag_in(bb):
        t = bb.reshape(slabs, sg, n_state, grp)
        return jnp.einsum('sgph,gk->sghkp', t, eye).reshape(slabs, ch, ns)

    def block_diag_out(cc):
        t = cc.reshape(slabs, sg, grp, n_state)
        return jnp.einsum('sgqp,gk->sgpkq', t, eye).reshape(slabs, ns, ch)

    w_b = jnp.concatenate([block_diag_in(bb_re), block_diag_in(bb_im)], axis=2).astype(BF16)
    w_c = jnp.concatenate([block_diag_out(c_re.astype(F32)), -block_diag_out(c_im.astype(F32))],
                          axis=1).astype(BF16)

    def powers(e):
        e = e[:, None, None]
        mg = jnp.exp(lr[None] * dt[None] * e)
        ang = li[None] * dt[None] * e
        re = (mg * jnp.cos(ang)).reshape(-1, slabs, ns)
        im = (mg * jnp.sin(ang)).reshape(-1, slabs, ns)
        return re, im

    seg = tt // SUBLANES
    row = jnp.arange(SUBLANES, dtype=F32)
    coefs = []
    for shift in (1, 2, 4):
        re, im = powers(jnp.full((SUBLANES,), float(seg * shift), F32))
        mask = (row >= shift)[:, None, None]
        coefs += [jnp.where(mask, re, 0.0), jnp.where(mask, im, 0.0)]
    coefs += list(powers(seg * (row + 1.0)))
    coefs += list(powers(jnp.ones((SUBLANES,), F32)))
    coef = jnp.stack(coefs, axis=0).transpose(2, 0, 1, 3)
    pw = jnp.stack(powers(jnp.arange(1, seg + 1, dtype=F32)), axis=0).transpose(2, 0, 1, 3)
    r_idx = jnp.arange(tt)
    src_time = (r_idx % SUBLANES) * seg + r_idx // SUBLANES
    perm = (src_time[:, None] == jnp.arange(tt)[None, :]).astype(BF16)

    nt = seq // tt
    return pl.pallas_call(
        _s5_kernel, out_shape=jax.ShapeDtypeStruct((m, d), BF16),
        grid=(bsz, slabs, nt),
        in_specs=[pl.BlockSpec((tt, ch), lambda b, s, t: (b * nt + t, s)),
                  pl.BlockSpec((tt, tt), lambda b, s, t: (0, 0)),
                  pl.BlockSpec((tt, tt), lambda b, s, t: (0, 0)),
                  pl.BlockSpec((1, ch, 2 * ns), lambda b, s, t: (s, 0, 0)),
                  pl.BlockSpec((1, 2 * ns, ch), lambda b, s, t: (s, 0, 0)),
                  pl.BlockSpec((1, 10, SUBLANES, ns), lambda b, s, t: (s, 0, 0, 0)),
                  pl.BlockSpec((1, 2, seg, ns), lambda b, s, t: (s, 0, 0, 0)),
                  pl.BlockSpec((1, ch), lambda b, s, t: (0, s))],
        out_specs=pl.BlockSpec((tt, ch), lambda b, s, t: (b * nt + t, s)),
        scratch_shapes=[pltpu.VMEM((tt, 2 * ns), F32), pltpu.VMEM((2, 1, ns), F32)],
        compiler_params=_cp("parallel", "parallel", "arbitrary"), name="s5_scan")(
            hn, perm, perm.T, w_b, w_c, coef, pw, d_skip.reshape(1, d))


def _pad_to(x, axis, size):
    pad = size - x.shape[axis]
    if pad == 0:
        return x
    widths = [(0, 0)] * x.ndim
    widths[axis] = (0, pad)
    return jnp.pad(x, widths)


def _round_up(n, k):
    return (n + k - 1) // k * k


def _gla_layer(hn, g_post, w_in_all, layer, w_gk_up, b_gk_up, o_norm, w_out, bsz, seq):
    rank = w_gk_up.shape[0]
    n_main = w_in_all.shape[2] - rank
    rank_pad = _round_up(rank, LANES)
    proj = _matmul(hn, w_in_all, BF16, layer=layer, n_out=n_main)
    gk_lo = _matmul(hn, w_in_all, F32, layer=layer, col0=n_main, n_out=rank_pad)
    o = _gla_scan(proj, gk_lo, _pad_to(w_gk_up, 0, rank_pad), b_gk_up, o_norm, bsz, seq)
    return _matmul_norm(o, [w_out.astype(BF16)], g_post)


def _rwkv_layer(hn, g_post, mix, w_r, w_k, w_v, w_o, w0, w_w1, w_w2, a0, w_a1, w_a2,
                w_g1, w_g2, k_k, k_a, r_k, lnx_g, lnx_b, bsz, seq):
    d = hn.shape[1]
    rkv = _shiftmix_matmul(hn, jnp.stack([mix[0], mix[2], mix[3]]),
                           jnp.stack([w_r, w_k, w_v]).astype(BF16), BF16, seq)
    rank_pad = _round_up(max(w_w1.shape[1], w_a1.shape[1], w_g1.shape[1]), LANES)
    w1 = jnp.stack([_pad_to(w, 1, rank_pad) for w in (w_w1, w_a1, w_g1)]).astype(BF16)
    w2 = jnp.stack([_pad_to(w, 0, rank_pad) for w in (w_w2, w_a2, w_g2)]).astype(BF16)
    low = _shiftmix_matmul(hn, jnp.stack([mix[1], mix[4], mix[5]]), w1, F32, seq)
    bias = jnp.stack([w0, a0, jnp.zeros_like(w0)]).reshape(3, 1, d)
    lr = _lowrank_out(low, w2, bias)
    params = _pad_to(jnp.stack([k_k, k_a, r_k, lnx_g, lnx_b]), 0, SUBLANES)
    y = _rwkv_scan(rkv, lr, params, bsz, seq)
    return _matmul_norm(y, [w_o.astype(BF16)], g_post)


def _s5_layer(hn, g_post, lam_re, lam_im, log_dt, b_re, b_im, c_re, c_im, d_skip,
              w_glu1, w_glu2, bsz, seq):
    y = _s5_scan(hn, lam_re, lam_im, log_dt, b_re, b_im, c_re, c_im, d_skip, bsz, seq)
    return _matmul_norm(y, [w_glu1.astype(BF16), w_glu2.astype(BF16)], g_post)


def kernel(x, mem, norm_gains, mem_norm, mem_w_kv, xa_wq, xa_wo, mlp_w1, mlp_w2, gla_w_in, gla_w_gk_up, gla_b_gk_up, gla_o_norm, gla_w_out, rwkv_mix, rwkv_w_r, rwkv_w_k, rwkv_w_v, rwkv_w_o, rwkv_w0, rwkv_w_w1, rwkv_w_w2, rwkv_a0, rwkv_w_a1, rwkv_w_a2, rwkv_w_g1, rwkv_w_g2, rwkv_k_k, rwkv_k_a, rwkv_r_k, rwkv_lnx_g, rwkv_lnx_b, s5_lam_re, s5_lam_im, s5_log_dt, s5_b_re, s5_b_im, s5_c_re, s5_c_im, s5_d, s5_w_glu1, s5_w_glu2):
    bsz, seq, d = x.shape
    n_mem = mem.shape[1]
    depth = norm_gains.shape[0]
    assert seq % CHUNK == 0 and xa_wq.shape[2] == XA_HEADS * XA_HEAD_DIM

    mem_kv = _norm_matmul(mem.reshape(bsz * n_mem, d), mem_norm, mem_w_kv.astype(BF16), BF16)
    mem_kv = mem_kv.reshape(bsz, n_mem, mem_w_kv.shape[1])

    x = x.reshape(bsz * seq, d)
    hn = _norm(x, norm_gains[0, 0], BF16)
    for i in range(depth):
        kind, j = i % N_MIXERS, i // N_MIXERS
        g = norm_gains[i]
        if kind == 0:
            branch = _gla_layer(hn, g[1], gla_w_in, j, gla_w_gk_up[j], gla_b_gk_up[j],
                           gla_o_norm[j], gla_w_out[j], bsz, seq)
        elif kind == 1:
            branch = _rwkv_layer(hn, g[1], rwkv_mix[j], rwkv_w_r[j], rwkv_w_k[j], rwkv_w_v[j],
                            rwkv_w_o[j], rwkv_w0[j], rwkv_w_w1[j], rwkv_w_w2[j], rwkv_a0[j],
                            rwkv_w_a1[j], rwkv_w_a2[j], rwkv_w_g1[j], rwkv_w_g2[j], rwkv_k_k[j],
                            rwkv_k_a[j], rwkv_r_k[j], rwkv_lnx_g[j], rwkv_lnx_b[j], bsz, seq)
        else:
            branch = _s5_layer(hn, g[1], s5_lam_re[j], s5_lam_im[j], s5_log_dt[j], s5_b_re[j],
                          s5_b_im[j], s5_c_re[j], s5_c_im[j], s5_d[j], s5_w_glu1[j], s5_w_glu2[j],
                          bsz, seq)
        x, hn = _cross_attention(x, branch, g[2], xa_wq[i].astype(BF16), mem_kv, xa_wo[i].astype(BF16), g[3],
                                 g[4], seq)
        hidden = _matmul(hn, mlp_w1, BF16, act="relu2", layer=i)
        branch = _matmul_acc(hidden, mlp_w2, layer=i)
        x, hn = _add_norm(x, branch, g[5], norm_gains[i + 1, 0] if i + 1 < depth else None)
    return x.reshape(bsz, seq, d)
```

```python
import functools

import jax
import jax.numpy as jnp
from jax import lax
from jax.experimental import pallas as pl
from jax.experimental.pallas import tpu as pltpu

F32 = jnp.float32
BF16 = jnp.bfloat16

NORM_EPS = 1e-6
CHUNK = 64
N_MIXERS = 3
GLA_HEADS = 4
GLA_GATE_NORMALIZER = 16.0
RWKV_HEAD = 64
RWKV_LNX_EPS = 64e-5
XA_HEADS = 4
XA_HEAD_DIM = 128
S5_SLAB_GROUPS = 16
SUBLANES = 8
LANES = 128
VMEM_LIMIT = 60 * 1024 * 1024


def _cp(*sem):
    return pltpu.CompilerParams(dimension_semantics=sem, vmem_limit_bytes=VMEM_LIMIT)


def _rms(x, gain, eps=NORM_EPS):
    ms = jnp.mean(x * x, axis=-1, keepdims=True)
    return x * lax.rsqrt(ms + eps) * gain


def _stat_scratch(rows):
    return pltpu.VMEM((rows, LANES), F32)


def _for_row_blocks(n_rows, body, carry=None, rows=2 * SUBLANES, unroll=1):
    def step(i, c):
        return body(pl.ds(pl.multiple_of(i * rows, rows), rows), c)
    return lax.fori_loop(0, n_rows // rows, step, carry, unroll=unroll)


def _sumsq_lanes(x):
    acc = x[:, 0:LANES] * x[:, 0:LANES]
    for j in range(1, x.shape[1] // LANES):
        blk = x[:, j * LANES:(j + 1) * LANES]
        acc = acc + blk * blk
    return acc


def _finish_scales(s_ref, d):
    ms = jnp.sum(s_ref[...], axis=-1, keepdims=True) * (1.0 / d)
    s_ref[...] = jnp.broadcast_to(lax.rsqrt(ms + NORM_EPS), s_ref.shape)


def _rows_of(ref):
    return lambda rs: ref[rs, :]


def _row_scales_into(s_ref, load, d):
    def body(rs, c):
        s_ref[rs, :] = _sumsq_lanes(load(rs))
        return c
    _for_row_blocks(s_ref.shape[0], body, rows=SUBLANES, unroll=2)
    _finish_scales(s_ref, d)


def _scaled(x, scale, gain):
    return x * jnp.tile(scale, (1, x.shape[1] // LANES)) * gain


def _norm_rows_into(dst_ref, load, g_ref, s_ref):
    _row_scales_into(s_ref, load, dst_ref.shape[1])

    def body(rs, c):
        dst_ref[rs, :] = _scaled(load(rs), s_ref[rs, :], g_ref[...]).astype(dst_ref.dtype)
        return c
    _for_row_blocks(dst_ref.shape[0], body, unroll=2)


def _add_norm_rows(o_ref, load_m, load_res, g_ref, s_ref, hn_ref=None, g_next_ref=None):
    _row_scales_into(s_ref, load_m, o_ref.shape[1])

    def body(rs, c):
        x_new = load_res(rs) + _scaled(load_m(rs), s_ref[rs, :], g_ref[...])
        o_ref[rs, :] = x_new
        if hn_ref is not None:
            s_ref[rs, :] = _sumsq_lanes(x_new)
        return c
    _for_row_blocks(o_ref.shape[0], body, rows=SUBLANES, unroll=2)
    if hn_ref is not None:
        _finish_scales(s_ref, o_ref.shape[1])

        def body2(rs, c):
            hn_ref[rs, :] = _scaled(o_ref[rs, :], s_ref[rs, :], g_next_ref[...]).astype(hn_ref.dtype)
            return c
        _for_row_blocks(o_ref.shape[0], body2, unroll=2)


def _dot(a, b, precision=None):
    return jnp.dot(a, b, preferred_element_type=F32, precision=precision)


def _dot_nt(a, b, precision=None):
    return lax.dot_general(a, b, (((1,), (1,)), ((), ())), preferred_element_type=F32,
                           precision=precision)


def _dot_tn(a, b, precision=None):
    return lax.dot_general(a, b, (((0,), (0,)), ((), ())), preferred_element_type=F32,
                           precision=precision)


def _split_bf16(x, pieces):
    out = []
    for _ in range(pieces - 1):
        out.append(x.astype(BF16))
        x = x - out[-1].astype(F32)
    return out + [x.astype(BF16)]


def _dot_ones(ones, x):
    hi, mid, lo = _split_bf16(x, 3)
    return _dot(ones, hi) + (_dot(ones, mid) + _dot(ones, lo))


def _dot_split(a, b):
    a_hi, a_lo = _split_bf16(a, 2)
    b_hi, b_lo = _split_bf16(b, 2)
    return _dot(a_hi, b_hi) + (_dot(a_hi, b_lo) + _dot(a_lo, b_hi))


def _sigmoid(x):
    return 1.0 / (1.0 + jnp.exp(-x))


def _softplus(x):
    return jnp.maximum(x, 0.0) + jnp.log1p(jnp.exp(-jnp.abs(x)))


def _norm_kernel(x_ref, g_ref, o_ref, s_ref):
    _norm_rows_into(o_ref, _rows_of(x_ref), g_ref, s_ref)


def _norm(x, gain, out_dtype=F32, tm=256):
    m, d = x.shape
    tm = min(tm, m)
    return pl.pallas_call(
        _norm_kernel, out_shape=jax.ShapeDtypeStruct((m, d), out_dtype),
        grid=(m // tm,),
        in_specs=[pl.BlockSpec((tm, d), lambda i: (i, 0)), pl.BlockSpec((1, d), lambda i: (0, 0))],
        out_specs=pl.BlockSpec((tm, d), lambda i: (i, 0)),
        scratch_shapes=[_stat_scratch(tm)],
        compiler_params=_cp("parallel"), name="rmsnorm")(x, gain.reshape(1, d))


def _norm_matmul_kernel(x_ref, g_ref, w_ref, o_ref, hn_ref, s_ref):
    @pl.when(pl.program_id(1) == 0)
    def _():
        _norm_rows_into(hn_ref, _rows_of(x_ref), g_ref, s_ref)

    o_ref[...] = _dot(hn_ref[...], w_ref[...]).astype(o_ref.dtype)


def _norm_matmul(x, gain, w, out_dtype, tm=512, tn=512):
    m, d = x.shape
    n = w.shape[1]
    tm, tn = min(tm, m), min(tn, n)
    return pl.pallas_call(
        _norm_matmul_kernel, out_shape=jax.ShapeDtypeStruct((m, n), out_dtype),
        grid=(m // tm, n // tn),
        in_specs=[pl.BlockSpec((tm, d), lambda i, j: (i, 0)),
                  pl.BlockSpec((1, d), lambda i, j: (0, 0)),
                  pl.BlockSpec((d, tn), lambda i, j: (0, j))],
        out_specs=pl.BlockSpec((tm, tn), lambda i, j: (i, j)),
        scratch_shapes=[pltpu.VMEM((tm, d), BF16), _stat_scratch(tm)],
        compiler_params=_cp("parallel", "arbitrary"), name="norm_matmul")(x, gain.reshape(1, d), w)


def _matmul_norm_kernel(*refs, glu, tn):
    if glu:
        a_ref, w_ref, w2_ref, g_ref, o_ref, acc_ref, s_ref = refs
    else:
        a_ref, w_ref, g_ref, o_ref, acc_ref, s_ref = refs
    j = pl.program_id(1)
    a = a_ref[...]
    y = _dot(a, w_ref[...])
    if glu:
        y = y * _sigmoid(_dot(a, w2_ref[...]))
    acc_ref[:, pl.ds(pl.multiple_of(j * tn, tn), tn)] = y

    @pl.when(j == pl.num_programs(1) - 1)
    def _():
        _norm_rows_into(o_ref, _rows_of(acc_ref), g_ref, s_ref)


def _matmul_norm(a, ws, gain, tm=1024):
    m, k = a.shape
    n = ws[0].shape[1]
    glu = len(ws) == 2
    tm, tn = min(tm, m), min(256 if glu else 512, n)
    w_specs = [pl.BlockSpec((k, tn), lambda i, j: (0, j)) for _ in ws]
    return pl.pallas_call(
        functools.partial(_matmul_norm_kernel, glu=glu, tn=tn),
        out_shape=jax.ShapeDtypeStruct((m, n), BF16),
        grid=(m // tm, n // tn),
        in_specs=[pl.BlockSpec((tm, k), lambda i, j: (i, 0))] + w_specs + [
            pl.BlockSpec((1, n), lambda i, j: (0, 0))],
        out_specs=pl.BlockSpec((tm, n), lambda i, j: (i, 0)),
        scratch_shapes=[pltpu.VMEM((tm, n), F32), _stat_scratch(tm)],
        compiler_params=_cp("parallel", "arbitrary"),
        name="glu_norm" if glu else "matmul_norm")(a, *ws, gain.reshape(1, n))


def _xa_kernel(x_ref, br_ref, g_in_ref, wq_ref, k_ref, v_ref, wo_ref, g_out_ref, g_next_ref, o_ref, hn_out_ref,
               hn_ref, s_ref):
    def x1(rs):
        return x_ref[rs, :] + br_ref[rs, :].astype(F32)

    _norm_rows_into(hn_ref, x1, g_in_ref, s_ref)
    q = _dot(hn_ref[...], wq_ref[...]) * (XA_HEAD_DIM ** -0.5)
    heads = []
    for h in range(XA_HEADS):
        sl = slice(h * XA_HEAD_DIM, (h + 1) * XA_HEAD_DIM)
        s = _dot_nt(q[:, sl].astype(BF16), k_ref[0, :, sl])
        s = s - jnp.max(s, axis=-1, keepdims=True)
        p = jnp.exp(s)
        p = p / jnp.sum(p, axis=-1, keepdims=True)
        heads.append(_dot(p.astype(BF16), v_ref[0, :, sl]))
    o = jnp.concatenate(heads, axis=-1).astype(BF16)
    o_ref[...] = _dot(o, wo_ref[...])
    _add_norm_rows(o_ref, _rows_of(o_ref), x1, g_out_ref, s_ref, hn_out_ref, g_next_ref)


def _cross_attention(x, branch, g_in, wq, mem_kv, wo, g_out, g_next, seq, tm=256):
    m, d = x.shape
    xw = wq.shape[1]
    n_mem = mem_kv.shape[1]
    tm = min(tm, seq)
    tiles_per_seq = seq // tm
    return pl.pallas_call(
        _xa_kernel, out_shape=[jax.ShapeDtypeStruct((m, d), F32), jax.ShapeDtypeStruct((m, d), BF16)],
        grid=(m // tm,),
        in_specs=[pl.BlockSpec((tm, d), lambda i: (i, 0)),
                  pl.BlockSpec((tm, d), lambda i: (i, 0)),
                  pl.BlockSpec((1, d), lambda i: (0, 0)),
                  pl.BlockSpec((d, xw), lambda i: (0, 0)),
                  pl.BlockSpec((1, n_mem, xw), lambda i: (i // tiles_per_seq, 0, 0)),
                  pl.BlockSpec((1, n_mem, xw), lambda i: (i // tiles_per_seq, 0, 1)),
                  pl.BlockSpec((xw, d), lambda i: (0, 0)),
                  pl.BlockSpec((1, d), lambda i: (0, 0)),
                  pl.BlockSpec((1, d), lambda i: (0, 0))],
        out_specs=[pl.BlockSpec((tm, d), lambda i: (i, 0)), pl.BlockSpec((tm, d), lambda i: (i, 0))],
        scratch_shapes=[pltpu.VMEM((tm, d), BF16), _stat_scratch(tm)],
        compiler_params=_cp("parallel"), name="cross_attention")(
            x, branch, g_in.reshape(1, d), wq, mem_kv, mem_kv, wo, g_out.reshape(1, d), g_next.reshape(1, d))


def _matmul_kernel(a_ref, w_ref, o_ref, *, act, col0, w_cols, w_is_transposed):
    w = w_ref[...]
    if w_cols is not None:
        tn = o_ref.shape[1]
        shape, axis = ((tn, 1), 0) if w_is_transposed else ((1, tn), 1)
        cols = col0 + pl.program_id(1) * tn + lax.broadcasted_iota(jnp.int32, shape, axis)
        w = jnp.where(cols < w_cols, w, 0.0)
    w = w.astype(BF16)
    y = _dot_nt(a_ref[...], w) if w_is_transposed else _dot(a_ref[...], w)
    if act == "relu2":
        y = jnp.maximum(y, 0.0)
        y = y * y
    o_ref[...] = y.astype(o_ref.dtype)


def _layer_weight_spec(w, layer, block, index_map):
    if w.ndim == 2:
        return pl.BlockSpec(block, index_map)
    return pl.BlockSpec((None,) + block, lambda *idx: (layer,) + index_map(*idx))


def _matmul(a, w, out_dtype, act=None, layer=None, col0=0, n_out=None, w_is_transposed=False, tm=2048, tn=512):
    m, k = a.shape
    w_cols = w.shape[-2] if w_is_transposed else w.shape[-1]
    n = w_cols if n_out is None else n_out
    tm, tn = min(tm, m), min(tn, n)
    assert col0 % tn == 0
    ragged = col0 + n > w_cols
    block, index_map = (((tn, k), lambda i, j: (col0 // tn + j, 0)) if w_is_transposed
                        else ((k, tn), lambda i, j: (0, col0 // tn + j)))
    return pl.pallas_call(
        functools.partial(_matmul_kernel, act=act, col0=col0, w_cols=w_cols if ragged else None,
                          w_is_transposed=w_is_transposed),
        out_shape=jax.ShapeDtypeStruct((m, n), out_dtype),
        grid=(m // tm, n // tn),
        in_specs=[pl.BlockSpec((tm, k), lambda i, j: (i, 0), pipeline_mode=pl.Buffered(1)),
                  _layer_weight_spec(w, layer, block, index_map)],
        out_specs=pl.BlockSpec((tm, tn), lambda i, j: (i, j)),
        compiler_params=_cp("parallel", "arbitrary"), name="matmul")(a, w)


def _matmul_acc_kernel(a_ref, w_ref, o_ref, *, ts):
    @pl.when(pl.program_id(2) == 0)
    def _():
        o_ref[...] = jnp.zeros_like(o_ref)

    a = a_ref[...]
    for n in range(o_ref.shape[1] // ts):
        sl = slice(n * ts, (n + 1) * ts)
        o_ref[:, sl] += _dot(a, w_ref[:, sl].astype(BF16))


def _matmul_acc(a, w, layer=None, tm=2048, tn=1024, tk=2048):
    m, k = a.shape
    n = w.shape[-1]
    tm, tn, tk = min(tm, m), min(tn, n), min(tk, k)
    return pl.pallas_call(
        functools.partial(_matmul_acc_kernel, ts=min(256, tn)),
        out_shape=jax.ShapeDtypeStruct((m, n), F32),
        grid=(m // tm, n // tn, k // tk),
        in_specs=[pl.BlockSpec((tm, tk), lambda i, j, kk: (i, kk)),
                  _layer_weight_spec(w, layer, (tk, tn), lambda i, j, kk: (kk, j))],
        out_specs=pl.BlockSpec((tm, tn), lambda i, j, kk: (i, j)),
        compiler_params=_cp("parallel", "parallel", "arbitrary"), name="matmul_acc")(a, w)


def _add_norm_kernel(x_ref, m_ref, gp_ref, gn_ref, o_ref, *rest):
    s_ref = rest[-1]
    hn_ref = rest[0] if len(rest) == 2 else None
    _add_norm_rows(o_ref, _rows_of(m_ref), _rows_of(x_ref), gp_ref, s_ref, hn_ref, gn_ref)


def _add_norm(x, m_branch, g_post, g_next, tm=256):
    m, d = x.shape
    tm = min(tm, m)
    row = pl.BlockSpec((tm, d), lambda i: (i, 0))
    vec = pl.BlockSpec((1, d), lambda i: (0, 0))
    emit_hn = g_next is not None
    out_shape = [jax.ShapeDtypeStruct((m, d), F32)] + ([jax.ShapeDtypeStruct((m, d), BF16)] if emit_hn else [])
    outs = pl.pallas_call(
        _add_norm_kernel, out_shape=out_shape, grid=(m // tm,),
        in_specs=[row, row, vec, vec], out_specs=[row] * len(out_shape),
        scratch_shapes=[_stat_scratch(tm)],
        compiler_params=_cp("parallel"), name="add_norm")(
            x, m_branch, g_post.reshape(1, d), (g_next if emit_hn else g_post).reshape(1, d))
    return (outs[0], outs[1]) if emit_hn else (outs[0], None)


def _gla_kernel(q_ref, k_ref, v_ref, g_ref, gk_ref, wup_ref, bup_ref, onorm_ref, tril_ref,
                o_ref, st_ref):
    @pl.when(pl.program_id(1) == 0)
    def _():
        st_ref[...] = jnp.zeros_like(st_ref)

    nb, _, dk = q_ref.shape
    bs = range(nb)
    gk = gk_ref[...].reshape(nb * CHUNK, gk_ref.shape[2])
    z = _dot_split(gk, wup_ref[...]) + bup_ref[...]
    log_alpha = (jnp.minimum(z, 0.0) - jnp.log1p(jnp.exp(-jnp.abs(z)))) / GLA_GATE_NORMALIZER
    cum_all = _dot_ones(tril_ref[...], log_alpha)
    cum = [cum_all[b * CHUNK:(b + 1) * CHUNK] for b in bs]
    cum_last = [cum[b][CHUNK - 1:CHUNK, :] for b in bs]
    k_dec = [(k_ref[b].astype(F32) * jnp.exp(cum_last[b] - cum[b])).astype(BF16) for b in bs]
    st = [st_ref[b] * jnp.exp(cum_last[b]) + _dot_tn(v_ref[b], k_dec[b]) for b in bs]
    for b in bs:
        st_ref[b] = st[b]
    q = [(q_ref[b].astype(F32) * dk ** -0.5).astype(BF16) for b in bs]
    o = [_dot_nt(q[b], st[b].astype(BF16)) for b in bs]
    for b in bs:
        g = g_ref[b].astype(F32)
        o_ref[b] = (_rms(o[b], onorm_ref[...]) * (g * _sigmoid(g))).astype(o_ref.dtype)


def _gla_scan(proj, gk_lo, w_up, b_up, o_norm, bsz, seq):
    dk_all = w_up.shape[1]
    dkh = dk_all // GLA_HEADS
    dv_all = (proj.shape[1] - 2 * dk_all) // 2
    dvh = dv_all // GLA_HEADS
    nc = seq // CHUNK
    rank_pad = gk_lo.shape[1]
    proj = proj.reshape(bsz, seq, proj.shape[1])
    gk_lo = gk_lo.reshape(bsz, seq, rank_pad)
    tril = jnp.kron(jnp.eye(bsz, dtype=F32), jnp.tril(jnp.ones((CHUNK, CHUNK), F32))).astype(BF16)
    k_off = dk_all // dkh
    v_off = 2 * dk_all // dvh
    g_off = v_off + dv_all // dvh
    out = pl.pallas_call(
        _gla_kernel, out_shape=jax.ShapeDtypeStruct((bsz, seq, dv_all), BF16),
        grid=(GLA_HEADS, nc),
        in_specs=[pl.BlockSpec((bsz, CHUNK, dkh), lambda h, c: (0, c, h)),
                  pl.BlockSpec((bsz, CHUNK, dkh), lambda h, c: (0, c, k_off + h)),
                  pl.BlockSpec((bsz, CHUNK, dvh), lambda h, c: (0, c, v_off + h)),
                  pl.BlockSpec((bsz, CHUNK, dvh), lambda h, c: (0, c, g_off + h)),
                  pl.BlockSpec((bsz, CHUNK, rank_pad), lambda h, c: (0, c, 0)),
                  pl.BlockSpec((rank_pad, dkh), lambda h, c: (0, h)),
                  pl.BlockSpec((1, dkh), lambda h, c: (0, h)),
                  pl.BlockSpec((1, dvh), lambda h, c: (0, 0)),
                  pl.BlockSpec((bsz * CHUNK, bsz * CHUNK), lambda h, c: (0, 0))],
        out_specs=pl.BlockSpec((bsz, CHUNK, dvh), lambda h, c: (0, c, h)),
        scratch_shapes=[pltpu.VMEM((bsz, dvh, dkh), F32)],
        compiler_params=_cp("parallel", "arbitrary"), name="gla_scan")(
            proj, proj, proj, proj, gk_lo, w_up, b_up.reshape(1, dk_all), o_norm.reshape(1, dvh), tril)
    return out.reshape(bsz * seq, dv_all)


def _shiftmix_matmul_kernel(hn_ref, prev_ref, mix_ref, w_ref, o_ref, xm_ref, *, tiles_per_seq):
    @pl.when(pl.program_id(2) == 0)
    def _():
        mix = mix_ref[0]
        prev_rows = prev_ref.shape[0]
        last = prev_ref[...].astype(F32)[prev_rows - 1:prev_rows, :]
        first_tile = pl.program_id(0) % tiles_per_seq == 0
        last = jnp.where(first_tile, 0.0, last)

        def body(rs, last):
            hn = hn_ref[rs, :].astype(F32)
            rows = lax.broadcasted_iota(jnp.int32, hn.shape, 0)
            shifted = jnp.where(rows == 0, last, pltpu.roll(hn, 1, axis=0))
            xm_ref[rs, :] = (hn + (shifted - hn) * mix).astype(BF16)
            return hn[hn.shape[0] - 1:, :]

        _for_row_blocks(hn_ref.shape[0], body, last)

    o_ref[0] = _dot(xm_ref[...], w_ref[0]).astype(o_ref.dtype)


def _shiftmix_matmul(hn, mix, w, out_dtype, seq, tm=2048, tn=512):
    m, d = hn.shape
    p_cnt, _, n = w.shape
    tm, tn = min(tm, seq), min(tn, n)
    prev_rows = 2 * SUBLANES
    blk = tm // prev_rows
    return pl.pallas_call(
        functools.partial(_shiftmix_matmul_kernel, tiles_per_seq=seq // tm),
        out_shape=jax.ShapeDtypeStruct((p_cnt, m, n), out_dtype),
        grid=(m // tm, p_cnt, n // tn),
        in_specs=[pl.BlockSpec((tm, d), lambda i, p, j: (i, 0), pipeline_mode=pl.Buffered(1)),
                  pl.BlockSpec((prev_rows, d), lambda i, p, j: (jnp.maximum(i * blk - 1, 0), 0)),
                  pl.BlockSpec((1, 1, d), lambda i, p, j: (p, 0, 0)),
                  pl.BlockSpec((1, d, tn), lambda i, p, j: (p, 0, j))],
        out_specs=pl.BlockSpec((1, tm, tn), lambda i, p, j: (p, i, j)),
        scratch_shapes=[pltpu.VMEM((tm, d), BF16)],
        compiler_params=_cp("parallel", "arbitrary", "arbitrary"), name="shiftmix_matmul")(
            hn, hn, mix.reshape(p_cnt, 1, d), w)


def _lowrank_out_kernel(h_ref, w_ref, b_ref, o_ref):
    p = pl.program_id(0)
    h = h_ref[0]
    act = jnp.where(p == 0, jnp.tanh(h), jnp.where(p == 1, h, _sigmoid(h)))
    o_ref[0] = _dot(act.astype(BF16), w_ref[0]) + b_ref[0]


def _lowrank_out(h, w2, bias, tm=512):
    p_cnt, m, r = h.shape
    d = w2.shape[2]
    tm = min(tm, m)
    return pl.pallas_call(
        _lowrank_out_kernel, out_shape=jax.ShapeDtypeStruct((p_cnt, m, d), F32),
        grid=(p_cnt, m // tm),
        in_specs=[pl.BlockSpec((1, tm, r), lambda p, i: (p, i, 0)),
                  pl.BlockSpec((1, r, d), lambda p, i: (p, 0, 0)),
                  pl.BlockSpec((1, 1, d), lambda p, i: (p, 0, 0))],
        out_specs=pl.BlockSpec((1, tm, d), lambda p, i: (p, i, 0)),
        compiler_params=_cp("arbitrary", "arbitrary"), name="lowrank_out")(h, w2, bias)


def _rwkv_kernel(rkv_ref, lr_ref, par_ref, bd_ref, tril_ref, o_ref, st_ref):
    @pl.when(pl.program_id(2) == 0)
    def _():
        st_ref[...] = jnp.zeros_like(st_ref)

    n = RWKV_HEAD
    heads = st_ref.shape[0]
    r = rkv_ref[0].astype(F32)
    k = rkv_ref[1].astype(F32)
    v = rkv_ref[2].astype(F32)
    w_log = -_softplus(-lr_ref[0]) - 0.5
    log_w = -jnp.exp(w_log)
    a = _sigmoid(lr_ref[1])
    gate = lr_ref[2]
    k_k, k_a, r_k = par_ref[0:1, :], par_ref[1:2, :], par_ref[2:3, :]
    lnx_g, lnx_b = par_ref[3:4, :], par_ref[4:5, :]
    bd = bd_ref[...]
    grp = bd.shape[0]

    def head_sums(x):
        x = x.astype(BF16)
        return jnp.concatenate([_dot(x[:, i:i + grp], bd) for i in range(0, x.shape[1], grp)], axis=1)

    kk = k * k_k
    kk = kk / jnp.maximum(jnp.sqrt(head_sums(kk * kk)), 1e-12)
    k2 = k * (1.0 + (a - 1.0) * k_a)
    cw = _dot_ones(tril_ref[...], log_w)
    cw_last = cw[CHUNK - 1:CHUNK, :]
    e_neg = jnp.exp(-cw)
    e_end = jnp.exp(cw_last - cw)
    a_t = (-kk * jnp.exp(cw - log_w)).astype(BF16)
    r_t = (r * jnp.exp(cw)).astype(BF16)
    b_vec = kk * a
    b_t = (b_vec * e_neg).astype(BF16)
    k_t = (k2 * e_neg).astype(BF16)
    b_w = (b_vec * e_end).astype(BF16)
    k_w = (k2 * e_end).astype(BF16)
    w_end = jnp.exp(cw_last)
    v_b = v.astype(BF16)

    rows = lax.broadcasted_iota(jnp.int32, (CHUNK, 2 * CHUNK), 0)
    cols = lax.broadcasted_iota(jnp.int32, (CHUNK, 2 * CHUNK), 1) % CHUNK
    strict = rows > cols
    incl = rows >= cols
    hs = range(heads)
    sls = [slice(h * n, (h + 1) * n) for h in hs]
    ar = [jnp.concatenate([a_t[:, sl], r_t[:, sl]], axis=0) for sl in sls]
    bk = [jnp.concatenate([b_t[:, sl], k_t[:, sl]], axis=0) for sl in sls]
    s0 = [st_ref[h] for h in hs]
    gram = [_dot_nt(ar[h], bk[h]) for h in hs]
    proj = [_dot_nt(ar[h], s0[h].astype(BF16)) for h in hs]
    v_h = [v_b[:, sl] for sl in sls]
    low = [jnp.where(strict, gram[h][:CHUNK], 0.0).astype(BF16) for h in hs]
    t_row = rows[:, :CHUNK]
    t_col = lax.broadcasted_iota(jnp.int32, (CHUNK, CHUNK), 1)
    eye = (t_row == t_col).astype(F32)

    def lower_left(b):
        return (t_row // (2 * b) == t_col // (2 * b)) & (t_row % (2 * b) >= b) & (t_col % (2 * b) < b)

    nil = [low[h][:, :CHUNK] for h in hs]
    t_inv = [eye + jnp.where(lower_left(1), nil[h], 0).astype(F32) for h in hs]
    b = 2
    while b < CHUNK:
        mask = lower_left(b)
        off = [jnp.where(mask, nil[h], 0) for h in hs]
        t_b = [t_inv[h].astype(BF16) for h in hs]
        right = [_dot(off[h], t_b[h]).astype(BF16) for h in hs]
        t_inv = [t_inv[h] + _dot(t_b[h], right[h]) for h in hs]
        b *= 2
    rhs = [proj[h][:CHUNK] + _dot(low[h][:, CHUNK:], v_h[h]) for h in hs]
    u = [_dot(t_inv[h].astype(BF16), rhs[h].astype(BF16)) for h in hs]
    uv =[jnp.concatenate([u[h].astype(BF16), v_h[h]], axis=0) for h in hs]
    upper = [jnp.where(incl, gram[h][CHUNK:], 0.0).astype(BF16) for h in hs]
    y_heads = [proj[h][CHUNK:] + _dot(upper[h], uv[h]) for h in hs]
    for h in hs:
        bkw = jnp.concatenate([b_w[:, sls[h]], k_w[:, sls[h]]], axis=0)
        st_ref[h] = s0[h] * w_end[:, sls[h]] + _dot_tn(uv[h], bkw)

    y = jnp.concatenate(y_heads, axis=1)
    inv_n = 1.0 / n
    mu = head_sums(y) * inv_n
    yc = y - mu
    var = head_sums(yc * yc) * inv_n
    y = yc * lax.rsqrt(var + RWKV_LNX_EPS) * lnx_g + lnx_b
    bonus = head_sums(r * k2 * r_k) * v
    o_ref[...] = ((y + bonus) * gate).astype(o_ref.dtype)


def _rwkv_scan(rkv, lr, params, bsz, seq, heads_per_step=32):
    _, m, d = rkv.shape
    n_heads = d // RWKV_HEAD
    hg = min(heads_per_step, n_heads)
    w = hg * RWKV_HEAD
    nc = seq // CHUNK
    grp = min(2 * LANES, w)
    lane_head = jnp.arange(grp) // RWKV_HEAD
    bd = (lane_head[:, None] == lane_head[None, :]).astype(BF16)
    tril = jnp.tril(jnp.ones((CHUNK, CHUNK), BF16))
    n_par = params.shape[0]
    return pl.pallas_call(
        _rwkv_kernel, out_shape=jax.ShapeDtypeStruct((m, d), BF16),
        grid=(bsz, n_heads // hg, nc),
        in_specs=[pl.BlockSpec((3, CHUNK, w), lambda b, g, c: (0, b * nc + c, g)),
                  pl.BlockSpec((3, CHUNK, w), lambda b, g, c: (0, b * nc + c, g)),
                  pl.BlockSpec((n_par, w), lambda b, g, c: (0, g)),
                  pl.BlockSpec((grp, grp), lambda b, g, c: (0, 0)),
                  pl.BlockSpec((CHUNK, CHUNK), lambda b, g, c: (0, 0))],
        out_specs=pl.BlockSpec((CHUNK, w), lambda b, g, c: (b * nc + c, g)),
        scratch_shapes=[pltpu.VMEM((hg, RWKV_HEAD, RWKV_HEAD), F32)],
        compiler_params=_cp("parallel", "parallel", "arbitrary"), name="rwkv_scan")(
            rkv, lr, params, bd, tril)


def _cmul_add(acc_r, acc_i, cr, ci, xr, xi):
    return acc_r + cr * xr - ci * xi, acc_i + cr * xi + ci * xr


def _s5_kernel(h_ref, perm_ref, unperm_ref, wb_ref, wc_ref, coef_ref, pow_ref, d_ref, o_ref, x_ref, carry_ref):
    @pl.when(pl.program_id(2) == 0)
    def _():
        carry_ref[...] = jnp.zeros_like(carry_ref)

    tt = h_ref.shape[0]
    ns = carry_ref.shape[2]
    seg = tt // SUBLANES
    h = _dot(perm_ref[...], h_ref[...]).astype(BF16)
    x_ref[...] = _dot(h, wb_ref[0])
    a_r, a_i = coef_ref[0, 8], coef_ref[0, 9]

    def step_rows(k, c_r, c_i, x_r, x_i):
        rows = pl.ds(pl.multiple_of(k * SUBLANES, SUBLANES), SUBLANES)
        xr, xi = _cmul_add(x_ref[rows, 0:ns], x_ref[rows, ns:2 * ns], c_r, c_i, x_r, x_i)
        x_ref[rows, 0:ns] = xr
        x_ref[rows, ns:2 * ns] = xi
        return xr, xi

    zeros = jnp.zeros((SUBLANES, ns), F32)
    end_r, end_i = lax.fori_loop(0, seg, lambda k, c: step_rows(k, a_r, a_i, *c), (zeros, zeros))

    for s_idx, shift in enumerate((1, 2, 4)):
        end_r, end_i = _cmul_add(end_r, end_i, coef_ref[0, 2 * s_idx], coef_ref[0, 2 * s_idx + 1],
                                 pltpu.roll(end_r, shift, axis=0), pltpu.roll(end_i, shift, axis=0))
    in_r = jnp.broadcast_to(carry_ref[0], (SUBLANES, ns))
    in_i = jnp.broadcast_to(carry_ref[1], (SUBLANES, ns))
    end_r, end_i = _cmul_add(end_r, end_i, coef_ref[0, 6], coef_ref[0, 7], in_r, in_i)
    carry_ref[0] = end_r[SUBLANES - 1:SUBLANES, :]
    carry_ref[1] = end_i[SUBLANES - 1:SUBLANES, :]
    first = lax.broadcasted_iota(jnp.int32, (SUBLANES, ns), 0) == 0
    in_r = jnp.where(first, in_r, pltpu.roll(end_r, 1, axis=0))
    in_i = jnp.where(first, in_i, pltpu.roll(end_i, 1, axis=0))

    def correct(k, c):
        p_r = jnp.broadcast_to(pow_ref[0, 0, pl.ds(k, 1), :], (SUBLANES, ns))
        p_i = jnp.broadcast_to(pow_ref[0, 1, pl.ds(k, 1), :], (SUBLANES, ns))
        step_rows(k, p_r, p_i, in_r, in_i)
        return c

    lax.fori_loop(0, seg, correct, 0, unroll=2)

    y = _dot(x_ref[...].astype(BF16), wc_ref[0]) + d_ref[...] * h.astype(F32)
    y = (0.5 * y * (1.0 + lax.erf(y * (2.0 ** -0.5)))).astype(BF16)
    o_ref[...] = _dot(unperm_ref[...], y).astype(o_ref.dtype)


def _s5_scan(hn, lam_re, lam_im, log_dt, b_re, b_im, c_re, c_im, d_skip, bsz, seq, tt=1024):
    m, d = hn.shape
    n_groups, n_state, grp = b_re.shape
    sg = min(S5_SLAB_GROUPS, n_groups)
    slabs = n_groups // sg
    ch = sg * grp
    ns = sg * n_state
    tt = min(tt, seq)

    lr = jnp.minimum(lam_re.astype(F32), -1e-4)
    li = lam_im.astype(F32)
    dt = jnp.exp(log_dt.astype(F32))[:, None]
    mag = jnp.exp(lr * dt)
    ab_re = mag * jnp.cos(li * dt)
    ab_im = mag * jnp.sin(li * dt)
    den = lr * lr + li * li
    nr, ni = ab_re - 1.0, ab_im
    z_re = (nr * lr + ni * li) / den
    z_im = (ni * lr - nr * li) / den
    bb_re = z_re[..., None] * b_re - z_im[..., None] * b_im
    bb_im = z_re[..., None] * b_im + z_im[..., None] * b_re

    eye = jnp.eye(sg, dtype=F32)

    def block_diag_in(bb):
        t = bb.reshape(slabs, sg, n_state, grp)
        return jnp.einsum('sgph,gk->sghkp', t, eye).reshape(slabs, ch, ns)

    def block_diag_out(cc):
        t = cc.reshape(slabs, sg, grp, n_state)
        return jnp.einsum('sgqp,gk->sgpkq', t, eye).reshape(slabs, ns, ch)

    w_b = jnp.concatenate([block_diag_in(bb_re), block_diag_in(bb_im)], axis=2).astype(BF16)
    w_c = jnp.concatenate([block_diag_out(c_re.astype(F32)), -block_diag_out(c_im.astype(F32))],
                          axis=1).astype(BF16)

    def powers(base_re, base_im, n):
        re, im = base_re[None], base_im[None]
        while re.shape[0] < n:
            top_re, top_im = re[-1:], im[-1:]
            re, im = (jnp.concatenate([re, re * top_re - im * top_im]),
                      jnp.concatenate([im, re * top_im + im * top_re]))
        return re[:n].reshape(n, slabs, ns), im[:n].reshape(n, slabs, ns)

    seg = tt // SUBLANES
    pw_re, pw_im = powers(ab_re, ab_im, seg)
    sg_re, sg_im = powers(pw_re[-1], pw_im[-1], SUBLANES)
    row = jnp.arange(SUBLANES)
    coefs = []
    for shift in (1, 2, 4):
        mask = (row >= shift)[:, None, None]
        coefs += [jnp.where(mask, sg_re[shift - 1][None], 0.0), jnp.where(mask, sg_im[shift - 1][None], 0.0)]
    coefs += [sg_re, sg_im]
    coefs += [jnp.broadcast_to(pw_re[:1], sg_re.shape), jnp.broadcast_to(pw_im[:1], sg_im.shape)]
    coef = jnp.stack(coefs, axis=0).transpose(2, 0, 1, 3)
    pw = jnp.stack([pw_re, pw_im], axis=0).transpose(2, 0, 1, 3)
    r_idx = jnp.arange(tt)
    src_time = (r_idx % SUBLANES) * seg + r_idx // SUBLANES
    perm = (src_time[:, None] == jnp.arange(tt)[None, :]).astype(BF16)

    nt = seq // tt
    return pl.pallas_call(
        _s5_kernel, out_shape=jax.ShapeDtypeStruct((m, d), BF16),
        grid=(bsz, slabs, nt),
        in_specs=[pl.BlockSpec((tt, ch), lambda b, s, t: (b * nt + t, s)),
                  pl.BlockSpec((tt, tt), lambda b, s, t: (0, 0)),
                  pl.BlockSpec((tt, tt), lambda b, s, t: (0, 0)),
                  pl.BlockSpec((1, ch, 2 * ns), lambda b, s, t: (s, 0, 0)),
                  pl.BlockSpec((1, 2 * ns, ch), lambda b, s, t: (s, 0, 0)),
                  pl.BlockSpec((1, 10, SUBLANES, ns), lambda b, s, t: (s, 0, 0, 0)),
                  pl.BlockSpec((1, 2, seg, ns), lambda b, s, t: (s, 0, 0, 0)),
                  pl.BlockSpec((1, ch), lambda b, s, t: (0, s))],
        out_specs=pl.BlockSpec((tt, ch), lambda b, s, t: (b * nt + t, s)),
        scratch_shapes=[pltpu.VMEM((tt, 2 * ns), F32), pltpu.VMEM((2, 1, ns), F32)],
        compiler_params=_cp("parallel", "parallel", "arbitrary"), name="s5_scan")(
            hn, perm, perm.T, w_b, w_c, coef, pw, d_skip.reshape(1, d))


def _pad_to(x, axis, size):
    pad = size - x.shape[axis]
    if pad == 0:
        return x
    widths = [(0, 0)] * x.ndim
    widths[axis] = (0, pad)
    return jnp.pad(x, widths)


def _round_up(n, k):
    return (n + k - 1) // k * k


def _gla_layer(hn, g_post, w_in_all, layer, w_gk_up, b_gk_up, o_norm, w_out, bsz, seq):
    rank = w_gk_up.shape[0]
    n_main = w_in_all.shape[2] - rank
    rank_pad = _round_up(rank, LANES)
    w_in_t = jnp.swapaxes(w_in_all, 1, 2)
    proj = _matmul(hn, w_in_t, BF16, layer=layer, n_out=n_main, w_is_transposed=True)
    gk_lo = _matmul(hn, w_in_t, F32, layer=layer, col0=n_main, n_out=rank_pad, w_is_transposed=True)
    o = _gla_scan(proj, gk_lo, _pad_to(w_gk_up, 0, rank_pad), b_gk_up, o_norm, bsz, seq)
    return _matmul_norm(o, [w_out.astype(BF16)], g_post)


def _rwkv_layer(hn, g_post, mix, w_r, w_k, w_v, w_o, w0, w_w1, w_w2, a0, w_a1, w_a2,
                w_g1, w_g2, k_k, k_a, r_k, lnx_g, lnx_b, bsz, seq):
    d = hn.shape[1]
    rkv = _shiftmix_matmul(hn, jnp.stack([mix[0], mix[2], mix[3]]),
                           jnp.stack([w_r, w_k, w_v]).astype(BF16), BF16, seq)
    rank_pad = _round_up(max(w_w1.shape[1], w_a1.shape[1], w_g1.shape[1]), LANES)
    w1 = jnp.stack([_pad_to(w, 1, rank_pad) for w in (w_w1, w_a1, w_g1)]).astype(BF16)
    w2 = jnp.stack([_pad_to(w, 0, rank_pad) for w in (w_w2, w_a2, w_g2)]).astype(BF16)
    low = _shiftmix_matmul(hn, jnp.stack([mix[1], mix[4], mix[5]]), w1, F32, seq)
    bias = jnp.stack([w0, a0, jnp.zeros_like(w0)]).reshape(3, 1, d)
    lr = _lowrank_out(low, w2, bias)
    params = _pad_to(jnp.stack([k_k, k_a, r_k, lnx_g, lnx_b]), 0, SUBLANES)
    y = _rwkv_scan(rkv, lr, params, bsz, seq)
    return _matmul_norm(y, [w_o.astype(BF16)], g_post)


def _s5_layer(hn, g_post, lam_re, lam_im, log_dt, b_re, b_im, c_re, c_im, d_skip,
              w_glu1, w_glu2, bsz, seq):
    y = _s5_scan(hn, lam_re, lam_im, log_dt, b_re, b_im, c_re, c_im, d_skip, bsz, seq)
    return _matmul_norm(y, [w_glu1.astype(BF16), w_glu2.astype(BF16)], g_post)


def kernel(x, mem, norm_gains, mem_norm, mem_w_kv, xa_wq, xa_wo, mlp_w1, mlp_w2, gla_w_in, gla_w_gk_up, gla_b_gk_up, gla_o_norm, gla_w_out, rwkv_mix, rwkv_w_r, rwkv_w_k, rwkv_w_v, rwkv_w_o, rwkv_w0, rwkv_w_w1, rwkv_w_w2, rwkv_a0, rwkv_w_a1, rwkv_w_a2, rwkv_w_g1, rwkv_w_g2, rwkv_k_k, rwkv_k_a, rwkv_r_k, rwkv_lnx_g, rwkv_lnx_b, s5_lam_re, s5_lam_im, s5_log_dt, s5_b_re, s5_b_im, s5_c_re, s5_c_im, s5_d, s5_w_glu1, s5_w_glu2):
    bsz, seq, d = x.shape
    n_mem = mem.shape[1]
    depth = norm_gains.shape[0]
    assert seq % CHUNK == 0 and xa_wq.shape[2] == XA_HEADS * XA_HEAD_DIM

    mem_kv = _norm_matmul(mem.reshape(bsz * n_mem, d), mem_norm, mem_w_kv.astype(BF16), BF16)
    mem_kv = mem_kv.reshape(bsz, n_mem, mem_w_kv.shape[1])

    x = x.reshape(bsz * seq, d)
    hn = _norm(x, norm_gains[0, 0], BF16)
    for i in range(depth):
        kind, j = i % N_MIXERS, i // N_MIXERS
        g = norm_gains[i]
        if kind == 0:
            branch = _gla_layer(hn, g[1], gla_w_in, j, gla_w_gk_up[j], gla_b_gk_up[j],
                           gla_o_norm[j], gla_w_out[j], bsz, seq)
        elif kind == 1:
            branch = _rwkv_layer(hn, g[1], rwkv_mix[j], rwkv_w_r[j], rwkv_w_k[j], rwkv_w_v[j],
                            rwkv_w_o[j], rwkv_w0[j], rwkv_w_w1[j], rwkv_w_w2[j], rwkv_a0[j],
                            rwkv_w_a1[j], rwkv_w_a2[j], rwkv_w_g1[j], rwkv_w_g2[j], rwkv_k_k[j],
                            rwkv_k_a[j], rwkv_r_k[j], rwkv_lnx_g[j], rwkv_lnx_b[j], bsz, seq)
        else:
            branch = _s5_layer(hn, g[1], s5_lam_re[j], s5_lam_im[j], s5_log_dt[j], s5_b_re[j],
                          s5_b_im[j], s5_c_re[j], s5_c_im[j], s5_d[j], s5_w_glu1[j], s5_w_glu2[j],
                          bsz, seq)
        x, hn = _cross_attention(x, branch, g[2], xa_wq[i].astype(BF16), mem_kv, xa_wo[i].astype(BF16), g[3],
                                 g[4], seq)
        hidden = _matmul(hn, mlp_w1, BF16, act="relu2", layer=i)
        branch = _matmul_acc(hidden, mlp_w2, layer=i)
        x, hn = _add_norm(x, branch, g[5], norm_gains[i + 1, 0] if i + 1 < depth else None)
    return x.reshape(bsz, seq, d)
```

```python
import functools

import jax
import jax.numpy as jnp
from jax import lax
from jax.experimental import pallas as pl
from jax.experimental.pallas import tpu as pltpu

F32 = jnp.float32
BF16 = jnp.bfloat16

NORM_EPS = 1e-6
CHUNK = 64
N_MIXERS = 3
GLA_HEADS = 4
GLA_GATE_NORMALIZER = 16.0
RWKV_HEAD = 64
RWKV_LNX_EPS = 64e-5
XA_HEADS = 4
XA_HEAD_DIM = 128
S5_SLAB_GROUPS = 16
SUBLANES = 8
LANES = 128
VMEM_LIMIT = 60 * 1024 * 1024


def _cp(*sem):
    return pltpu.CompilerParams(dimension_semantics=sem, vmem_limit_bytes=VMEM_LIMIT)


def _rms(x, gain, eps=NORM_EPS):
    ms = jnp.mean(x * x, axis=-1, keepdims=True)
    return x * lax.rsqrt(ms + eps) * gain


def _stat_scratch(rows):
    return pltpu.VMEM((rows, LANES), F32)


def _for_row_blocks(n_rows, body, carry=None, rows=2 * SUBLANES, unroll=1):
    def step(i, c):
        return body(pl.ds(pl.multiple_of(i * rows, rows), rows), c)
    return lax.fori_loop(0, n_rows // rows, step, carry, unroll=unroll)


def _sumsq_lanes(x):
    acc = x[:, 0:LANES] * x[:, 0:LANES]
    for j in range(1, x.shape[1] // LANES):
        blk = x[:, j * LANES:(j + 1) * LANES]
        acc = acc + blk * blk
    return acc


def _finish_scales(s_ref, d):
    ms = jnp.sum(s_ref[...], axis=-1, keepdims=True) * (1.0 / d)
    s_ref[...] = jnp.broadcast_to(lax.rsqrt(ms + NORM_EPS), s_ref.shape)


def _rows_of(ref):
    return lambda rs: ref[rs, :]


def _row_scales_into(s_ref, load, d):
    def body(rs, c):
        s_ref[rs, :] = _sumsq_lanes(load(rs))
        return c
    _for_row_blocks(s_ref.shape[0], body, rows=SUBLANES, unroll=2)
    _finish_scales(s_ref, d)


GAIN_ROWS = 2 * SUBLANES


def _row_vector(v):
    return jnp.broadcast_to(v.reshape(1, -1), (GAIN_ROWS, v.shape[-1]))


def _vector_spec(d, index_map):
    return pl.BlockSpec((GAIN_ROWS, d), index_map)


def _scaled(x, scale, g_ref):
    return x * jnp.tile(scale, (1, x.shape[1] // LANES)) * g_ref[0:x.shape[0], :]


def _norm_rows_into(dst_ref, load, g_ref, s_ref):
    _row_scales_into(s_ref, load, dst_ref.shape[1])

    def body(rs, c):
        dst_ref[rs, :] = _scaled(load(rs), s_ref[rs, :], g_ref).astype(dst_ref.dtype)
        return c
    _for_row_blocks(dst_ref.shape[0], body, unroll=2)


def _add_norm_rows(o_ref, load_m, load_res, g_ref, s_ref, hn_ref=None, g_next_ref=None):
    _row_scales_into(s_ref, load_m, o_ref.shape[1])

    def body(rs, c):
        x_new = load_res(rs) + _scaled(load_m(rs), s_ref[rs, :], g_ref)
        o_ref[rs, :] = x_new
        if hn_ref is not None:
            s_ref[rs, :] = _sumsq_lanes(x_new)
        return c
    _for_row_blocks(o_ref.shape[0], body, rows=SUBLANES, unroll=2)
    if hn_ref is not None:
        _finish_scales(s_ref, o_ref.shape[1])

        def body2(rs, c):
            hn_ref[rs, :] = _scaled(o_ref[rs, :], s_ref[rs, :], g_next_ref).astype(hn_ref.dtype)
            return c
        _for_row_blocks(o_ref.shape[0], body2, unroll=2)


def _dot(a, b):
    return jnp.dot(a, b, preferred_element_type=F32)


def _dot_nt(a, b):
    return lax.dot_general(a, b, (((1,), (1,)), ((), ())), preferred_element_type=F32)


def _dot_tn(a, b):
    return lax.dot_general(a, b, (((0,), (0,)), ((), ())), preferred_element_type=F32)


def _split_bf16(x, pieces):
    out = []
    for _ in range(pieces - 1):
        out.append(x.astype(BF16))
        x = x - out[-1].astype(F32)
    return out + [x.astype(BF16)]


def _dot_ones(ones, x):
    hi, mid, lo = _split_bf16(x, 3)
    return _dot(ones, hi) + (_dot(ones, mid) + _dot(ones, lo))


def _dot_split(a, b):
    a_hi, a_lo = _split_bf16(a, 2)
    b_hi, b_lo = _split_bf16(b, 2)
    return _dot(a_hi, b_hi) + (_dot(a_hi, b_lo) + _dot(a_lo, b_hi))


def _sigmoid(x):
    return 1.0 / (1.0 + jnp.exp(-x))


def _softplus(x):
    return jnp.maximum(x, 0.0) + jnp.log1p(jnp.exp(-jnp.abs(x)))


def _norm_kernel(x_ref, g_ref, o_ref, s_ref):
    _norm_rows_into(o_ref, _rows_of(x_ref), g_ref, s_ref)


def _norm(x, gain, out_dtype=F32, tm=256):
    m, d = x.shape
    tm = min(tm, m)
    return pl.pallas_call(
        _norm_kernel, out_shape=jax.ShapeDtypeStruct((m, d), out_dtype),
        grid=(m // tm,),
        in_specs=[pl.BlockSpec((tm, d), lambda i: (i, 0)), _vector_spec(d, lambda i: (0, 0))],
        out_specs=pl.BlockSpec((tm, d), lambda i: (i, 0)),
        scratch_shapes=[_stat_scratch(tm)],
        compiler_params=_cp("parallel"), name="rmsnorm")(x, _row_vector(gain))


def _norm_matmul_kernel(x_ref, g_ref, w_ref, o_ref, hn_ref, s_ref):
    @pl.when(pl.program_id(1) == 0)
    def _():
        _norm_rows_into(hn_ref, _rows_of(x_ref), g_ref, s_ref)

    o_ref[...] = _dot(hn_ref[...], w_ref[...]).astype(o_ref.dtype)


def _norm_matmul(x, gain, w, out_dtype, tm=512, tn=512):
    m, d = x.shape
    n = w.shape[1]
    tm, tn = min(tm, m), min(tn, n)
    return pl.pallas_call(
        _norm_matmul_kernel, out_shape=jax.ShapeDtypeStruct((m, n), out_dtype),
        grid=(m // tm, n // tn),
        in_specs=[pl.BlockSpec((tm, d), lambda i, j: (i, 0)),
                  _vector_spec(d, lambda i, j: (0, 0)),
                  pl.BlockSpec((d, tn), lambda i, j: (0, j))],
        out_specs=pl.BlockSpec((tm, tn), lambda i, j: (i, j)),
        scratch_shapes=[pltpu.VMEM((tm, d), BF16), _stat_scratch(tm)],
        compiler_params=_cp("parallel", "arbitrary"), name="norm_matmul")(x, _row_vector(gain), w)


def _matmul_norm_kernel(*refs, glu, tn):
    if glu:
        a_ref, w_ref, w2_ref, g_ref, o_ref, acc_ref, s_ref = refs
    else:
        a_ref, w_ref, g_ref, o_ref, acc_ref, s_ref = refs
    j = pl.program_id(1)
    a = a_ref[...]
    y = _dot(a, w_ref[...])
    if glu:
        y = y * _sigmoid(_dot(a, w2_ref[...]))
    acc_ref[:, pl.ds(pl.multiple_of(j * tn, tn), tn)] = y

    @pl.when(j == pl.num_programs(1) - 1)
    def _():
        _norm_rows_into(o_ref, _rows_of(acc_ref), g_ref, s_ref)


def _matmul_norm(a, ws, gain, tm=1024):
    m, k = a.shape
    n = ws[0].shape[1]
    glu = len(ws) == 2
    tm, tn = min(tm, m), min(256 if glu else 512, n)
    w_specs = [pl.BlockSpec((k, tn), lambda i, j: (0, j)) for _ in ws]
    return pl.pallas_call(
        functools.partial(_matmul_norm_kernel, glu=glu, tn=tn),
        out_shape=jax.ShapeDtypeStruct((m, n), BF16),
        grid=(m // tm, n // tn),
        in_specs=[pl.BlockSpec((tm, k), lambda i, j: (i, 0))] + w_specs + [
            _vector_spec(n, lambda i, j: (0, 0))],
        out_specs=pl.BlockSpec((tm, n), lambda i, j: (i, 0)),
        scratch_shapes=[pltpu.VMEM((tm, n), F32), _stat_scratch(tm)],
        compiler_params=_cp("parallel", "arbitrary"),
        name="glu_norm" if glu else "matmul_norm")(a, *ws, _row_vector(gain))


def _xa_kernel(x_ref, br_ref, g_in_ref, wq_ref, k_ref, v_ref, wo_ref, g_out_ref, g_next_ref, o_ref, hn_out_ref,
               hn_ref, x1_ref, s_ref):
    def add_branch(rs):
        x1_ref[rs, :] = x_ref[rs, :] + br_ref[rs, :].astype(F32)
        return x1_ref[rs, :]

    _row_scales_into(s_ref, add_branch, x_ref.shape[1])
    x1 = _rows_of(x1_ref)

    def normed(rs, c):
        hn_ref[rs, :] = _scaled(x1(rs), s_ref[rs, :], g_in_ref).astype(hn_ref.dtype)
        return c
    _for_row_blocks(hn_ref.shape[0], normed, unroll=2)
    q = _dot(hn_ref[...], wq_ref[...]) * (XA_HEAD_DIM ** -0.5)
    heads = []
    for h in range(XA_HEADS):
        sl = slice(h * XA_HEAD_DIM, (h + 1) * XA_HEAD_DIM)
        s = _dot_nt(q[:, sl].astype(BF16), k_ref[0, :, sl])
        s = s - jnp.max(s, axis=-1, keepdims=True)
        p = jnp.exp(s)
        p = p / jnp.sum(p, axis=-1, keepdims=True)
        heads.append(_dot(p.astype(BF16), v_ref[0, :, sl]))
    o = jnp.concatenate(heads, axis=-1).astype(BF16)
    o_ref[...] = _dot(o, wo_ref[...])
    _add_norm_rows(o_ref, _rows_of(o_ref), x1, g_out_ref, s_ref, hn_out_ref, g_next_ref)


def _cross_attention(x, branch, g_in, wq, mem_kv, wo, g_out, g_next, seq, tm=256):
    m, d = x.shape
    xw = wq.shape[1]
    n_mem = mem_kv.shape[1]
    tm = min(tm, seq)
    tiles_per_seq = seq // tm
    return pl.pallas_call(
        _xa_kernel, out_shape=[jax.ShapeDtypeStruct((m, d), F32), jax.ShapeDtypeStruct((m, d), BF16)],
        grid=(m // tm,),
        in_specs=[pl.BlockSpec((tm, d), lambda i: (i, 0)),
                  pl.BlockSpec((tm, d), lambda i: (i, 0)),
                  _vector_spec(d, lambda i: (0, 0)),
                  pl.BlockSpec((d, xw), lambda i: (0, 0)),
                  pl.BlockSpec((1, n_mem, xw), lambda i: (i // tiles_per_seq, 0, 0)),
                  pl.BlockSpec((1, n_mem, xw), lambda i: (i // tiles_per_seq, 0, 1)),
                  pl.BlockSpec((xw, d), lambda i: (0, 0)),
                  _vector_spec(d, lambda i: (0, 0)),
                  _vector_spec(d, lambda i: (0, 0))],
        out_specs=[pl.BlockSpec((tm, d), lambda i: (i, 0)), pl.BlockSpec((tm, d), lambda i: (i, 0))],
        scratch_shapes=[pltpu.VMEM((tm, d), BF16), pltpu.VMEM((tm, d), F32), _stat_scratch(tm)],
        compiler_params=_cp("parallel"), name="cross_attention")(
            x, branch, _row_vector(g_in), wq, mem_kv, mem_kv, wo, _row_vector(g_out), _row_vector(g_next))


def _matmul_kernel(a_ref, w_ref, o_ref, *, act, col0, w_cols, w_is_transposed):
    w = w_ref[...]
    if w_cols is not None:
        tn = o_ref.shape[1]
        shape, axis = ((tn, 1), 0) if w_is_transposed else ((1, tn), 1)
        cols = col0 + pl.program_id(1) * tn + lax.broadcasted_iota(jnp.int32, shape, axis)
        w = jnp.where(cols < w_cols, w, 0.0)
    w = w.astype(BF16)
    y = _dot_nt(a_ref[...], w) if w_is_transposed else _dot(a_ref[...], w)
    if act == "relu2":
        y = jnp.maximum(y, 0.0)
        y = y * y
    o_ref[...] = y.astype(o_ref.dtype)


def _layer_weight_spec(w, layer, block, index_map):
    if w.ndim == 2:
        return pl.BlockSpec(block, index_map)
    return pl.BlockSpec((None,) + block, lambda *idx: (layer,) + index_map(*idx))


def _matmul(a, w, out_dtype, act=None, layer=None, col0=0, n_out=None, w_is_transposed=False, tm=2048, tn=512):
    m, k = a.shape
    w_cols = w.shape[-2] if w_is_transposed else w.shape[-1]
    n = w_cols if n_out is None else n_out
    tm, tn = min(tm, m), min(tn, n)
    assert col0 % tn == 0
    ragged = col0 + n > w_cols
    block, index_map = (((tn, k), lambda i, j: (col0 // tn + j, 0)) if w_is_transposed
                        else ((k, tn), lambda i, j: (0, col0 // tn + j)))
    return pl.pallas_call(
        functools.partial(_matmul_kernel, act=act, col0=col0, w_cols=w_cols if ragged else None,
                          w_is_transposed=w_is_transposed),
        out_shape=jax.ShapeDtypeStruct((m, n), out_dtype),
        grid=(m // tm, n // tn),
        in_specs=[pl.BlockSpec((tm, k), lambda i, j: (i, 0), pipeline_mode=pl.Buffered(1)),
                  _layer_weight_spec(w, layer, block, index_map)],
        out_specs=pl.BlockSpec((tm, tn), lambda i, j: (i, j)),
        compiler_params=_cp("parallel", "arbitrary"), name="matmul")(a, w)


def _matmul_acc_kernel(a_ref, w_ref, o_ref, *, ts):
    @pl.when(pl.program_id(2) == 0)
    def _():
        o_ref[...] = jnp.zeros_like(o_ref)

    a = a_ref[...]
    for n in range(o_ref.shape[1] // ts):
        sl = slice(n * ts, (n + 1) * ts)
        o_ref[:, sl] += _dot(a, w_ref[:, sl].astype(BF16))


def _matmul_acc(a, w, layer=None, tm=2048, tn=1024, tk=2048):
    m, k = a.shape
    n = w.shape[-1]
    tm, tn, tk = min(tm, m), min(tn, n), min(tk, k)
    return pl.pallas_call(
        functools.partial(_matmul_acc_kernel, ts=min(256, tn)),
        out_shape=jax.ShapeDtypeStruct((m, n), F32),
        grid=(m // tm, n // tn, k // tk),
        in_specs=[pl.BlockSpec((tm, tk), lambda i, j, kk: (i, kk)),
                  _layer_weight_spec(w, layer, (tk, tn), lambda i, j, kk: (kk, j))],
        out_specs=pl.BlockSpec((tm, tn), lambda i, j, kk: (i, j)),
        compiler_params=_cp("parallel", "parallel", "arbitrary"), name="matmul_acc")(a, w)


def _add_norm_kernel(x_ref, m_ref, gp_ref, gn_ref, o_ref, *rest):
    s_ref = rest[-1]
    hn_ref = rest[0] if len(rest) == 2 else None
    _add_norm_rows(o_ref, _rows_of(m_ref), _rows_of(x_ref), gp_ref, s_ref, hn_ref, gn_ref)


def _add_norm(x, m_branch, g_post, g_next, tm=256):
    m, d = x.shape
    tm = min(tm, m)
    row = pl.BlockSpec((tm, d), lambda i: (i, 0))
    vec = _vector_spec(d, lambda i: (0, 0))
    emit_hn = g_next is not None
    out_shape = [jax.ShapeDtypeStruct((m, d), F32)] + ([jax.ShapeDtypeStruct((m, d), BF16)] if emit_hn else [])
    outs = pl.pallas_call(
        _add_norm_kernel, out_shape=out_shape, grid=(m // tm,),
        in_specs=[row, row, vec, vec], out_specs=[row] * len(out_shape),
        scratch_shapes=[_stat_scratch(tm)],
        compiler_params=_cp("parallel"), name="add_norm")(
            x, m_branch, _row_vector(g_post), _row_vector(g_next if emit_hn else g_post))
    return (outs[0], outs[1]) if emit_hn else (outs[0], None)


def _gla_kernel(q_ref, k_ref, v_ref, g_ref, gk_ref, wup_ref, bup_ref, onorm_ref, tril_ref,
                o_ref, st_ref):
    @pl.when(pl.program_id(1) == 0)
    def _():
        st_ref[...] = jnp.zeros_like(st_ref)

    nb, _, dk = q_ref.shape
    bs = range(nb)
    gk = gk_ref[...].reshape(nb * CHUNK, gk_ref.shape[2])
    z = _dot_split(gk, wup_ref[...]) + bup_ref[...]
    log_alpha = (jnp.minimum(z, 0.0) - jnp.log1p(jnp.exp(-jnp.abs(z)))) / GLA_GATE_NORMALIZER
    cum_all = _dot_ones(tril_ref[...], log_alpha)
    cum = [cum_all[b * CHUNK:(b + 1) * CHUNK] for b in bs]
    cum_last = [cum[b][CHUNK - 1:CHUNK, :] for b in bs]
    k_dec = [(k_ref[b].astype(F32) * jnp.exp(cum_last[b] - cum[b])).astype(BF16) for b in bs]
    st = [st_ref[b] * jnp.exp(cum_last[b]) + _dot_tn(v_ref[b], k_dec[b]) for b in bs]
    for b in bs:
        st_ref[b] = st[b]
    q = [(q_ref[b].astype(F32) * dk ** -0.5).astype(BF16) for b in bs]
    o = [_dot_nt(q[b], st[b].astype(BF16)) for b in bs]
    for b in bs:
        g = g_ref[b].astype(F32)
        o_ref[b] = (_rms(o[b], onorm_ref[...]) * (g * _sigmoid(g))).astype(o_ref.dtype)


def _gla_scan(proj, gk_lo, w_up, b_up, o_norm, bsz, seq):
    dk_all = w_up.shape[1]
    dkh = dk_all // GLA_HEADS
    dv_all = (proj.shape[1] - 2 * dk_all) // 2
    dvh = dv_all // GLA_HEADS
    nc = seq // CHUNK
    rank_pad = gk_lo.shape[1]
    proj = proj.reshape(bsz, seq, proj.shape[1])
    gk_lo = gk_lo.reshape(bsz, seq, rank_pad)
    tril = jnp.kron(jnp.eye(bsz, dtype=F32), jnp.tril(jnp.ones((CHUNK, CHUNK), F32))).astype(BF16)
    k_off = dk_all // dkh
    v_off = 2 * dk_all // dvh
    g_off = v_off + dv_all // dvh
    out = pl.pallas_call(
        _gla_kernel, out_shape=jax.ShapeDtypeStruct((bsz, seq, dv_all), BF16),
        grid=(GLA_HEADS, nc),
        in_specs=[pl.BlockSpec((bsz, CHUNK, dkh), lambda h, c: (0, c, h)),
                  pl.BlockSpec((bsz, CHUNK, dkh), lambda h, c: (0, c, k_off + h)),
                  pl.BlockSpec((bsz, CHUNK, dvh), lambda h, c: (0, c, v_off + h)),
                  pl.BlockSpec((bsz, CHUNK, dvh), lambda h, c: (0, c, g_off + h)),
                  pl.BlockSpec((bsz, CHUNK, rank_pad), lambda h, c: (0, c, 0)),
                  pl.BlockSpec((rank_pad, dkh), lambda h, c: (0, h)),
                  pl.BlockSpec((1, dkh), lambda h, c: (0, h)),
                  pl.BlockSpec((1, dvh), lambda h, c: (0, 0)),
                  pl.BlockSpec((bsz * CHUNK, bsz * CHUNK), lambda h, c: (0, 0))],
        out_specs=pl.BlockSpec((bsz, CHUNK, dvh), lambda h, c: (0, c, h)),
        scratch_shapes=[pltpu.VMEM((bsz, dvh, dkh), F32)],
        compiler_params=_cp("parallel", "arbitrary"), name="gla_scan")(
            proj, proj, proj, proj, gk_lo, w_up, b_up.reshape(1, dk_all), o_norm.reshape(1, dvh), tril)
    return out.reshape(bsz * seq, dv_all)


def _shiftmix_matmul_kernel(hn_ref, prev_ref, mix_ref, w_ref, o_ref, xm_ref, *, tiles_per_seq):
    @pl.when(pl.program_id(2) == 0)
    def _():
        prev_rows = prev_ref.shape[0]
        last = prev_ref[...].astype(F32)[prev_rows - 1:prev_rows, :]
        first_tile = pl.program_id(0) % tiles_per_seq == 0
        last = jnp.where(first_tile, 0.0, last)

        def body(rs, last):
            hn = hn_ref[rs, :].astype(F32)
            rows = lax.broadcasted_iota(jnp.int32, hn.shape, 0)
            shifted = jnp.where(rows == 0, last, pltpu.roll(hn, 1, axis=0))
            xm_ref[rs, :] = (hn + (shifted - hn) * mix_ref[0]).astype(BF16)
            return hn[hn.shape[0] - 1:, :]

        _for_row_blocks(hn_ref.shape[0], body, last)

    o_ref[0] = _dot(xm_ref[...], w_ref[0]).astype(o_ref.dtype)


def _shiftmix_matmul(hn, mix, w, out_dtype, seq, tm=2048, tn=512):
    m, d = hn.shape
    p_cnt, _, n = w.shape
    tm, tn = min(tm, seq), min(tn, n)
    prev_rows = 2 * SUBLANES
    blk = tm // prev_rows
    return pl.pallas_call(
        functools.partial(_shiftmix_matmul_kernel, tiles_per_seq=seq // tm),
        out_shape=jax.ShapeDtypeStruct((p_cnt, m, n), out_dtype),
        grid=(m // tm, p_cnt, n // tn),
        in_specs=[pl.BlockSpec((tm, d), lambda i, p, j: (i, 0), pipeline_mode=pl.Buffered(1)),
                  pl.BlockSpec((prev_rows, d), lambda i, p, j: (jnp.maximum(i * blk - 1, 0), 0)),
                  pl.BlockSpec((1, GAIN_ROWS, d), lambda i, p, j: (p, 0, 0)),
                  pl.BlockSpec((1, d, tn), lambda i, p, j: (p, 0, j))],
        out_specs=pl.BlockSpec((1, tm, tn), lambda i, p, j: (p, i, j)),
        scratch_shapes=[pltpu.VMEM((tm, d), BF16)],
        compiler_params=_cp("parallel", "arbitrary", "arbitrary"), name="shiftmix_matmul")(
            hn, hn, jnp.broadcast_to(mix.reshape(p_cnt, 1, d), (p_cnt, GAIN_ROWS, d)), w)


def _lowrank_out_kernel(h_ref, w_ref, b_ref, o_ref):
    p = pl.program_id(0)
    h = h_ref[0]
    act = jnp.where(p == 0, jnp.tanh(h), jnp.where(p == 1, h, _sigmoid(h)))
    o_ref[0] = _dot(act.astype(BF16), w_ref[0]) + b_ref[0]


def _lowrank_out(h, w2, bias, tm=512):
    p_cnt, m, r = h.shape
    d = w2.shape[2]
    tm = min(tm, m)
    return pl.pallas_call(
        _lowrank_out_kernel, out_shape=jax.ShapeDtypeStruct((p_cnt, m, d), F32),
        grid=(p_cnt, m // tm),
        in_specs=[pl.BlockSpec((1, tm, r), lambda p, i: (p, i, 0)),
                  pl.BlockSpec((1, r, d), lambda p, i: (p, 0, 0)),
                  pl.BlockSpec((1, 1, d), lambda p, i: (p, 0, 0))],
        out_specs=pl.BlockSpec((1, tm, d), lambda p, i: (p, i, 0)),
        compiler_params=_cp("arbitrary", "arbitrary"), name="lowrank_out")(h, w2, bias)


def _rwkv_kernel(rkv_ref, lr_ref, par_ref, bd_ref, tril_ref, o_ref, st_ref):
    @pl.when(pl.program_id(2) == 0)
    def _():
        st_ref[...] = jnp.zeros_like(st_ref)

    n = RWKV_HEAD
    heads = st_ref.shape[0]
    r = rkv_ref[0].astype(F32)
    k = rkv_ref[1].astype(F32)
    v = rkv_ref[2].astype(F32)
    w_log = -_softplus(-lr_ref[0]) - 0.5
    log_w = -jnp.exp(w_log)
    a = _sigmoid(lr_ref[1])
    gate = lr_ref[2]
    k_k, k_a, r_k = par_ref[0:1, :], par_ref[1:2, :], par_ref[2:3, :]
    lnx_g, lnx_b = par_ref[3:4, :], par_ref[4:5, :]
    bd = bd_ref[...]
    grp = bd.shape[0]

    def head_sums(x):
        x = x.astype(BF16)
        return jnp.concatenate([_dot(x[:, i:i + grp], bd) for i in range(0, x.shape[1], grp)], axis=1)

    kk = k * k_k
    kk = kk / jnp.maximum(jnp.sqrt(head_sums(kk * kk)), 1e-12)
    k2 = k * (1.0 + (a - 1.0) * k_a)
    cw = _dot_ones(tril_ref[...], log_w)
    cw_last = cw[CHUNK - 1:CHUNK, :]
    e_neg = jnp.exp(-cw)
    e_end = jnp.exp(cw_last - cw)
    a_t = (-kk * jnp.exp(cw - log_w)).astype(BF16)
    r_t = (r * jnp.exp(cw)).astype(BF16)
    b_vec = kk * a
    b_t = (b_vec * e_neg).astype(BF16)
    k_t = (k2 * e_neg).astype(BF16)
    b_w = (b_vec * e_end).astype(BF16)
    k_w = (k2 * e_end).astype(BF16)
    w_end = jnp.exp(cw_last)
    v_b = v.astype(BF16)

    rows = lax.broadcasted_iota(jnp.int32, (CHUNK, 2 * CHUNK), 0)
    cols = lax.broadcasted_iota(jnp.int32, (CHUNK, 2 * CHUNK), 1) % CHUNK
    strict = rows > cols
    incl = rows >= cols
    hs = range(heads)
    sls = [slice(h * n, (h + 1) * n) for h in hs]
    ar = [jnp.concatenate([a_t[:, sl], r_t[:, sl]], axis=0) for sl in sls]
    bk = [jnp.concatenate([b_t[:, sl], k_t[:, sl]], axis=0) for sl in sls]
    s0 = [st_ref[h] for h in hs]
    gram = [_dot_nt(ar[h], bk[h]) for h in hs]
    proj = [_dot_nt(ar[h], s0[h].astype(BF16)) for h in hs]
    v_h = [v_b[:, sl] for sl in sls]
    low = [jnp.where(strict, gram[h][:CHUNK], 0.0).astype(BF16) for h in hs]
    t_row = rows[:, :CHUNK]
    t_col = lax.broadcasted_iota(jnp.int32, (CHUNK, CHUNK), 1)
    eye = (t_row == t_col).astype(F32)

    def lower_left(b):
        return (t_row // (2 * b) == t_col // (2 * b)) & (t_row % (2 * b) >= b) & (t_col % (2 * b) < b)

    nil = [low[h][:, :CHUNK] for h in hs]
    t_inv = [eye + jnp.where(lower_left(1), nil[h], 0).astype(F32) for h in hs]
    b = 2
    while b < CHUNK:
        mask = lower_left(b)
        off = [jnp.where(mask, nil[h], 0) for h in hs]
        t_b = [t_inv[h].astype(BF16) for h in hs]
        right = [_dot(off[h], t_b[h]).astype(BF16) for h in hs]
        t_inv = [t_inv[h] + _dot(t_b[h], right[h]) for h in hs]
        b *= 2
    rhs = [proj[h][:CHUNK] + _dot(low[h][:, CHUNK:], v_h[h]) for h in hs]
    u = [_dot(t_inv[h].astype(BF16), rhs[h].astype(BF16)) for h in hs]
    uv = [jnp.concatenate([u[h].astype(BF16), v_h[h]], axis=0) for h in hs]
    upper = [jnp.where(incl, gram[h][CHUNK:], 0.0).astype(BF16) for h in hs]
    y_heads = [proj[h][CHUNK:] + _dot(upper[h], uv[h]) for h in hs]
    for h in hs:
        bkw = jnp.concatenate([b_w[:, sls[h]], k_w[:, sls[h]]], axis=0)
        st_ref[h] = s0[h] * w_end[:, sls[h]] + _dot_tn(uv[h], bkw)

    y = jnp.concatenate(y_heads, axis=1)
    inv_n = 1.0 / n
    mu = head_sums(y) * inv_n
    yc = y - mu
    var = head_sums(yc * yc) * inv_n
    y = yc * lax.rsqrt(var + RWKV_LNX_EPS) * lnx_g + lnx_b
    bonus = head_sums(r * k2 * r_k) * v
    o_ref[...] = ((y + bonus) * gate).astype(o_ref.dtype)


def _rwkv_scan(rkv, lr, params, bsz, seq, heads_per_step=32):
    _, m, d = rkv.shape
    n_heads = d // RWKV_HEAD
    hg = min(heads_per_step, n_heads)
    w = hg * RWKV_HEAD
    nc = seq // CHUNK
    grp = min(2 * LANES, w)
    lane_head = jnp.arange(grp) // RWKV_HEAD
    bd = (lane_head[:, None] == lane_head[None, :]).astype(BF16)
    tril = jnp.tril(jnp.ones((CHUNK, CHUNK), BF16))
    n_par = params.shape[0]
    return pl.pallas_call(
        _rwkv_kernel, out_shape=jax.ShapeDtypeStruct((m, d), BF16),
        grid=(bsz, n_heads // hg, nc),
        in_specs=[pl.BlockSpec((3, CHUNK, w), lambda b, g, c: (0, b * nc + c, g)),
                  pl.BlockSpec((3, CHUNK, w), lambda b, g, c: (0, b * nc + c, g)),
                  pl.BlockSpec((n_par, w), lambda b, g, c: (0, g)),
                  pl.BlockSpec((grp, grp), lambda b, g, c: (0, 0)),
                  pl.BlockSpec((CHUNK, CHUNK), lambda b, g, c: (0, 0))],
        out_specs=pl.BlockSpec((CHUNK, w), lambda b, g, c: (b * nc + c, g)),
        scratch_shapes=[pltpu.VMEM((hg, RWKV_HEAD, RWKV_HEAD), F32)],
        compiler_params=_cp("parallel", "parallel", "arbitrary"), name="rwkv_scan")(
            rkv, lr, params, bd, tril)


def _cmul_add(acc_r, acc_i, cr, ci, xr, xi):
    return acc_r + cr * xr - ci * xi, acc_i + cr * xi + ci * xr


def _s5_kernel(h_ref, perm_ref, unperm_ref, wb_ref, wc_ref, coef_ref, pow_ref, d_ref, o_ref, x_ref, carry_ref):
    @pl.when(pl.program_id(2) == 0)
    def _():
        carry_ref[...] = jnp.zeros_like(carry_ref)

    tt = h_ref.shape[0]
    ns = carry_ref.shape[2]
    seg = tt // SUBLANES
    h = _dot(perm_ref[...], h_ref[...]).astype(BF16)
    x_ref[...] = _dot(h, wb_ref[0])
    a_r, a_i = coef_ref[0, 8], coef_ref[0, 9]

    def step_rows(k, c_r, c_i, x_r, x_i):
        rows = pl.ds(pl.multiple_of(k * SUBLANES, SUBLANES), SUBLANES)
        xr, xi = _cmul_add(x_ref[rows, 0:ns], x_ref[rows, ns:2 * ns], c_r, c_i, x_r, x_i)
        x_ref[rows, 0:ns] = xr
        x_ref[rows, ns:2 * ns] = xi
        return xr, xi

    zeros = jnp.zeros((SUBLANES, ns), F32)
    end_r, end_i = lax.fori_loop(0, seg, lambda k, c: step_rows(k, a_r, a_i, *c), (zeros, zeros))

    for s_idx, shift in enumerate((1, 2, 4)):
        end_r, end_i = _cmul_add(end_r, end_i, coef_ref[0, 2 * s_idx], coef_ref[0, 2 * s_idx + 1],
                                 pltpu.roll(end_r, shift, axis=0), pltpu.roll(end_i, shift, axis=0))
    in_r = jnp.broadcast_to(carry_ref[0], (SUBLANES, ns))
    in_i = jnp.broadcast_to(carry_ref[1], (SUBLANES, ns))
    end_r, end_i = _cmul_add(end_r, end_i, coef_ref[0, 6], coef_ref[0, 7], in_r, in_i)
    carry_ref[0] = end_r[SUBLANES - 1:SUBLANES, :]
    carry_ref[1] = end_i[SUBLANES - 1:SUBLANES, :]
    first = lax.broadcasted_iota(jnp.int32, (SUBLANES, ns), 0) == 0
    in_r = jnp.where(first, in_r, pltpu.roll(end_r, 1, axis=0))
    in_i = jnp.where(first, in_i, pltpu.roll(end_i, 1, axis=0))

    def correct(k, c):
        p_r = jnp.broadcast_to(pow_ref[0, 0, pl.ds(k, 1), :], (SUBLANES, ns))
        p_i = jnp.broadcast_to(pow_ref[0, 1, pl.ds(k, 1), :], (SUBLANES, ns))
        step_rows(k, p_r, p_i, in_r, in_i)
        return c

    lax.fori_loop(0, seg, correct, 0, unroll=2)

    y = _dot(x_ref[...].astype(BF16), wc_ref[0]) + d_ref[...] * h.astype(F32)
    y = (0.5 * y * (1.0 + lax.erf(y * (2.0 ** -0.5)))).astype(BF16)
    o_ref[...] = _dot(unperm_ref[...], y).astype(o_ref.dtype)


def _s5_scan(hn, lam_re, lam_im, log_dt, b_re, b_im, c_re, c_im, d_skip, bsz, seq, tt=1024):
    m, d = hn.shape
    n_groups, n_state, grp = b_re.shape
    sg = min(S5_SLAB_GROUPS, n_groups)
    slabs = n_groups // sg
    ch = sg * grp
    ns = sg * n_state
    tt = min(tt, seq)

    lr = jnp.minimum(lam_re.astype(F32), -1e-4)
    li = lam_im.astype(F32)
    dt = jnp.exp(log_dt.astype(F32))[:, None]
    mag = jnp.exp(lr * dt)
    ab_re = mag * jnp.cos(li * dt)
    ab_im = mag * jnp.sin(li * dt)
    den = lr * lr + li * li
    nr, ni = ab_re - 1.0, ab_im
    z_re = (nr * lr + ni * li) / den
    z_im = (ni * lr - nr * li) / den
    bb_re = z_re[..., None] * b_re - z_im[..., None] * b_im
    bb_im = z_re[..., None] * b_im + z_im[..., None] * b_re

    eye = jnp.eye(sg, dtype=F32)

    def block_diag_in(bb):
        t = bb.reshape(slabs, sg, n_state, grp)
        return jnp.einsum('sgph,gk->sghkp', t, eye).reshape(slabs, ch, ns)

    def block_diag_out(cc):
        t = cc.reshape(slabs, sg, grp, n_state)
        return jnp.einsum('sgqp,gk->sgpkq', t, eye).reshape(slabs, ns, ch)

    w_b = jnp.concatenate([block_diag_in(bb_re), block_diag_in(bb_im)], axis=2).astype(BF16)
    w_c = jnp.concatenate([block_diag_out(c_re.astype(F32)), -block_diag_out(c_im.astype(F32))],
                          axis=1).astype(BF16)

    def powers(base_re, base_im, n):
        re, im = base_re[None], base_im[None]
        while re.shape[0] < n:
            top_re, top_im = re[-1:], im[-1:]
            re, im = (jnp.concatenate([re, re * top_re - im * top_im]),
                      jnp.concatenate([im, re * top_im + im * top_re]))
        return re[:n].reshape(n, slabs, ns), im[:n].reshape(n, slabs, ns)

    seg = tt // SUBLANES
    pw_re, pw_im = powers(ab_re, ab_im, seg)
    sg_re, sg_im = powers(pw_re[-1], pw_im[-1], SUBLANES)
    row = jnp.arange(SUBLANES)
    coefs = []
    for shift in (1, 2, 4):
        mask = (row >= shift)[:, None, None]
        coefs += [jnp.where(mask, sg_re[shift - 1][None], 0.0), jnp.where(mask, sg_im[shift - 1][None], 0.0)]
    coefs += [sg_re, sg_im]
    coefs += [jnp.broadcast_to(pw_re[:1], sg_re.shape), jnp.broadcast_to(pw_im[:1], sg_im.shape)]
    coef = jnp.stack(coefs, axis=0).transpose(2, 0, 1, 3)
    pw = jnp.stack([pw_re, pw_im], axis=0).transpose(2, 0, 1, 3)
    r_idx = jnp.arange(tt)
    src_time = (r_idx % SUBLANES) * seg + r_idx // SUBLANES
    perm = (src_time[:, None] == jnp.arange(tt)[None, :]).astype(BF16)

    nt = seq // tt
    return pl.pallas_call(
        _s5_kernel, out_shape=jax.ShapeDtypeStruct((m, d), BF16),
        grid=(bsz, slabs, nt),
        in_specs=[pl.BlockSpec((tt, ch), lambda b, s, t: (b * nt + t, s)),
                  pl.BlockSpec((tt, tt), lambda b, s, t: (0, 0)),
                  pl.BlockSpec((tt, tt), lambda b, s, t: (0, 0)),
                  pl.BlockSpec((1, ch, 2 * ns), lambda b, s, t: (s, 0, 0)),
                  pl.BlockSpec((1, 2 * ns, ch), lambda b, s, t: (s, 0, 0)),
                  pl.BlockSpec((1, 10, SUBLANES, ns), lambda b, s, t: (s, 0, 0, 0)),
                  pl.BlockSpec((1, 2, seg, ns), lambda b, s, t: (s, 0, 0, 0)),
                  pl.BlockSpec((1, ch), lambda b, s, t: (0, s))],
        out_specs=pl.BlockSpec((tt, ch), lambda b, s, t: (b * nt + t, s)),
        scratch_shapes=[pltpu.VMEM((tt, 2 * ns), F32), pltpu.VMEM((2, 1, ns), F32)],
        compiler_params=_cp("parallel", "parallel", "arbitrary"), name="s5_scan")(
            hn, perm, perm.T, w_b, w_c, coef, pw, d_skip.reshape(1, d))


def _pad_to(x, axis, size):
    pad = size - x.shape[axis]
    if pad == 0:
        return x
    widths = [(0, 0)] * x.ndim
    widths[axis] = (0, pad)
    return jnp.pad(x, widths)


def _round_up(n, k):
    return (n + k - 1) // k * k


def _gla_layer(hn, g_post, w_in_all, layer, w_gk_up, b_gk_up, o_norm, w_out, bsz, seq):
    rank = w_gk_up.shape[0]
    n_main = w_in_all.shape[2] - rank
    rank_pad = _round_up(rank, LANES)
    w_in_t = jnp.swapaxes(w_in_all, 1, 2)
    proj = _matmul(hn, w_in_t, BF16, layer=layer, n_out=n_main, w_is_transposed=True)
    gk_lo = _matmul(hn, w_in_t, F32, layer=layer, col0=n_main, n_out=rank_pad, w_is_transposed=True)
    o = _gla_scan(proj, gk_lo, _pad_to(w_gk_up, 0, rank_pad), b_gk_up, o_norm, bsz, seq)
    return _matmul_norm(o, [w_out.astype(BF16)], g_post)


def _rwkv_layer(hn, g_post, mix, w_r, w_k, w_v, w_o, w0, w_w1, w_w2, a0, w_a1, w_a2,
                w_g1, w_g2, k_k, k_a, r_k, lnx_g, lnx_b, bsz, seq):
    d = hn.shape[1]
    rkv = _shiftmix_matmul(hn, jnp.stack([mix[0], mix[2], mix[3]]),
                           jnp.stack([w_r, w_k, w_v]).astype(BF16), BF16, seq)
    rank_pad = _round_up(max(w_w1.shape[1], w_a1.shape[1], w_g1.shape[1]), LANES)
    w1 = jnp.stack([_pad_to(w, 1, rank_pad) for w in (w_w1, w_a1, w_g1)]).astype(BF16)
    w2 = jnp.stack([_pad_to(w, 0, rank_pad) for w in (w_w2, w_a2, w_g2)]).astype(BF16)
    low = _shiftmix_matmul(hn, jnp.stack([mix[1], mix[4], mix[5]]), w1, F32, seq)
    bias = jnp.stack([w0, a0, jnp.zeros_like(w0)]).reshape(3, 1, d)
    lr = _lowrank_out(low, w2, bias)
    params = _pad_to(jnp.stack([k_k, k_a, r_k, lnx_g, lnx_b]), 0, SUBLANES)
    y = _rwkv_scan(rkv, lr, params, bsz, seq)
    return _matmul_norm(y, [w_o.astype(BF16)], g_post)


def _s5_layer(hn, g_post, lam_re, lam_im, log_dt, b_re, b_im, c_re, c_im, d_skip,
              w_glu1, w_glu2, bsz, seq):
    y = _s5_scan(hn, lam_re, lam_im, log_dt, b_re, b_im, c_re, c_im, d_skip, bsz, seq)
    return _matmul_norm(y, [w_glu1.astype(BF16), w_glu2.astype(BF16)], g_post)


def kernel(x, mem, norm_gains, mem_norm, mem_w_kv, xa_wq, xa_wo, mlp_w1, mlp_w2, gla_w_in, gla_w_gk_up, gla_b_gk_up, gla_o_norm, gla_w_out, rwkv_mix, rwkv_w_r, rwkv_w_k, rwkv_w_v, rwkv_w_o, rwkv_w0, rwkv_w_w1, rwkv_w_w2, rwkv_a0, rwkv_w_a1, rwkv_w_a2, rwkv_w_g1, rwkv_w_g2, rwkv_k_k, rwkv_k_a, rwkv_r_k, rwkv_lnx_g, rwkv_lnx_b, s5_lam_re, s5_lam_im, s5_log_dt, s5_b_re, s5_b_im, s5_c_re, s5_c_im, s5_d, s5_w_glu1, s5_w_glu2):
    bsz, seq, d = x.shape
    n_mem = mem.shape[1]
    depth = norm_gains.shape[0]
    assert seq % CHUNK == 0 and xa_wq.shape[2] == XA_HEADS * XA_HEAD_DIM

    mem_kv = _norm_matmul(mem.reshape(bsz * n_mem, d), mem_norm, mem_w_kv.astype(BF16), BF16)
    mem_kv = mem_kv.reshape(bsz, n_mem, mem_w_kv.shape[1])

    x = x.reshape(bsz * seq, d)
    hn = _norm(x, norm_gains[0, 0], BF16)
    for i in range(depth):
        kind, j = i % N_MIXERS, i // N_MIXERS
        g = norm_gains[i]
        if kind == 0:
            branch = _gla_layer(hn, g[1], gla_w_in, j, gla_w_gk_up[j], gla_b_gk_up[j],
                           gla_o_norm[j], gla_w_out[j], bsz, seq)
        elif kind == 1:
            branch = _rwkv_layer(hn, g[1], rwkv_mix[j], rwkv_w_r[j], rwkv_w_k[j], rwkv_w_v[j],
                            rwkv_w_o[j], rwkv_w0[j], rwkv_w_w1[j], rwkv_w_w2[j], rwkv_a0[j],
                            rwkv_w_a1[j], rwkv_w_a2[j], rwkv_w_g1[j], rwkv_w_g2[j], rwkv_k_k[j],
                            rwkv_k_a[j], rwkv_r_k[j], rwkv_lnx_g[j], rwkv_lnx_b[j], bsz, seq)
        else:
            branch = _s5_layer(hn, g[1], s5_lam_re[j], s5_lam_im[j], s5_log_dt[j], s5_b_re[j],
                          s5_b_im[j], s5_c_re[j], s5_c_im[j], s5_d[j], s5_w_glu1[j], s5_w_glu2[j],
                          bsz, seq)
        x, hn = _cross_attention(x, branch, g[2], xa_wq[i].astype(BF16), mem_kv, xa_wo[i].astype(BF16), g[3],
                                 g[4], seq)
        hidden = _matmul(hn, mlp_w1, BF16, act="relu2", layer=i)
        branch = _matmul_acc(hidden, mlp_w2, layer=i)
        x, hn = _add_norm(x, branch, g[5], norm_gains[i + 1, 0] if i + 1 < depth else None)
    return x.reshape(bsz, seq, d)
```

```python
import functools

import jax
import jax.numpy as jnp
from jax import lax
from jax.experimental import pallas as pl
from jax.experimental.pallas import tpu as pltpu

F32 = jnp.float32
BF16 = jnp.bfloat16

NORM_EPS = 1e-6
CHUNK = 64
N_MIXERS = 3
GLA_HEADS = 4
GLA_GATE_NORMALIZER = 16.0
RWKV_HEAD = 64
RWKV_LNX_EPS = 64e-5
XA_HEADS = 4
XA_HEAD_DIM = 128
S5_SLAB_GROUPS = 16
SUBLANES = 8
LANES = 128
VMEM_LIMIT = 60 * 1024 * 1024


def _cp(*sem):
    return pltpu.CompilerParams(dimension_semantics=sem, vmem_limit_bytes=VMEM_LIMIT)


def _rms(x, gain, eps=NORM_EPS):
    ms = jnp.mean(x * x, axis=-1, keepdims=True)
    return x * lax.rsqrt(ms + eps) * gain


def _stat_scratch(rows):
    return pltpu.VMEM((rows, LANES), F32)


def _for_row_blocks(n_rows, body, carry=None, rows=2 * SUBLANES, unroll=1):
    def step(i, c):
        return body(pl.ds(pl.multiple_of(i * rows, rows), rows), c)
    return lax.fori_loop(0, n_rows // rows, step, carry, unroll=unroll)


def _sumsq_lanes(x):
    acc = x[:, 0:LANES] * x[:, 0:LANES]
    for j in range(1, x.shape[1] // LANES):
        blk = x[:, j * LANES:(j + 1) * LANES]
        acc = acc + blk * blk
    return acc


def _finish_scales(s_ref, d):
    ms = jnp.sum(s_ref[...], axis=-1, keepdims=True) * (1.0 / d)
    s_ref[...] = jnp.broadcast_to(lax.rsqrt(ms + NORM_EPS), s_ref.shape)


def _rows_of(ref):
    return lambda rs: ref[rs, :]


def _row_scales_into(s_ref, load, d):
    def body(rs, c):
        s_ref[rs, :] = _sumsq_lanes(load(rs))
        return c
    _for_row_blocks(s_ref.shape[0], body, rows=SUBLANES, unroll=2)
    _finish_scales(s_ref, d)


GAIN_ROWS = 2 * SUBLANES


def _row_vector(v):
    return jnp.broadcast_to(v.reshape(1, -1), (GAIN_ROWS, v.shape[-1]))


def _vector_spec(d, index_map):
    return pl.BlockSpec((GAIN_ROWS, d), index_map)


def _scaled(x, scale, g_ref):
    return x * jnp.tile(scale, (1, x.shape[1] // LANES)) * g_ref[0:x.shape[0], :]


def _norm_rows_into(dst_ref, load, g_ref, s_ref):
    _row_scales_into(s_ref, load, dst_ref.shape[1])

    def body(rs, c):
        dst_ref[rs, :] = _scaled(load(rs), s_ref[rs, :], g_ref).astype(dst_ref.dtype)
        return c
    _for_row_blocks(dst_ref.shape[0], body, unroll=2)


def _add_norm_rows(o_ref, load_m, load_res, g_ref, s_ref, hn_ref=None, g_next_ref=None):
    _row_scales_into(s_ref, load_m, o_ref.shape[1])

    def body(rs, c):
        x_new = load_res(rs) + _scaled(load_m(rs), s_ref[rs, :], g_ref)
        o_ref[rs, :] = x_new
        if hn_ref is not None:
            s_ref[rs, :] = _sumsq_lanes(x_new)
        return c
    _for_row_blocks(o_ref.shape[0], body, rows=SUBLANES, unroll=2)
    if hn_ref is not None:
        _finish_scales(s_ref, o_ref.shape[1])

        def body2(rs, c):
            hn_ref[rs, :] = _scaled(o_ref[rs, :], s_ref[rs, :], g_next_ref).astype(hn_ref.dtype)
            return c
        _for_row_blocks(o_ref.shape[0], body2, unroll=2)


def _dot(a, b):
    return jnp.dot(a, b, preferred_element_type=F32)


def _dot_nt(a, b):
    return lax.dot_general(a, b, (((1,), (1,)), ((), ())), preferred_element_type=F32)


def _dot_tn(a, b):
    return lax.dot_general(a, b, (((0,), (0,)), ((), ())), preferred_element_type=F32)


def _split_bf16(x, pieces):
    out = []
    for _ in range(pieces - 1):
        out.append(x.astype(BF16))
        x = x - out[-1].astype(F32)
    return out + [x.astype(BF16)]


def _dot_ones(ones, x):
    hi, mid, lo = _split_bf16(x, 3)
    return _dot(ones, hi) + (_dot(ones, mid) + _dot(ones, lo))


def _dot_split(a, b):
    a_hi, a_lo = _split_bf16(a, 2)
    b_hi, b_lo = _split_bf16(b, 2)
    return _dot(a_hi, b_hi) + (_dot(a_hi, b_lo) + _dot(a_lo, b_hi))


def _sigmoid(x):
    return 1.0 / (1.0 + jnp.exp(-x))


def _softplus(x):
    return jnp.maximum(x, 0.0) + jnp.log1p(jnp.exp(-jnp.abs(x)))


def _norm_kernel(x_ref, g_ref, o_ref, s_ref):
    _norm_rows_into(o_ref, _rows_of(x_ref), g_ref, s_ref)


def _norm(x, gain, out_dtype=F32, tm=256):
    m, d = x.shape
    tm = min(tm, m)
    return pl.pallas_call(
        _norm_kernel, out_shape=jax.ShapeDtypeStruct((m, d), out_dtype),
        grid=(m // tm,),
        in_specs=[pl.BlockSpec((tm, d), lambda i: (i, 0)), _vector_spec(d, lambda i: (0, 0))],
        out_specs=pl.BlockSpec((tm, d), lambda i: (i, 0)),
        scratch_shapes=[_stat_scratch(tm)],
        compiler_params=_cp("parallel"), name="rmsnorm")(x, _row_vector(gain))


def _norm_matmul_kernel(x_ref, g_ref, w_ref, o_ref, hn_ref, s_ref):
    @pl.when(pl.program_id(1) == 0)
    def _():
        _norm_rows_into(hn_ref, _rows_of(x_ref), g_ref, s_ref)

    o_ref[...] = _dot(hn_ref[...], w_ref[...]).astype(o_ref.dtype)


def _norm_matmul(x, gain, w, out_dtype, tm=512, tn=512):
    m, d = x.shape
    n = w.shape[1]
    tm, tn = min(tm, m), min(tn, n)
    return pl.pallas_call(
        _norm_matmul_kernel, out_shape=jax.ShapeDtypeStruct((m, n), out_dtype),
        grid=(m // tm, n // tn),
        in_specs=[pl.BlockSpec((tm, d), lambda i, j: (i, 0)),
                  _vector_spec(d, lambda i, j: (0, 0)),
                  pl.BlockSpec((d, tn), lambda i, j: (0, j))],
        out_specs=pl.BlockSpec((tm, tn), lambda i, j: (i, j)),
        scratch_shapes=[pltpu.VMEM((tm, d), BF16), _stat_scratch(tm)],
        compiler_params=_cp("parallel", "arbitrary"), name="norm_matmul")(x, _row_vector(gain), w)


def _matmul_norm_kernel(*refs, glu, tn):
    if glu:
        a_ref, w_ref, w2_ref, g_ref, o_ref, acc_ref, s_ref = refs
    else:
        a_ref, w_ref, g_ref, o_ref, acc_ref, s_ref = refs
    j = pl.program_id(1)
    a = a_ref[...]
    y = _dot(a, w_ref[...])
    if glu:
        y = y * _sigmoid(_dot(a, w2_ref[...]))
    acc_ref[:, pl.ds(pl.multiple_of(j * tn, tn), tn)] = y

    @pl.when(j == pl.num_programs(1) - 1)
    def _():
        _norm_rows_into(o_ref, _rows_of(acc_ref), g_ref, s_ref)


def _matmul_norm(a, ws, gain, tm=1024):
    m, k = a.shape
    n = ws[0].shape[1]
    glu = len(ws) == 2
    tm, tn = min(tm, m), min(256 if glu else 512, n)
    w_specs = [pl.BlockSpec((k, tn), lambda i, j: (0, j)) for _ in ws]
    return pl.pallas_call(
        functools.partial(_matmul_norm_kernel, glu=glu, tn=tn),
        out_shape=jax.ShapeDtypeStruct((m, n), BF16),
        grid=(m // tm, n // tn),
        in_specs=[pl.BlockSpec((tm, k), lambda i, j: (i, 0))] + w_specs + [
            _vector_spec(n, lambda i, j: (0, 0))],
        out_specs=pl.BlockSpec((tm, n), lambda i, j: (i, 0)),
        scratch_shapes=[pltpu.VMEM((tm, n), F32), _stat_scratch(tm)],
        compiler_params=_cp("parallel", "arbitrary"),
        name="glu_norm" if glu else "matmul_norm")(a, *ws, _row_vector(gain))


def _xa_kernel(x_ref, br_ref, g_in_ref, wq_ref, k_ref, v_ref, wo_ref, g_out_ref, g_next_ref, o_ref, hn_out_ref,
               hn_ref, x1_ref, s_ref):
    def add_branch(rs):
        x1_ref[rs, :] = x_ref[rs, :] + br_ref[rs, :].astype(F32)
        return x1_ref[rs, :]

    _row_scales_into(s_ref, add_branch, x_ref.shape[1])
    x1 = _rows_of(x1_ref)

    def normed(rs, c):
        hn_ref[rs, :] = _scaled(x1(rs), s_ref[rs, :], g_in_ref).astype(hn_ref.dtype)
        return c
    _for_row_blocks(hn_ref.shape[0], normed, unroll=2)
    q = _dot(hn_ref[...], wq_ref[...]) * (XA_HEAD_DIM ** -0.5)
    heads = []
    for h in range(XA_HEADS):
        sl = slice(h * XA_HEAD_DIM, (h + 1) * XA_HEAD_DIM)
        s = _dot_nt(q[:, sl].astype(BF16), k_ref[0, :, sl])
        s = s - jnp.max(s, axis=-1, keepdims=True)
        p = jnp.exp(s)
        p = p / jnp.sum(p, axis=-1, keepdims=True)
        heads.append(_dot(p.astype(BF16), v_ref[0, :, sl]))
    o = jnp.concatenate(heads, axis=-1).astype(BF16)
    o_ref[...] = _dot(o, wo_ref[...])
    _add_norm_rows(o_ref, _rows_of(o_ref), x1, g_out_ref, s_ref, hn_out_ref, g_next_ref)


def _cross_attention(x, branch, g_in, wq, mem_kv, wo, g_out, g_next, seq, tm=256):
    m, d = x.shape
    xw = wq.shape[1]
    n_mem = mem_kv.shape[1]
    tm = min(tm, seq)
    tiles_per_seq = seq // tm
    return pl.pallas_call(
        _xa_kernel, out_shape=[jax.ShapeDtypeStruct((m, d), F32), jax.ShapeDtypeStruct((m, d), BF16)],
        grid=(m // tm,),
        in_specs=[pl.BlockSpec((tm, d), lambda i: (i, 0)),
                  pl.BlockSpec((tm, d), lambda i: (i, 0)),
                  _vector_spec(d, lambda i: (0, 0)),
                  pl.BlockSpec((d, xw), lambda i: (0, 0)),
                  pl.BlockSpec((1, n_mem, xw), lambda i: (i // tiles_per_seq, 0, 0)),
                  pl.BlockSpec((1, n_mem, xw), lambda i: (i // tiles_per_seq, 0, 1)),
                  pl.BlockSpec((xw, d), lambda i: (0, 0)),
                  _vector_spec(d, lambda i: (0, 0)),
                  _vector_spec(d, lambda i: (0, 0))],
        out_specs=[pl.BlockSpec((tm, d), lambda i: (i, 0)), pl.BlockSpec((tm, d), lambda i: (i, 0))],
        scratch_shapes=[pltpu.VMEM((tm, d), BF16), pltpu.VMEM((tm, d), F32), _stat_scratch(tm)],
        compiler_params=_cp("parallel"), name="cross_attention")(
            x, branch, _row_vector(g_in), wq, mem_kv, mem_kv, wo, _row_vector(g_out), _row_vector(g_next))


def _matmul_kernel(a_ref, w_ref, o_ref, *, act, col0, w_cols, w_is_transposed):
    w = w_ref[...]
    if w_cols is not None:
        tn = o_ref.shape[1]
        shape, axis = ((tn, 1), 0) if w_is_transposed else ((1, tn), 1)
        cols = col0 + pl.program_id(1) * tn + lax.broadcasted_iota(jnp.int32, shape, axis)
        w = jnp.where(cols < w_cols, w, 0.0)
    w = w.astype(BF16)
    y = _dot_nt(a_ref[...], w) if w_is_transposed else _dot(a_ref[...], w)
    if act == "relu2":
        y = jnp.maximum(y, 0.0)
        y = y * y
    o_ref[...] = y.astype(o_ref.dtype)


def _layer_weight_spec(w, layer, block, index_map):
    if w.ndim == 2:
        return pl.BlockSpec(block, index_map)
    return pl.BlockSpec((None,) + block, lambda *idx: (layer,) + index_map(*idx))


def _matmul(a, w, out_dtype, act=None, layer=None, col0=0, n_out=None, w_is_transposed=False, tm=2048, tn=512):
    m, k = a.shape
    w_cols = w.shape[-2] if w_is_transposed else w.shape[-1]
    n = w_cols if n_out is None else n_out
    tm, tn = min(tm, m), min(tn, n)
    assert col0 % tn == 0
    ragged = col0 + n > w_cols
    block, index_map = (((tn, k), lambda i, j: (col0 // tn + j, 0)) if w_is_transposed
                        else ((k, tn), lambda i, j: (0, col0 // tn + j)))
    return pl.pallas_call(
        functools.partial(_matmul_kernel, act=act, col0=col0, w_cols=w_cols if ragged else None,
                          w_is_transposed=w_is_transposed),
        out_shape=jax.ShapeDtypeStruct((m, n), out_dtype),
        grid=(m // tm, n // tn),
        in_specs=[pl.BlockSpec((tm, k), lambda i, j: (i, 0), pipeline_mode=pl.Buffered(1)),
                  _layer_weight_spec(w, layer, block, index_map)],
        out_specs=pl.BlockSpec((tm, tn), lambda i, j: (i, j)),
        compiler_params=_cp("parallel", "arbitrary"), name="matmul")(a, w)


def _matmul_acc_kernel(a_ref, w_ref, o_ref, *, ts):
    @pl.when(pl.program_id(2) == 0)
    def _():
        o_ref[...] = jnp.zeros_like(o_ref)

    a = a_ref[...]
    for n in range(o_ref.shape[1] // ts):
        sl = slice(n * ts, (n + 1) * ts)
        o_ref[:, sl] += _dot(a, w_ref[:, sl].astype(BF16))


def _matmul_acc(a, w, layer=None, tm=2048, tn=1024, tk=2048):
    m, k = a.shape
    n = w.shape[-1]
    tm, tn, tk = min(tm, m), min(tn, n), min(tk, k)
    return pl.pallas_call(
        functools.partial(_matmul_acc_kernel, ts=min(256, tn)),
        out_shape=jax.ShapeDtypeStruct((m, n), F32),
        grid=(m // tm, n // tn, k // tk),
        in_specs=[pl.BlockSpec((tm, tk), lambda i, j, kk: (i, kk)),
                  _layer_weight_spec(w, layer, (tk, tn), lambda i, j, kk: (kk, j))],
        out_specs=pl.BlockSpec((tm, tn), lambda i, j, kk: (i, j)),
        compiler_params=_cp("parallel", "parallel", "arbitrary"), name="matmul_acc")(a, w)


def _add_norm_kernel(x_ref, m_ref, gp_ref, gn_ref, o_ref, *rest):
    s_ref = rest[-1]
    hn_ref = rest[0] if len(rest) == 2 else None
    _add_norm_rows(o_ref, _rows_of(m_ref), _rows_of(x_ref), gp_ref, s_ref, hn_ref, gn_ref)


def _add_norm(x, m_branch, g_post, g_next, tm=256):
    m, d = x.shape
    tm = min(tm, m)
    row = pl.BlockSpec((tm, d), lambda i: (i, 0))
    vec = _vector_spec(d, lambda i: (0, 0))
    emit_hn = g_next is not None
    out_shape = [jax.ShapeDtypeStruct((m, d), F32)] + ([jax.ShapeDtypeStruct((m, d), BF16)] if emit_hn else [])
    outs = pl.pallas_call(
        _add_norm_kernel, out_shape=out_shape, grid=(m // tm,),
        in_specs=[row, row, vec, vec], out_specs=[row] * len(out_shape),
        scratch_shapes=[_stat_scratch(tm)],
        compiler_params=_cp("parallel"), name="add_norm")(
            x, m_branch, _row_vector(g_post), _row_vector(g_next if emit_hn else g_post))
    return (outs[0], outs[1]) if emit_hn else (outs[0], None)


def _gla_kernel(q_ref, k_ref, v_ref, g_ref, gk_ref, wup_ref, bup_ref, onorm_ref, tril_ref,
                o_ref, st_ref):
    @pl.when(pl.program_id(1) == 0)
    def _():
        st_ref[...] = jnp.zeros_like(st_ref)

    nb, _, dk = q_ref.shape
    bs = range(nb)
    gk = gk_ref[...].reshape(nb * CHUNK, gk_ref.shape[2])
    z = _dot_split(gk, wup_ref[...]) + bup_ref[...]
    log_alpha = (jnp.minimum(z, 0.0) - jnp.log1p(jnp.exp(-jnp.abs(z)))) / GLA_GATE_NORMALIZER
    cum_all = _dot_ones(tril_ref[...], log_alpha)
    cum = [cum_all[b * CHUNK:(b + 1) * CHUNK] for b in bs]
    cum_last = [cum[b][CHUNK - 1:CHUNK, :] for b in bs]
    k_dec = [(k_ref[b].astype(F32) * jnp.exp(cum_last[b] - cum[b])).astype(BF16) for b in bs]
    st = [st_ref[b] * jnp.exp(cum_last[b]) + _dot_tn(v_ref[b], k_dec[b]) for b in bs]
    for b in bs:
        st_ref[b] = st[b]
    q = [(q_ref[b].astype(F32) * dk ** -0.5).astype(BF16) for b in bs]
    o = [_dot_nt(q[b], st[b].astype(BF16)) for b in bs]
    for b in bs:
        g = g_ref[b].astype(F32)
        o_ref[b] = (_rms(o[b], onorm_ref[...]) * (g * _sigmoid(g))).astype(o_ref.dtype)


def _gla_scan(proj, gk_lo, w_up, b_up, o_norm, bsz, seq):
    dk_all = w_up.shape[1]
    dkh = dk_all // GLA_HEADS
    dv_all = (proj.shape[1] - 2 * dk_all) // 2
    dvh = dv_all // GLA_HEADS
    nc = seq // CHUNK
    rank_pad = gk_lo.shape[1]
    proj = proj.reshape(bsz, seq, proj.shape[1])
    gk_lo = gk_lo.reshape(bsz, seq, rank_pad)
    tril = jnp.kron(jnp.eye(bsz, dtype=F32), jnp.tril(jnp.ones((CHUNK, CHUNK), F32))).astype(BF16)
    k_off = dk_all // dkh
    v_off = 2 * dk_all // dvh
    g_off = v_off + dv_all // dvh
    out = pl.pallas_call(
        _gla_kernel, out_shape=jax.ShapeDtypeStruct((bsz, seq, dv_all), BF16),
        grid=(GLA_HEADS, nc),
        in_specs=[pl.BlockSpec((bsz, CHUNK, dkh), lambda h, c: (0, c, h)),
                  pl.BlockSpec((bsz, CHUNK, dkh), lambda h, c: (0, c, k_off + h)),
                  pl.BlockSpec((bsz, CHUNK, dvh), lambda h, c: (0, c, v_off + h)),
                  pl.BlockSpec((bsz, CHUNK, dvh), lambda h, c: (0, c, g_off + h)),
                  pl.BlockSpec((bsz, CHUNK, rank_pad), lambda h, c: (0, c, 0)),
                  pl.BlockSpec((rank_pad, dkh), lambda h, c: (0, h)),
                  pl.BlockSpec((1, dkh), lambda h, c: (0, h)),
                  pl.BlockSpec((1, dvh), lambda h, c: (0, 0)),
                  pl.BlockSpec((bsz * CHUNK, bsz * CHUNK), lambda h, c: (0, 0))],
        out_specs=pl.BlockSpec((bsz, CHUNK, dvh), lambda h, c: (0, c, h)),
        scratch_shapes=[pltpu.VMEM((bsz, dvh, dkh), F32)],
        compiler_params=_cp("parallel", "arbitrary"), name="gla_scan")(
            proj, proj, proj, proj, gk_lo, w_up, b_up.reshape(1, dk_all), o_norm.reshape(1, dvh), tril)
    return out.reshape(bsz * seq, dv_all)


def _shiftmix_matmul_kernel(hn_ref, prev_ref, mix_ref, w_ref, o_ref, xm_ref, *, tiles_per_seq):
    @pl.when(pl.program_id(2) == 0)
    def _():
        prev_rows = prev_ref.shape[0]
        last = prev_ref[...].astype(F32)[prev_rows - 1:prev_rows, :]
        first_tile = pl.program_id(0) % tiles_per_seq == 0
        last = jnp.where(first_tile, 0.0, last)

        def body(rs, last):
            hn = hn_ref[rs, :].astype(F32)
            rows = lax.broadcasted_iota(jnp.int32, hn.shape, 0)
            shifted = jnp.where(rows == 0, last, pltpu.roll(hn, 1, axis=0))
            xm_ref[rs, :] = (hn + (shifted - hn) * mix_ref[0]).astype(BF16)
            return hn[hn.shape[0] - 1:, :]

        _for_row_blocks(hn_ref.shape[0], body, last)

    o_ref[0] = _dot(xm_ref[...], w_ref[0]).astype(o_ref.dtype)


def _shiftmix_matmul(hn, mix, w, out_dtype, seq, tm=2048, tn=512):
    m, d = hn.shape
    p_cnt, _, n = w.shape
    tm, tn = min(tm, seq), min(tn, n)
    prev_rows = 2 * SUBLANES
    blk = tm // prev_rows
    return pl.pallas_call(
        functools.partial(_shiftmix_matmul_kernel, tiles_per_seq=seq // tm),
        out_shape=jax.ShapeDtypeStruct((p_cnt, m, n), out_dtype),
        grid=(m // tm, p_cnt, n // tn),
        in_specs=[pl.BlockSpec((tm, d), lambda i, p, j: (i, 0), pipeline_mode=pl.Buffered(1)),
                  pl.BlockSpec((prev_rows, d), lambda i, p, j: (jnp.maximum(i * blk - 1, 0), 0)),
                  pl.BlockSpec((1, GAIN_ROWS, d), lambda i, p, j: (p, 0, 0)),
                  pl.BlockSpec((1, d, tn), lambda i, p, j: (p, 0, j))],
        out_specs=pl.BlockSpec((1, tm, tn), lambda i, p, j: (p, i, j)),
        scratch_shapes=[pltpu.VMEM((tm, d), BF16)],
        compiler_params=_cp("parallel", "arbitrary", "arbitrary"), name="shiftmix_matmul")(
            hn, hn, jnp.broadcast_to(mix.reshape(p_cnt, 1, d), (p_cnt, GAIN_ROWS, d)), w)


def _lowrank_out_kernel(h_ref, w_ref, b_ref, o_ref):
    p = pl.program_id(0)
    h = h_ref[0]
    act = jnp.where(p == 0, jnp.tanh(h), jnp.where(p == 1, h, _sigmoid(h)))
    o_ref[0] = _dot(act.astype(BF16), w_ref[0]) + b_ref[0]


def _lowrank_out(h, w2, bias, tm=512):
    p_cnt, m, r = h.shape
    d = w2.shape[2]
    tm = min(tm, m)
    return pl.pallas_call(
        _lowrank_out_kernel, out_shape=jax.ShapeDtypeStruct((p_cnt, m, d), F32),
        grid=(p_cnt, m // tm),
        in_specs=[pl.BlockSpec((1, tm, r), lambda p, i: (p, i, 0)),
                  pl.BlockSpec((1, r, d), lambda p, i: (p, 0, 0)),
                  pl.BlockSpec((1, 1, d), lambda p, i: (p, 0, 0))],
        out_specs=pl.BlockSpec((1, tm, d), lambda p, i: (p, i, 0)),
        compiler_params=_cp("arbitrary", "arbitrary"), name="lowrank_out")(h, w2, bias)


def _rwkv_kernel(rkv_ref, lr_ref, par_ref, bd_ref, tril_ref, o_ref, st_ref):
    @pl.when(pl.program_id(2) == 0)
    def _():
        st_ref[...] = jnp.zeros_like(st_ref)

    n = RWKV_HEAD
    heads = 2 * st_ref.shape[0]
    r = rkv_ref[0].astype(F32)
    k = rkv_ref[1].astype(F32)
    v = rkv_ref[2].astype(F32)
    w_log = -_softplus(-lr_ref[0]) - 0.5
    log_w = -jnp.exp(w_log)
    a = _sigmoid(lr_ref[1])
    gate = lr_ref[2]
    k_k, k_a, r_k = par_ref[0:1, :], par_ref[1:2, :], par_ref[2:3, :]
    lnx_g, lnx_b = par_ref[3:4, :], par_ref[4:5, :]
    bd = bd_ref[...]
    grp = bd.shape[0]

    def head_sums(x):
        x = x.astype(BF16)
        return jnp.concatenate([_dot(x[:, i:i + grp], bd) for i in range(0, x.shape[1], grp)], axis=1)

    kk = k * k_k
    kk = kk / jnp.maximum(jnp.sqrt(head_sums(kk * kk)), 1e-12)
    k2 = k * (1.0 + (a - 1.0) * k_a)
    cw = _dot_ones(tril_ref[...], log_w)
    cw_last = cw[CHUNK - 1:CHUNK, :]
    e_neg = jnp.exp(-cw)
    e_end = jnp.exp(cw_last - cw)
    a_t = (-kk * jnp.exp(cw - log_w)).astype(BF16)
    r_t = (r * jnp.exp(cw)).astype(BF16)
    b_vec = kk * a
    b_t = (b_vec * e_neg).astype(BF16)
    k_t = (k2 * e_neg).astype(BF16)
    b_w = (b_vec * e_end).astype(BF16)
    k_w = (k2 * e_end).astype(BF16)
    w_end = jnp.exp(cw_last)
    v_b = v.astype(BF16)

    pw = 2 * n
    rows = lax.broadcasted_iota(jnp.int32, (CHUNK, 2 * pw), 0)
    cols = lax.broadcasted_iota(jnp.int32, (CHUNK, 2 * pw), 1) % CHUNK
    strict = rows > cols
    incl = rows >= cols
    t_row = lax.broadcasted_iota(jnp.int32, (CHUNK, pw), 0)
    t_col = lax.broadcasted_iota(jnp.int32, (CHUNK, pw), 1) % CHUNK
    eye = (t_row == t_col).astype(F32)

    def pair_diag(x):
        first = lax.broadcasted_iota(jnp.int32, x.shape, 1) < n
        zero = jnp.zeros_like(x)
        return jnp.concatenate([jnp.where(first, x, zero), jnp.where(first, zero, x)], axis=0)

    def lower_left(b):
        return (t_row // (2 * b) == t_col // (2 * b)) & (t_row % (2 * b) >= b) & (t_col % (2 * b) < b)

    ps = range(heads // 2)
    sls = [slice(p * pw, (p + 1) * pw) for p in ps]
    ar = [jnp.concatenate([a_t[:, sl], r_t[:, sl]], axis=0) for sl in sls]
    bk = [jnp.concatenate([pair_diag(b_t[:, sl]), pair_diag(k_t[:, sl])], axis=0) for sl in sls]
    s0 = [st_ref[p] for p in ps]
    gram = [_dot_nt(ar[p], bk[p]) for p in ps]
    proj = [_dot_nt(ar[p], s0[p].astype(BF16)) for p in ps]
    v_d = [pair_diag(v_b[:, sl]) for sl in sls]
    low = [jnp.where(strict, gram[p][:CHUNK], 0.0).astype(BF16) for p in ps]
    nil = [low[p][:, :pw] for p in ps]
    t_inv = [eye + jnp.where(lower_left(1), nil[p], 0).astype(F32) for p in ps]
    b = 2
    while b < CHUNK:
        mask = lower_left(b)
        off = [jnp.where(mask, nil[p], 0) for p in ps]
        t_b = [t_inv[p].astype(BF16) for p in ps]
        right = [_dot(off[p], pair_diag(t_b[p])).astype(BF16) for p in ps]
        t_inv = [t_inv[p] + _dot(t_b[p], pair_diag(right[p])) for p in ps]
        b *= 2
    rhs = [proj[p][:CHUNK] + _dot(low[p][:, pw:], v_d[p]) for p in ps]
    u = [_dot(t_inv[p].astype(BF16), pair_diag(rhs[p].astype(BF16))).astype(BF16) for p in ps]
    upper = [jnp.where(incl, gram[p][CHUNK:], 0.0).astype(BF16) for p in ps]
    y_pairs = [proj[p][CHUNK:] + _dot(upper[p], jnp.concatenate([pair_diag(u[p]), v_d[p]], axis=0)) for p in ps]
    pair_rows = lax.broadcasted_iota(jnp.int32, (pw, pw), 0) // n
    pair_cols = lax.broadcasted_iota(jnp.int32, (pw, pw), 1) // n
    for p in ps:
        uv = jnp.concatenate([u[p], v_b[:, sls[p]]], axis=0)
        bkw = jnp.concatenate([b_w[:, sls[p]], k_w[:, sls[p]]], axis=0)
        grown = s0[p] * w_end[:, sls[p]] + _dot_tn(uv, bkw)
        st_ref[p] = jnp.where(pair_rows == pair_cols, grown, 0.0)

    y = jnp.concatenate(y_pairs, axis=1)
    inv_n = 1.0 / n
    mu = head_sums(y) * inv_n
    yc = y - mu
    var = head_sums(yc * yc) * inv_n
    y = yc * lax.rsqrt(var + RWKV_LNX_EPS) * lnx_g + lnx_b
    bonus = head_sums(r * k2 * r_k) * v
    o_ref[...] = ((y + bonus) * gate).astype(o_ref.dtype)


def _rwkv_scan(rkv, lr, params, bsz, seq, heads_per_step=32):
    _, m, d = rkv.shape
    n_heads = d // RWKV_HEAD
    hg = min(heads_per_step, n_heads)
    w = hg * RWKV_HEAD
    nc = seq // CHUNK
    grp = min(2 * LANES, w)
    lane_head = jnp.arange(grp) // RWKV_HEAD
    bd = (lane_head[:, None] == lane_head[None, :]).astype(BF16)
    tril = jnp.tril(jnp.ones((CHUNK, CHUNK), BF16))
    n_par = params.shape[0]
    return pl.pallas_call(
        _rwkv_kernel, out_shape=jax.ShapeDtypeStruct((m, d), BF16),
        grid=(bsz, n_heads // hg, nc),
        in_specs=[pl.BlockSpec((3, CHUNK, w), lambda b, g, c: (0, b * nc + c, g)),
                  pl.BlockSpec((3, CHUNK, w), lambda b, g, c: (0, b * nc + c, g)),
                  pl.BlockSpec((n_par, w), lambda b, g, c: (0, g)),
                  pl.BlockSpec((grp, grp), lambda b, g, c: (0, 0)),
                  pl.BlockSpec((CHUNK, CHUNK), lambda b, g, c: (0, 0))],
        out_specs=pl.BlockSpec((CHUNK, w), lambda b, g, c: (b * nc + c, g)),
        scratch_shapes=[pltpu.VMEM((hg // 2, 2 * RWKV_HEAD, 2 * RWKV_HEAD), F32)],
        compiler_params=_cp("parallel", "parallel", "arbitrary"), name="rwkv_scan")(
            rkv, lr, params, bd, tril)


def _cmul_add(acc_r, acc_i, cr, ci, xr, xi):
    return acc_r + cr * xr - ci * xi, acc_i + cr * xi + ci * xr


def _s5_kernel(h_ref, perm_ref, unperm_ref, wb_ref, wc_ref, coef_ref, pow_ref, d_ref, o_ref, x_ref, carry_ref):
    @pl.when(pl.program_id(2) == 0)
    def _():
        carry_ref[...] = jnp.zeros_like(carry_ref)

    tt = h_ref.shape[0]
    ns = carry_ref.shape[2]
    seg = tt // SUBLANES
    h = _dot(perm_ref[...], h_ref[...]).astype(BF16)
    x_ref[...] = _dot(h, wb_ref[0])
    a_r, a_i = coef_ref[0, 8], coef_ref[0, 9]

    def step_rows(k, c_r, c_i, x_r, x_i):
        rows = pl.ds(pl.multiple_of(k * SUBLANES, SUBLANES), SUBLANES)
        xr, xi = _cmul_add(x_ref[rows, 0:ns], x_ref[rows, ns:2 * ns], c_r, c_i, x_r, x_i)
        x_ref[rows, 0:ns] = xr
        x_ref[rows, ns:2 * ns] = xi
        return xr, xi

    zeros = jnp.zeros((SUBLANES, ns), F32)
    end_r, end_i = lax.fori_loop(0, seg, lambda k, c: step_rows(k, a_r, a_i, *c), (zeros, zeros))

    for s_idx, shift in enumerate((1, 2, 4)):
        end_r, end_i = _cmul_add(end_r, end_i, coef_ref[0, 2 * s_idx], coef_ref[0, 2 * s_idx + 1],
                                 pltpu.roll(end_r, shift, axis=0), pltpu.roll(end_i, shift, axis=0))
    in_r = jnp.broadcast_to(carry_ref[0], (SUBLANES, ns))
    in_i = jnp.broadcast_to(carry_ref[1], (SUBLANES, ns))
    end_r, end_i = _cmul_add(end_r, end_i, coef_ref[0, 6], coef_ref[0, 7], in_r, in_i)
    carry_ref[0] = end_r[SUBLANES - 1:SUBLANES, :]
    carry_ref[1] = end_i[SUBLANES - 1:SUBLANES, :]
    first = lax.broadcasted_iota(jnp.int32, (SUBLANES, ns), 0) == 0
    in_r = jnp.where(first, in_r, pltpu.roll(end_r, 1, axis=0))
    in_i = jnp.where(first, in_i, pltpu.roll(end_i, 1, axis=0))

    def correct(k, c):
        p_r = jnp.broadcast_to(pow_ref[0, 0, pl.ds(k, 1), :], (SUBLANES, ns))
        p_i = jnp.broadcast_to(pow_ref[0, 1, pl.ds(k, 1), :], (SUBLANES, ns))
        step_rows(k, p_r, p_i, in_r, in_i)
        return c

    lax.fori_loop(0, seg, correct, 0, unroll=2)

    y = _dot(x_ref[...].astype(BF16), wc_ref[0]) + d_ref[...] * h.astype(F32)
    y = (0.5 * y * (1.0 + lax.erf(y * (2.0 ** -0.5)))).astype(BF16)
    o_ref[...] = _dot(unperm_ref[...], y).astype(o_ref.dtype)


def _s5_scan(hn, lam_re, lam_im, log_dt, b_re, b_im, c_re, c_im, d_skip, bsz, seq, tt=1024):
    m, d = hn.shape
    n_groups, n_state, grp = b_re.shape
    sg = min(S5_SLAB_GROUPS, n_groups)
    slabs = n_groups // sg
    ch = sg * grp
    ns = sg * n_state
    tt = min(tt, seq)

    lr = jnp.minimum(lam_re.astype(F32), -1e-4)
    li = lam_im.astype(F32)
    dt = jnp.exp(log_dt.astype(F32))[:, None]
    mag = jnp.exp(lr * dt)
    ab_re = mag * jnp.cos(li * dt)
    ab_im = mag * jnp.sin(li * dt)
    den = lr * lr + li * li
    nr, ni = ab_re - 1.0, ab_im
    z_re = (nr * lr + ni * li) / den
    z_im = (ni * lr - nr * li) / den
    bb_re = z_re[..., None] * b_re - z_im[..., None] * b_im
    bb_im = z_re[..., None] * b_im + z_im[..., None] * b_re

    eye = jnp.eye(sg, dtype=F32)

    def block_diag_in(bb):
        t = bb.reshape(slabs, sg, n_state, grp)
        return jnp.einsum('sgph,gk->sghkp', t, eye).reshape(slabs, ch, ns)

    def block_diag_out(cc):
        t = cc.reshape(slabs, sg, grp, n_state)
        return jnp.einsum('sgqp,gk->sgpkq', t, eye).reshape(slabs, ns, ch)

    w_b = jnp.concatenate([block_diag_in(bb_re), block_diag_in(bb_im)], axis=2).astype(BF16)
    w_c = jnp.concatenate([block_diag_out(c_re.astype(F32)), -block_diag_out(c_im.astype(F32))],
                          axis=1).astype(BF16)

    def powers(base_re, base_im, n):
        re, im = base_re[None], base_im[None]
        while re.shape[0] < n:
            top_re, top_im = re[-1:], im[-1:]
            re, im = (jnp.concatenate([re, re * top_re - im * top_im]),
                      jnp.concatenate([im, re * top_im + im * top_re]))
        return re[:n].reshape(n, slabs, ns), im[:n].reshape(n, slabs, ns)

    seg = tt // SUBLANES
    pw_re, pw_im = powers(ab_re, ab_im, seg)
    sg_re, sg_im = powers(pw_re[-1], pw_im[-1], SUBLANES)
    row = jnp.arange(SUBLANES)
    coefs = []
    for shift in (1, 2, 4):
        mask = (row >= shift)[:, None, None]
        coefs += [jnp.where(mask, sg_re[shift - 1][None], 0.0), jnp.where(mask, sg_im[shift - 1][None], 0.0)]
    coefs += [sg_re, sg_im]
    coefs += [jnp.broadcast_to(pw_re[:1], sg_re.shape), jnp.broadcast_to(pw_im[:1], sg_im.shape)]
    coef = jnp.stack(coefs, axis=0).transpose(2, 0, 1, 3)
    pw = jnp.stack([pw_re, pw_im], axis=0).transpose(2, 0, 1, 3)
    r_idx = jnp.arange(tt)
    src_time = (r_idx % SUBLANES) * seg + r_idx // SUBLANES
    perm = (src_time[:, None] == jnp.arange(tt)[None, :]).astype(BF16)

    nt = seq // tt
    return pl.pallas_call(
        _s5_kernel, out_shape=jax.ShapeDtypeStruct((m, d), BF16),
        grid=(bsz, slabs, nt),
        in_specs=[pl.BlockSpec((tt, ch), lambda b, s, t: (b * nt + t, s)),
                  pl.BlockSpec((tt, tt), lambda b, s, t: (0, 0)),
                  pl.BlockSpec((tt, tt), lambda b, s, t: (0, 0)),
                  pl.BlockSpec((1, ch, 2 * ns), lambda b, s, t: (s, 0, 0)),
                  pl.BlockSpec((1, 2 * ns, ch), lambda b, s, t: (s, 0, 0)),
                  pl.BlockSpec((1, 10, SUBLANES, ns), lambda b, s, t: (s, 0, 0, 0)),
                  pl.BlockSpec((1, 2, seg, ns), lambda b, s, t: (s, 0, 0, 0)),
                  pl.BlockSpec((1, ch), lambda b, s, t: (0, s))],
        out_specs=pl.BlockSpec((tt, ch), lambda b, s, t: (b * nt + t, s)),
        scratch_shapes=[pltpu.VMEM((tt, 2 * ns), F32), pltpu.VMEM((2, 1, ns), F32)],
        compiler_params=_cp("parallel", "parallel", "arbitrary"), name="s5_scan")(
            hn, perm, perm.T, w_b, w_c, coef, pw, d_skip.reshape(1, d))


def _pad_to(x, axis, size):
    pad = size - x.shape[axis]
    if pad == 0:
        return x
    widths = [(0, 0)] * x.ndim
    widths[axis] = (0, pad)
    return jnp.pad(x, widths)


def _round_up(n, k):
    return (n + k - 1) // k * k


def _gla_layer(hn, g_post, w_in_all, layer, w_gk_up, b_gk_up, o_norm, w_out, bsz, seq):
    rank = w_gk_up.shape[0]
    n_main = w_in_all.shape[2] - rank
    rank_pad = _round_up(rank, LANES)
    w_in_t = jnp.swapaxes(w_in_all, 1, 2)
    proj = _matmul(hn, w_in_t, BF16, layer=layer, n_out=n_main, w_is_transposed=True)
    gk_lo = _matmul(hn, w_in_t, F32, layer=layer, col0=n_main, n_out=rank_pad, w_is_transposed=True)
    o = _gla_scan(proj, gk_lo, _pad_to(w_gk_up, 0, rank_pad), b_gk_up, o_norm, bsz, seq)
    return _matmul_norm(o, [w_out.astype(BF16)], g_post)


def _rwkv_layer(hn, g_post, mix, w_r, w_k, w_v, w_o, w0, w_w1, w_w2, a0, w_a1, w_a2,
                w_g1, w_g2, k_k, k_a, r_k, lnx_g, lnx_b, bsz, seq):
    d = hn.shape[1]
    rkv = _shiftmix_matmul(hn, jnp.stack([mix[0], mix[2], mix[3]]),
                           jnp.stack([w_r, w_k, w_v]).astype(BF16), BF16, seq)
    rank_pad = _round_up(max(w_w1.shape[1], w_a1.shape[1], w_g1.shape[1]), LANES)
    w1 = jnp.stack([_pad_to(w, 1, rank_pad) for w in (w_w1, w_a1, w_g1)]).astype(BF16)
    w2 = jnp.stack([_pad_to(w, 0, rank_pad) for w in (w_w2, w_a2, w_g2)]).astype(BF16)
    low = _shiftmix_matmul(hn, jnp.stack([mix[1], mix[4], mix[5]]), w1, F32, seq)
    bias = jnp.stack([w0, a0, jnp.zeros_like(w0)]).reshape(3, 1, d)
    lr = _lowrank_out(low, w2, bias)
    params = _pad_to(jnp.stack([k_k, k_a, r_k, lnx_g, lnx_b]), 0, SUBLANES)
    y = _rwkv_scan(rkv, lr, params, bsz, seq)
    return _matmul_norm(y, [w_o.astype(BF16)], g_post)


def _s5_layer(hn, g_post, lam_re, lam_im, log_dt, b_re, b_im, c_re, c_im, d_skip,
              w_glu1, w_glu2, bsz, seq):
    y = _s5_scan(hn, lam_re, lam_im, log_dt, b_re, b_im, c_re, c_im, d_skip, bsz, seq)
    return _matmul_norm(y, [w_glu1.astype(BF16), w_glu2.astype(BF16)], g_post)


def kernel(x, mem, norm_gains, mem_norm, mem_w_kv, xa_wq, xa_wo, mlp_w1, mlp_w2, gla_w_in, gla_w_gk_up, gla_b_gk_up, gla_o_norm, gla_w_out, rwkv_mix, rwkv_w_r, rwkv_w_k, rwkv_w_v, rwkv_w_o, rwkv_w0, rwkv_w_w1, rwkv_w_w2, rwkv_a0, rwkv_w_a1, rwkv_w_a2, rwkv_w_g1, rwkv_w_g2, rwkv_k_k, rwkv_k_a, rwkv_r_k, rwkv_lnx_g, rwkv_lnx_b, s5_lam_re, s5_lam_im, s5_log_dt, s5_b_re, s5_b_im, s5_c_re, s5_c_im, s5_d, s5_w_glu1, s5_w_glu2):
    bsz, seq, d = x.shape
    n_mem = mem.shape[1]
    depth = norm_gains.shape[0]
    assert seq % CHUNK == 0 and xa_wq.shape[2] == XA_HEADS * XA_HEAD_DIM

    mem_kv = _norm_matmul(mem.reshape(bsz * n_mem, d), mem_norm, mem_w_kv.astype(BF16), BF16)
    mem_kv = mem_kv.reshape(bsz, n_mem, mem_w_kv.shape[1])

    x = x.reshape(bsz * seq, d)
    hn = _norm(x, norm_gains[0, 0], BF16)
    for i in range(depth):
        kind, j = i % N_MIXERS, i // N_MIXERS
        g = norm_gains[i]
        if kind == 0:
            branch = _gla_layer(hn, g[1], gla_w_in, j, gla_w_gk_up[j], gla_b_gk_up[j],
                           gla_o_norm[j], gla_w_out[j], bsz, seq)
        elif kind == 1:
            branch = _rwkv_layer(hn, g[1], rwkv_mix[j], rwkv_w_r[j], rwkv_w_k[j], rwkv_w_v[j],
                            rwkv_w_o[j], rwkv_w0[j], rwkv_w_w1[j], rwkv_w_w2[j], rwkv_a0[j],
                            rwkv_w_a1[j], rwkv_w_a2[j], rwkv_w_g1[j], rwkv_w_g2[j], rwkv_k_k[j],
                            rwkv_k_a[j], rwkv_r_k[j], rwkv_lnx_g[j], rwkv_lnx_b[j], bsz, seq)
        else:
            branch = _s5_layer(hn, g[1], s5_lam_re[j], s5_lam_im[j], s5_log_dt[j], s5_b_re[j],
                          s5_b_im[j], s5_c_re[j], s5_c_im[j], s5_d[j], s5_w_glu1[j], s5_w_glu2[j],
                          bsz, seq)
        x, hn = _cross_attention(x, branch, g[2], xa_wq[i].astype(BF16), mem_kv, xa_wo[i].astype(BF16), g[3],
                                 g[4], seq)
        hidden = _matmul(hn, mlp_w1, BF16, act="relu2", layer=i)
        branch = _matmul_acc(hidden, mlp_w2, layer=i)
        x, hn = _add_norm(x, branch, g[5], norm_gains[i + 1, 0] if i + 1 < depth else None)
    return x.reshape(bsz, seq, d)
```

```python
import functools

import jax
import jax.numpy as jnp
from jax import lax
from jax.experimental import pallas as pl
from jax.experimental.pallas import tpu as pltpu

F32 = jnp.float32
BF16 = jnp.bfloat16

NORM_EPS = 1e-6
CHUNK = 64
N_MIXERS = 3
GLA_HEADS = 4
GLA_GATE_NORMALIZER = 16.0
RWKV_HEAD = 64
RWKV_LNX_EPS = 64e-5
XA_HEADS = 4
XA_HEAD_DIM = 128
S5_SLAB_GROUPS = 16
SUBLANES = 8
LANES = 128
VMEM_LIMIT = 60 * 1024 * 1024


def _cp(*sem):
    return pltpu.CompilerParams(dimension_semantics=sem, vmem_limit_bytes=VMEM_LIMIT)


def _rms(x, gain, eps=NORM_EPS):
    ms = jnp.mean(x * x, axis=-1, keepdims=True)
    return x * lax.rsqrt(ms + eps) * gain


def _stat_scratch(rows):
    return pltpu.VMEM((rows, LANES), F32)


def _for_row_blocks(n_rows, body, carry=None, rows=2 * SUBLANES, unroll=1):
    def step(i, c):
        return body(pl.ds(pl.multiple_of(i * rows, rows), rows), c)
    return lax.fori_loop(0, n_rows // rows, step, carry, unroll=unroll)


def _sumsq_lanes(x):
    acc = x[:, 0:LANES] * x[:, 0:LANES]
    for j in range(1, x.shape[1] // LANES):
        blk = x[:, j * LANES:(j + 1) * LANES]
        acc = acc + blk * blk
    return acc


def _finish_scales(s_ref, d):
    ms = jnp.sum(s_ref[...], axis=-1, keepdims=True) * (1.0 / d)
    s_ref[...] = jnp.broadcast_to(lax.rsqrt(ms + NORM_EPS), s_ref.shape)


def _rows_of(ref):
    return lambda rs: ref[rs, :]


def _row_scales_into(s_ref, load, d):
    def body(rs, c):
        s_ref[rs, :] = _sumsq_lanes(load(rs))
        return c
    _for_row_blocks(s_ref.shape[0], body, rows=SUBLANES, unroll=2)
    _finish_scales(s_ref, d)


GAIN_ROWS = 2 * SUBLANES


def _row_vector(v):
    return jnp.broadcast_to(v.reshape(1, -1), (GAIN_ROWS, v.shape[-1]))


def _vector_spec(d, index_map):
    return pl.BlockSpec((GAIN_ROWS, d), index_map)


def _scaled(x, scale, g_ref):
    return x * jnp.tile(scale, (1, x.shape[1] // LANES)) * g_ref[0:x.shape[0], :]


def _norm_rows_into(dst_ref, load, g_ref, s_ref):
    _row_scales_into(s_ref, load, dst_ref.shape[1])

    def body(rs, c):
        dst_ref[rs, :] = _scaled(load(rs), s_ref[rs, :], g_ref).astype(dst_ref.dtype)
        return c
    _for_row_blocks(dst_ref.shape[0], body, unroll=2)


def _add_norm_rows(o_ref, load_m, load_res, g_ref, s_ref, hn_ref=None, g_next_ref=None):
    _row_scales_into(s_ref, load_m, o_ref.shape[1])

    def body(rs, c):
        x_new = load_res(rs) + _scaled(load_m(rs), s_ref[rs, :], g_ref)
        o_ref[rs, :] = x_new
        if hn_ref is not None:
            s_ref[rs, :] = _sumsq_lanes(x_new)
        return c
    _for_row_blocks(o_ref.shape[0], body, rows=SUBLANES, unroll=2)
    if hn_ref is not None:
        _finish_scales(s_ref, o_ref.shape[1])

        def body2(rs, c):
            hn_ref[rs, :] = _scaled(o_ref[rs, :], s_ref[rs, :], g_next_ref).astype(hn_ref.dtype)
            return c
        _for_row_blocks(o_ref.shape[0], body2, unroll=2)


def _dot(a, b):
    return jnp.dot(a, b, preferred_element_type=F32)


def _dot_nt(a, b):
    return lax.dot_general(a, b, (((1,), (1,)), ((), ())), preferred_element_type=F32)


def _dot_tn(a, b):
    return lax.dot_general(a, b, (((0,), (0,)), ((), ())), preferred_element_type=F32)


def _split_bf16(x, pieces):
    out = []
    for _ in range(pieces - 1):
        out.append(x.astype(BF16))
        x = x - out[-1].astype(F32)
    return out + [x.astype(BF16)]


def _dot_ones(ones, x):
    hi, mid, lo = _split_bf16(x, 3)
    return _dot(ones, hi) + (_dot(ones, mid) + _dot(ones, lo))


def _dot_split(a, b):
    a_hi, a_lo = _split_bf16(a, 2)
    b_hi, b_lo = _split_bf16(b, 2)
    return _dot(a_hi, b_hi) + (_dot(a_hi, b_lo) + _dot(a_lo, b_hi))


def _sigmoid(x):
    return 1.0 / (1.0 + jnp.exp(-x))


def _softplus(x):
    return jnp.maximum(x, 0.0) + jnp.log1p(jnp.exp(-jnp.abs(x)))


def _norm_kernel(x_ref, g_ref, o_ref, s_ref):
    _norm_rows_into(o_ref, _rows_of(x_ref), g_ref, s_ref)


def _norm(x, gain, out_dtype=F32, tm=256):
    m, d = x.shape
    tm = min(tm, m)
    return pl.pallas_call(
        _norm_kernel, out_shape=jax.ShapeDtypeStruct((m, d), out_dtype),
        grid=(m // tm,),
        in_specs=[pl.BlockSpec((tm, d), lambda i: (i, 0)), _vector_spec(d, lambda i: (0, 0))],
        out_specs=pl.BlockSpec((tm, d), lambda i: (i, 0)),
        scratch_shapes=[_stat_scratch(tm)],
        compiler_params=_cp("parallel"), name="rmsnorm")(x, _row_vector(gain))


def _norm_matmul_kernel(x_ref, g_ref, w_ref, o_ref, hn_ref, s_ref):
    @pl.when(pl.program_id(1) == 0)
    def _():
        _norm_rows_into(hn_ref, _rows_of(x_ref), g_ref, s_ref)

    o_ref[...] = _dot(hn_ref[...], w_ref[...]).astype(o_ref.dtype)


def _norm_matmul(x, gain, w, out_dtype, tm=512, tn=512):
    m, d = x.shape
    n = w.shape[1]
    tm, tn = min(tm, m), min(tn, n)
    return pl.pallas_call(
        _norm_matmul_kernel, out_shape=jax.ShapeDtypeStruct((m, n), out_dtype),
        grid=(m // tm, n // tn),
        in_specs=[pl.BlockSpec((tm, d), lambda i, j: (i, 0)),
                  _vector_spec(d, lambda i, j: (0, 0)),
                  pl.BlockSpec((d, tn), lambda i, j: (0, j))],
        out_specs=pl.BlockSpec((tm, tn), lambda i, j: (i, j)),
        scratch_shapes=[pltpu.VMEM((tm, d), BF16), _stat_scratch(tm)],
        compiler_params=_cp("parallel", "arbitrary"), name="norm_matmul")(x, _row_vector(gain), w)


def _matmul_norm_kernel(*refs, glu, tn):
    if glu:
        a_ref, w_ref, w2_ref, g_ref, o_ref, acc_ref, s_ref = refs
    else:
        a_ref, w_ref, g_ref, o_ref, acc_ref, s_ref = refs
    j = pl.program_id(1)
    a = a_ref[...]
    y = _dot(a, w_ref[...])
    if glu:
        y = y * _sigmoid(_dot(a, w2_ref[...]))
    acc_ref[:, pl.ds(pl.multiple_of(j * tn, tn), tn)] = y

    @pl.when(j == pl.num_programs(1) - 1)
    def _():
        _norm_rows_into(o_ref, _rows_of(acc_ref), g_ref, s_ref)


def _matmul_norm(a, ws, gain, tm=1024):
    m, k = a.shape
    n = ws[0].shape[1]
    glu = len(ws) == 2
    tm, tn = min(tm, m), min(256 if glu else 512, n)
    w_specs = [pl.BlockSpec((k, tn), lambda i, j: (0, j)) for _ in ws]
    return pl.pallas_call(
        functools.partial(_matmul_norm_kernel, glu=glu, tn=tn),
        out_shape=jax.ShapeDtypeStruct((m, n), BF16),
        grid=(m // tm, n // tn),
        in_specs=[pl.BlockSpec((tm, k), lambda i, j: (i, 0))] + w_specs + [
            _vector_spec(n, lambda i, j: (0, 0))],
        out_specs=pl.BlockSpec((tm, n), lambda i, j: (i, 0)),
        scratch_shapes=[pltpu.VMEM((tm, n), F32), _stat_scratch(tm)],
        compiler_params=_cp("parallel", "arbitrary"),
        name="glu_norm" if glu else "matmul_norm")(a, *ws, _row_vector(gain))


def _xa_kernel(x_ref, br_ref, g_in_ref, wq_ref, k_ref, v_ref, wo_ref, g_out_ref, g_next_ref, o_ref, hn_out_ref,
               hn_ref, x1_ref, s_ref):
    def add_branch(rs):
        x1_ref[rs, :] = x_ref[rs, :] + br_ref[rs, :].astype(F32)
        return x1_ref[rs, :]

    _row_scales_into(s_ref, add_branch, x_ref.shape[1])
    x1 = _rows_of(x1_ref)

    def normed(rs, c):
        hn_ref[rs, :] = _scaled(x1(rs), s_ref[rs, :], g_in_ref).astype(hn_ref.dtype)
        return c
    _for_row_blocks(hn_ref.shape[0], normed, unroll=2)
    q = _dot(hn_ref[...], wq_ref[...]) * (XA_HEAD_DIM ** -0.5)
    heads = []
    for h in range(XA_HEADS):
        sl = slice(h * XA_HEAD_DIM, (h + 1) * XA_HEAD_DIM)
        s = _dot_nt(q[:, sl].astype(BF16), k_ref[0, :, sl])
        s = s - jnp.max(s, axis=-1, keepdims=True)
        p = jnp.exp(s)
        p = p / jnp.sum(p, axis=-1, keepdims=True)
        heads.append(_dot(p.astype(BF16), v_ref[0, :, sl]))
    o = jnp.concatenate(heads, axis=-1).astype(BF16)
    o_ref[...] = _dot(o, wo_ref[...])
    _add_norm_rows(o_ref, _rows_of(o_ref), x1, g_out_ref, s_ref, hn_out_ref, g_next_ref)


def _cross_attention(x, branch, g_in, wq, mem_kv, wo, g_out, g_next, seq, tm=256):
    m, d = x.shape
    xw = wq.shape[1]
    n_mem = mem_kv.shape[1]
    tm = min(tm, seq)
    tiles_per_seq = seq // tm
    return pl.pallas_call(
        _xa_kernel, out_shape=[jax.ShapeDtypeStruct((m, d), F32), jax.ShapeDtypeStruct((m, d), BF16)],
        grid=(m // tm,),
        in_specs=[pl.BlockSpec((tm, d), lambda i: (i, 0)),
                  pl.BlockSpec((tm, d), lambda i: (i, 0)),
                  _vector_spec(d, lambda i: (0, 0)),
                  pl.BlockSpec((d, xw), lambda i: (0, 0)),
                  pl.BlockSpec((1, n_mem, xw), lambda i: (i // tiles_per_seq, 0, 0)),
                  pl.BlockSpec((1, n_mem, xw), lambda i: (i // tiles_per_seq, 0, 1)),
                  pl.BlockSpec((xw, d), lambda i: (0, 0)),
                  _vector_spec(d, lambda i: (0, 0)),
                  _vector_spec(d, lambda i: (0, 0))],
        out_specs=[pl.BlockSpec((tm, d), lambda i: (i, 0)), pl.BlockSpec((tm, d), lambda i: (i, 0))],
        scratch_shapes=[pltpu.VMEM((tm, d), BF16), pltpu.VMEM((tm, d), F32), _stat_scratch(tm)],
        compiler_params=_cp("parallel"), name="cross_attention")(
            x, branch, _row_vector(g_in), wq, mem_kv, mem_kv, wo, _row_vector(g_out), _row_vector(g_next))


def _matmul_kernel(a_ref, w_ref, o_ref, *, act, col0, w_cols, w_is_transposed):
    w = w_ref[...]
    if w_cols is not None:
        tn = o_ref.shape[1]
        shape, axis = ((tn, 1), 0) if w_is_transposed else ((1, tn), 1)
        cols = col0 + pl.program_id(1) * tn + lax.broadcasted_iota(jnp.int32, shape, axis)
        w = jnp.where(cols < w_cols, w, 0.0)
    w = w.astype(BF16)
    y = _dot_nt(a_ref[...], w) if w_is_transposed else _dot(a_ref[...], w)
    if act == "relu2":
        y = jnp.maximum(y, 0.0)
        y = y * y
    o_ref[...] = y.astype(o_ref.dtype)


def _layer_weight_spec(w, layer, block, index_map):
    if w.ndim == 2:
        return pl.BlockSpec(block, index_map)
    return pl.BlockSpec((None,) + block, lambda *idx: (layer,) + index_map(*idx))


def _matmul(a, w, out_dtype, act=None, layer=None, col0=0, n_out=None, w_is_transposed=False, tm=2048, tn=512):
    m, k = a.shape
    w_cols = w.shape[-2] if w_is_transposed else w.shape[-1]
    n = w_cols if n_out is None else n_out
    tm, tn = min(tm, m), min(tn, n)
    assert col0 % tn == 0
    ragged = col0 + n > w_cols
    block, index_map = (((tn, k), lambda i, j: (col0 // tn + j, 0)) if w_is_transposed
                        else ((k, tn), lambda i, j: (0, col0 // tn + j)))
    return pl.pallas_call(
        functools.partial(_matmul_kernel, act=act, col0=col0, w_cols=w_cols if ragged else None,
                          w_is_transposed=w_is_transposed),
        out_shape=jax.ShapeDtypeStruct((m, n), out_dtype),
        grid=(m // tm, n // tn),
        in_specs=[pl.BlockSpec((tm, k), lambda i, j: (i, 0), pipeline_mode=pl.Buffered(1)),
                  _layer_weight_spec(w, layer, block, index_map)],
        out_specs=pl.BlockSpec((tm, tn), lambda i, j: (i, j)),
        compiler_params=_cp("parallel", "arbitrary"), name="matmul")(a, w)


def _matmul_acc_kernel(a_ref, w_ref, o_ref, *, ts):
    @pl.when(pl.program_id(2) == 0)
    def _():
        o_ref[...] = jnp.zeros_like(o_ref)

    a = a_ref[...]
    for n in range(o_ref.shape[1] // ts):
        sl = slice(n * ts, (n + 1) * ts)
        o_ref[:, sl] += _dot(a, w_ref[:, sl].astype(BF16))


def _matmul_acc(a, w, layer=None, tm=2048, tn=1024, tk=2048):
    m, k = a.shape
    n = w.shape[-1]
    tm, tn, tk = min(tm, m), min(tn, n), min(tk, k)
    return pl.pallas_call(
        functools.partial(_matmul_acc_kernel, ts=min(256, tn)),
        out_shape=jax.ShapeDtypeStruct((m, n), F32),
        grid=(m // tm, n // tn, k // tk),
        in_specs=[pl.BlockSpec((tm, tk), lambda i, j, kk: (i, kk)),
                  _layer_weight_spec(w, layer, (tk, tn), lambda i, j, kk: (kk, j))],
        out_specs=pl.BlockSpec((tm, tn), lambda i, j, kk: (i, j)),
        compiler_params=_cp("parallel", "parallel", "arbitrary"), name="matmul_acc")(a, w)


def _add_norm_kernel(x_ref, m_ref, gp_ref, gn_ref, o_ref, *rest):
    s_ref = rest[-1]
    hn_ref = rest[0] if len(rest) == 2 else None
    _add_norm_rows(o_ref, _rows_of(m_ref), _rows_of(x_ref), gp_ref, s_ref, hn_ref, gn_ref)


def _add_norm(x, m_branch, g_post, g_next, tm=256):
    m, d = x.shape
    tm = min(tm, m)
    row = pl.BlockSpec((tm, d), lambda i: (i, 0))
    vec = _vector_spec(d, lambda i: (0, 0))
    emit_hn = g_next is not None
    out_shape = [jax.ShapeDtypeStruct((m, d), F32)] + ([jax.ShapeDtypeStruct((m, d), BF16)] if emit_hn else [])
    outs = pl.pallas_call(
        _add_norm_kernel, out_shape=out_shape, grid=(m // tm,),
        in_specs=[row, row, vec, vec], out_specs=[row] * len(out_shape),
        scratch_shapes=[_stat_scratch(tm)],
        compiler_params=_cp("parallel"), name="add_norm")(
            x, m_branch, _row_vector(g_post), _row_vector(g_next if emit_hn else g_post))
    return (outs[0], outs[1]) if emit_hn else (outs[0], None)


def _gla_kernel(q_ref, k_ref, v_ref, g_ref, gk_ref, wup_ref, bup_ref, onorm_ref, tril_ref,
                o_ref, st_ref):
    @pl.when(pl.program_id(1) == 0)
    def _():
        st_ref[...] = jnp.zeros_like(st_ref)

    nb, _, dk = q_ref.shape
    bs = range(nb)
    gk = gk_ref[...].reshape(nb * CHUNK, gk_ref.shape[2])
    z = _dot_split(gk, wup_ref[...]) + bup_ref[...]
    log_alpha = (jnp.minimum(z, 0.0) - jnp.log1p(jnp.exp(-jnp.abs(z)))) / GLA_GATE_NORMALIZER
    cum_all = _dot_ones(tril_ref[...], log_alpha)
    cum = [cum_all[b * CHUNK:(b + 1) * CHUNK] for b in bs]
    cum_last = [cum[b][CHUNK - 1:CHUNK, :] for b in bs]
    k_dec = [(k_ref[b].astype(F32) * jnp.exp(cum_last[b] - cum[b])).astype(BF16) for b in bs]
    st = [st_ref[b] * jnp.exp(cum_last[b]) + _dot_tn(v_ref[b], k_dec[b]) for b in bs]
    for b in bs:
        st_ref[b] = st[b]
    q = [(q_ref[b].astype(F32) * dk ** -0.5).astype(BF16) for b in bs]
    o = [_dot_nt(q[b], st[b].astype(BF16)) for b in bs]
    for b in bs:
        g = g_ref[b].astype(F32)
        o_ref[b] = (_rms(o[b], onorm_ref[...]) * (g * _sigmoid(g))).astype(o_ref.dtype)


def _gla_scan(proj, gk_lo, w_up, b_up, o_norm, bsz, seq):
    dk_all = w_up.shape[1]
    dkh = dk_all // GLA_HEADS
    dv_all = (proj.shape[1] - 2 * dk_all) // 2
    dvh = dv_all // GLA_HEADS
    nc = seq // CHUNK
    rank_pad = gk_lo.shape[1]
    proj = proj.reshape(bsz, seq, proj.shape[1])
    gk_lo = gk_lo.reshape(bsz, seq, rank_pad)
    tril = jnp.kron(jnp.eye(bsz, dtype=F32), jnp.tril(jnp.ones((CHUNK, CHUNK), F32))).astype(BF16)
    k_off = dk_all // dkh
    v_off = 2 * dk_all // dvh
    g_off = v_off + dv_all // dvh
    out = pl.pallas_call(
        _gla_kernel, out_shape=jax.ShapeDtypeStruct((bsz, seq, dv_all), BF16),
        grid=(GLA_HEADS, nc),
        in_specs=[pl.BlockSpec((bsz, CHUNK, dkh), lambda h, c: (0, c, h)),
                  pl.BlockSpec((bsz, CHUNK, dkh), lambda h, c: (0, c, k_off + h)),
                  pl.BlockSpec((bsz, CHUNK, dvh), lambda h, c: (0, c, v_off + h)),
                  pl.BlockSpec((bsz, CHUNK, dvh), lambda h, c: (0, c, g_off + h)),
                  pl.BlockSpec((bsz, CHUNK, rank_pad), lambda h, c: (0, c, 0)),
                  pl.BlockSpec((rank_pad, dkh), lambda h, c: (0, h)),
                  pl.BlockSpec((1, dkh), lambda h, c: (0, h)),
                  pl.BlockSpec((1, dvh), lambda h, c: (0, 0)),
                  pl.BlockSpec((bsz * CHUNK, bsz * CHUNK), lambda h, c: (0, 0))],
        out_specs=pl.BlockSpec((bsz, CHUNK, dvh), lambda h, c: (0, c, h)),
        scratch_shapes=[pltpu.VMEM((bsz, dvh, dkh), F32)],
        compiler_params=_cp("parallel", "arbitrary"), name="gla_scan")(
            proj, proj, proj, proj, gk_lo, w_up, b_up.reshape(1, dk_all), o_norm.reshape(1, dvh), tril)
    return out.reshape(bsz * seq, dv_all)


def _shiftmix_matmul_kernel(hn_ref, prev_ref, mix_ref, w_ref, o_ref, xm_ref, *, tiles_per_seq):
    @pl.when(pl.program_id(2) == 0)
    def _():
        prev_rows = prev_ref.shape[0]
        last = prev_ref[...].astype(F32)[prev_rows - 1:prev_rows, :]
        first_tile = pl.program_id(0) % tiles_per_seq == 0
        last = jnp.where(first_tile, 0.0, last)

        def body(rs, last):
            hn = hn_ref[rs, :].astype(F32)
            rows = lax.broadcasted_iota(jnp.int32, hn.shape, 0)
            shifted = jnp.where(rows == 0, last, pltpu.roll(hn, 1, axis=0))
            xm_ref[rs, :] = (hn + (shifted - hn) * mix_ref[0]).astype(BF16)
            return hn[hn.shape[0] - 1:, :]

        _for_row_blocks(hn_ref.shape[0], body, last)

    o_ref[0] = _dot(xm_ref[...], w_ref[0]).astype(o_ref.dtype)


def _shiftmix_matmul(hn, mix, w, out_dtype, seq, tm=2048, tn=512):
    m, d = hn.shape
    p_cnt, _, n = w.shape
    tm, tn = min(tm, seq), min(tn, n)
    prev_rows = 2 * SUBLANES
    blk = tm // prev_rows
    return pl.pallas_call(
        functools.partial(_shiftmix_matmul_kernel, tiles_per_seq=seq // tm),
        out_shape=jax.ShapeDtypeStruct((p_cnt, m, n), out_dtype),
        grid=(m // tm, p_cnt, n // tn),
        in_specs=[pl.BlockSpec((tm, d), lambda i, p, j: (i, 0), pipeline_mode=pl.Buffered(1)),
                  pl.BlockSpec((prev_rows, d), lambda i, p, j: (jnp.maximum(i * blk - 1, 0), 0)),
                  pl.BlockSpec((1, GAIN_ROWS, d), lambda i, p, j: (p, 0, 0)),
                  pl.BlockSpec((1, d, tn), lambda i, p, j: (p, 0, j))],
        out_specs=pl.BlockSpec((1, tm, tn), lambda i, p, j: (p, i, j)),
        scratch_shapes=[pltpu.VMEM((tm, d), BF16)],
        compiler_params=_cp("parallel", "arbitrary", "arbitrary"), name="shiftmix_matmul")(
            hn, hn, jnp.broadcast_to(mix.reshape(p_cnt, 1, d), (p_cnt, GAIN_ROWS, d)), w)


def _lowrank_out_kernel(h_ref, w_ref, b_ref, o_ref):
    p = pl.program_id(0)
    h = h_ref[0]
    act = jnp.where(p == 0, jnp.tanh(h), jnp.where(p == 1, h, _sigmoid(h)))
    o_ref[0] = _dot(act.astype(BF16), w_ref[0]) + b_ref[0]


def _lowrank_out(h, w2, bias, tm=512):
    p_cnt, m, r = h.shape
    d = w2.shape[2]
    tm = min(tm, m)
    return pl.pallas_call(
        _lowrank_out_kernel, out_shape=jax.ShapeDtypeStruct((p_cnt, m, d), F32),
        grid=(p_cnt, m // tm),
        in_specs=[pl.BlockSpec((1, tm, r), lambda p, i: (p, i, 0)),
                  pl.BlockSpec((1, r, d), lambda p, i: (p, 0, 0)),
                  pl.BlockSpec((1, 1, d), lambda p, i: (p, 0, 0))],
        out_specs=pl.BlockSpec((1, tm, d), lambda p, i: (p, i, 0)),
        compiler_params=_cp("arbitrary", "arbitrary"), name="lowrank_out")(h, w2, bias)


def _rwkv_kernel(rkv_ref, lr_ref, par_ref, bd_ref, tril_ref, o_ref, st_ref):
    @pl.when(pl.program_id(2) == 0)
    def _():
        st_ref[...] = jnp.zeros_like(st_ref)

    n = RWKV_HEAD
    heads = 2 * st_ref.shape[0]
    r = rkv_ref[0].astype(F32)
    k = rkv_ref[1].astype(F32)
    v = rkv_ref[2].astype(F32)
    w_log = -_softplus(-lr_ref[0]) - 0.5
    log_w = -jnp.exp(w_log)
    a = _sigmoid(lr_ref[1])
    gate = lr_ref[2]
    k_k, k_a, r_k = par_ref[0:1, :], par_ref[1:2, :], par_ref[2:3, :]
    lnx_g, lnx_b = par_ref[3:4, :], par_ref[4:5, :]
    bd = bd_ref[...]
    grp = bd.shape[0]

    def head_sums(x):
        x = x.astype(BF16)
        return jnp.concatenate([_dot(x[:, i:i + grp], bd) for i in range(0, x.shape[1], grp)], axis=1)

    kk = k * k_k
    kk = kk / jnp.maximum(jnp.sqrt(head_sums(kk * kk)), 1e-12)
    k2 = k * (1.0 + (a - 1.0) * k_a)
    cw = _dot_ones(tril_ref[...], log_w)
    cw_last = cw[CHUNK - 1:CHUNK, :]
    e_neg = jnp.exp(-cw)
    e_end = jnp.exp(cw_last - cw)
    a_t = (-kk * jnp.exp(cw - log_w)).astype(BF16)
    r_t = (r * jnp.exp(cw)).astype(BF16)
    b_vec = kk * a
    b_t = (b_vec * e_neg).astype(BF16)
    k_t = (k2 * e_neg).astype(BF16)
    b_w = (b_vec * e_end).astype(BF16)
    k_w = (k2 * e_end).astype(BF16)
    w_end = jnp.exp(cw_last)
    v_b = v.astype(BF16)

    pw = 2 * n
    rows = lax.broadcasted_iota(jnp.int32, (CHUNK, 2 * pw), 0)
    cols = lax.broadcasted_iota(jnp.int32, (CHUNK, 2 * pw), 1) % CHUNK
    strict = rows > cols
    incl = rows >= cols
    t_row = lax.broadcasted_iota(jnp.int32, (CHUNK, pw), 0)
    t_col = lax.broadcasted_iota(jnp.int32, (CHUNK, pw), 1) % CHUNK
    eye = (t_row == t_col).astype(F32)

    def pair_diag(x):
        first = lax.broadcasted_iota(jnp.int32, x.shape, 1) < n
        zero = jnp.zeros_like(x)
        return jnp.concatenate([jnp.where(first, x, zero), jnp.where(first, zero, x)], axis=0)

    def lower_left(b):
        return (t_row // (2 * b) == t_col // (2 * b)) & (t_row % (2 * b) >= b) & (t_col % (2 * b) < b)

    ps = range(heads // 2)
    sls = [slice(p * pw, (p + 1) * pw) for p in ps]
    ar = [jnp.concatenate([a_t[:, sl], r_t[:, sl]], axis=0) for sl in sls]
    bk = [jnp.concatenate([pair_diag(b_t[:, sl]), pair_diag(k_t[:, sl])], axis=0) for sl in sls]
    s0 = [st_ref[p] for p in ps]
    gram = [_dot_nt(ar[p], bk[p]) for p in ps]
    proj = [_dot_nt(ar[p], s0[p].astype(BF16)) for p in ps]
    v_d = [pair_diag(v_b[:, sl]) for sl in sls]
    low = [jnp.where(strict, gram[p][:CHUNK], 0.0).astype(BF16) for p in ps]
    nil = [low[p][:, :pw] for p in ps]
    t_inv = [eye + jnp.where(lower_left(1), nil[p], 0).astype(F32) for p in ps]
    b = 2
    while b < CHUNK:
        mask = lower_left(b)
        off = [jnp.where(mask, nil[p], 0) for p in ps]
        t_b = [t_inv[p].astype(BF16) for p in ps]
        right = [_dot(off[p], pair_diag(t_b[p])).astype(BF16) for p in ps]
        t_inv = [t_inv[p] + _dot(t_b[p], pair_diag(right[p])) for p in ps]
        b *= 2
    rhs = [proj[p][:CHUNK] + _dot(low[p][:, pw:], v_d[p]) for p in ps]
    u = [_dot(t_inv[p].astype(BF16), pair_diag(rhs[p].astype(BF16))).astype(BF16) for p in ps]
    upper = [jnp.where(incl, gram[p][CHUNK:], 0.0).astype(BF16) for p in ps]
    y_pairs = [proj[p][CHUNK:] + _dot(upper[p], jnp.concatenate([pair_diag(u[p]), v_d[p]], axis=0)) for p in ps]
    pair_rows = lax.broadcasted_iota(jnp.int32, (pw, pw), 0) // n
    pair_cols = lax.broadcasted_iota(jnp.int32, (pw, pw), 1) // n
    for p in ps:
        uv = jnp.concatenate([u[p], v_b[:, sls[p]]], axis=0)
        bkw = jnp.concatenate([b_w[:, sls[p]], k_w[:, sls[p]]], axis=0)
        grown = s0[p] * w_end[:, sls[p]] + _dot_tn(uv, bkw)
        st_ref[p] = jnp.where(pair_rows == pair_cols, grown, 0.0)

    y = jnp.concatenate(y_pairs, axis=1)
    inv_n = 1.0 / n
    mu = head_sums(y) * inv_n
    yc = y - mu
    var = head_sums(yc * yc) * inv_n
    y = yc * lax.rsqrt(var + RWKV_LNX_EPS) * lnx_g + lnx_b
    bonus = head_sums(r * k2 * r_k) * v
    o_ref[...] = ((y + bonus) * gate).astype(o_ref.dtype)


def _rwkv_scan(rkv, lr, params, bsz, seq, heads_per_step=32):
    _, m, d = rkv.shape
    n_heads = d // RWKV_HEAD
    hg = min(heads_per_step, n_heads)
    w = hg * RWKV_HEAD
    nc = seq // CHUNK
    grp = min(2 * LANES, w)
    lane_head = jnp.arange(grp) // RWKV_HEAD
    bd = (lane_head[:, None] == lane_head[None, :]).astype(BF16)
    tril = jnp.tril(jnp.ones((CHUNK, CHUNK), BF16))
    n_par = params.shape[0]
    return pl.pallas_call(
        _rwkv_kernel, out_shape=jax.ShapeDtypeStruct((m, d), BF16),
        grid=(bsz, n_heads // hg, nc),
        in_specs=[pl.BlockSpec((3, CHUNK, w), lambda b, g, c: (0, b * nc + c, g)),
                  pl.BlockSpec((3, CHUNK, w), lambda b, g, c: (0, b * nc + c, g)),
                  pl.BlockSpec((n_par, w), lambda b, g, c: (0, g)),
                  pl.BlockSpec((grp, grp), lambda b, g, c: (0, 0)),
                  pl.BlockSpec((CHUNK, CHUNK), lambda b, g, c: (0, 0))],
        out_specs=pl.BlockSpec((CHUNK, w), lambda b, g, c: (b * nc + c, g)),
        scratch_shapes=[pltpu.VMEM((hg // 2, 2 * RWKV_HEAD, 2 * RWKV_HEAD), F32)],
        compiler_params=_cp("parallel", "parallel", "arbitrary"), name="rwkv_scan")(
            rkv, lr, params, bd, tril)


def _cmul_add(acc_r, acc_i, cr, ci, xr, xi):
    return acc_r + cr * xr - ci * xi, acc_i + cr * xi + ci * xr


def _s5_kernel(h_ref, wb_ref, wc_ref, coef_ref, pow_ref, d_ref, o_ref, x_ref, carry_ref):
    @pl.when(pl.program_id(2) == 0)
    def _():
        carry_ref[...] = jnp.zeros_like(carry_ref)

    tt = h_ref.shape[0]
    ns = carry_ref.shape[2]
    seg = tt // SUBLANES
    h32 = h_ref[...].astype(F32).reshape(SUBLANES, seg, h_ref.shape[1])
    h = jnp.swapaxes(h32, 0, 1).reshape(tt, h_ref.shape[1]).astype(BF16)
    x_ref[...] = _dot(h, wb_ref[0])
    a_r, a_i = coef_ref[0, 8], coef_ref[0, 9]

    def step_rows(k, c_r, c_i, x_r, x_i):
        rows = pl.ds(pl.multiple_of(k * SUBLANES, SUBLANES), SUBLANES)
        xr, xi = _cmul_add(x_ref[rows, 0:ns], x_ref[rows, ns:2 * ns], c_r, c_i, x_r, x_i)
        x_ref[rows, 0:ns] = xr
        x_ref[rows, ns:2 * ns] = xi
        return xr, xi

    zeros = jnp.zeros((SUBLANES, ns), F32)
    end_r, end_i = lax.fori_loop(0, seg, lambda k, c: step_rows(k, a_r, a_i, *c), (zeros, zeros))

    for s_idx, shift in enumerate((1, 2, 4)):
        end_r, end_i = _cmul_add(end_r, end_i, coef_ref[0, 2 * s_idx], coef_ref[0, 2 * s_idx + 1],
                                 pltpu.roll(end_r, shift, axis=0), pltpu.roll(end_i, shift, axis=0))
    in_r = jnp.broadcast_to(carry_ref[0], (SUBLANES, ns))
    in_i = jnp.broadcast_to(carry_ref[1], (SUBLANES, ns))
    end_r, end_i = _cmul_add(end_r, end_i, coef_ref[0, 6], coef_ref[0, 7], in_r, in_i)
    carry_ref[0] = end_r[SUBLANES - 1:SUBLANES, :]
    carry_ref[1] = end_i[SUBLANES - 1:SUBLANES, :]
    first = lax.broadcasted_iota(jnp.int32, (SUBLANES, ns), 0) == 0
    in_r = jnp.where(first, in_r, pltpu.roll(end_r, 1, axis=0))
    in_i = jnp.where(first, in_i, pltpu.roll(end_i, 1, axis=0))

    def correct(k, c):
        p_r = jnp.broadcast_to(pow_ref[0, 0, pl.ds(k, 1), :], (SUBLANES, ns))
        p_i = jnp.broadcast_to(pow_ref[0, 1, pl.ds(k, 1), :], (SUBLANES, ns))
        step_rows(k, p_r, p_i, in_r, in_i)
        return c

    lax.fori_loop(0, seg, correct, 0, unroll=2)

    y = _dot(x_ref[...].astype(BF16), wc_ref[0]) + d_ref[...] * h.astype(F32)
    y = 0.5 * y * (1.0 + lax.erf(y * (2.0 ** -0.5)))
    y = jnp.swapaxes(y.reshape(seg, SUBLANES, y.shape[1]), 0, 1)
    o_ref[...] = y.reshape(tt, o_ref.shape[1]).astype(o_ref.dtype)


def _s5_scan(hn, lam_re, lam_im, log_dt, b_re, b_im, c_re, c_im, d_skip, bsz, seq, tt=1024):
    m, d = hn.shape
    n_groups, n_state, grp = b_re.shape
    sg = min(S5_SLAB_GROUPS, n_groups)
    slabs = n_groups // sg
    ch = sg * grp
    ns = sg * n_state
    tt = min(tt, seq)

    lr = jnp.minimum(lam_re.astype(F32), -1e-4)
    li = lam_im.astype(F32)
    dt = jnp.exp(log_dt.astype(F32))[:, None]
    mag = jnp.exp(lr * dt)
    ab_re = mag * jnp.cos(li * dt)
    ab_im = mag * jnp.sin(li * dt)
    den = lr * lr + li * li
    nr, ni = ab_re - 1.0, ab_im
    z_re = (nr * lr + ni * li) / den
    z_im = (ni * lr - nr * li) / den
    bb_re = z_re[..., None] * b_re - z_im[..., None] * b_im
    bb_im = z_re[..., None] * b_im + z_im[..., None] * b_re

    eye = jnp.eye(sg, dtype=F32)

    def block_diag_in(bb):
        t = bb.reshape(slabs, sg, n_state, grp)
        return jnp.einsum('sgph,gk->sghkp', t, eye).reshape(slabs, ch, ns)

    def block_diag_out(cc):
        t = cc.reshape(slabs, sg, grp, n_state)
        return jnp.einsum('sgqp,gk->sgpkq', t, eye).reshape(slabs, ns, ch)

    w_b = jnp.concatenate([block_diag_in(bb_re), block_diag_in(bb_im)], axis=2).astype(BF16)
    w_c = jnp.concatenate([block_diag_out(c_re.astype(F32)), -block_diag_out(c_im.astype(F32))],
                          axis=1).astype(BF16)

    def powers(base_re, base_im, n):
        re, im = base_re[None], base_im[None]
        while re.shape[0] < n:
            top_re, top_im = re[-1:], im[-1:]
            re, im = (jnp.concatenate([re, re * top_re - im * top_im]),
                      jnp.concatenate([im, re * top_im + im * top_re]))
        return re[:n].reshape(n, slabs, ns), im[:n].reshape(n, slabs, ns)

    seg = tt // SUBLANES
    pw_re, pw_im = powers(ab_re, ab_im, seg)
    sg_re, sg_im = powers(pw_re[-1], pw_im[-1], SUBLANES)
    row = jnp.arange(SUBLANES)
    coefs = []
    for shift in (1, 2, 4):
        mask = (row >= shift)[:, None, None]
        coefs += [jnp.where(mask, sg_re[shift - 1][None], 0.0), jnp.where(mask, sg_im[shift - 1][None], 0.0)]
    coefs += [sg_re, sg_im]
    coefs += [jnp.broadcast_to(pw_re[:1], sg_re.shape), jnp.broadcast_to(pw_im[:1], sg_im.shape)]
    coef = jnp.stack(coefs, axis=0).transpose(2, 0, 1, 3)
    pw = jnp.stack([pw_re, pw_im], axis=0).transpose(2, 0, 1, 3)

    nt = seq // tt
    return pl.pallas_call(
        _s5_kernel, out_shape=jax.ShapeDtypeStruct((m, d), BF16),
        grid=(bsz, slabs, nt),
        in_specs=[pl.BlockSpec((tt, ch), lambda b, s, t: (b * nt + t, s)),
                  pl.BlockSpec((1, ch, 2 * ns), lambda b, s, t: (s, 0, 0)),
                  pl.BlockSpec((1, 2 * ns, ch), lambda b, s, t: (s, 0, 0)),
                  pl.BlockSpec((1, 10, SUBLANES, ns), lambda b, s, t: (s, 0, 0, 0)),
                  pl.BlockSpec((1, 2, seg, ns), lambda b, s, t: (s, 0, 0, 0)),
                  pl.BlockSpec((1, ch), lambda b, s, t: (0, s))],
        out_specs=pl.BlockSpec((tt, ch), lambda b, s, t: (b * nt + t, s)),
        scratch_shapes=[pltpu.VMEM((tt, 2 * ns), F32), pltpu.VMEM((2, 1, ns), F32)],
        compiler_params=_cp("parallel", "parallel", "arbitrary"), name="s5_scan")(
            hn, w_b, w_c, coef, pw, d_skip.reshape(1, d))


def _pad_to(x, axis, size):
    pad = size - x.shape[axis]
    if pad == 0:
        return x
    widths = [(0, 0)] * x.ndim
    widths[axis] = (0, pad)
    return jnp.pad(x, widths)


def _round_up(n, k):
    return (n + k - 1) // k * k


def _gla_layer(hn, g_post, w_in_all, layer, w_gk_up, b_gk_up, o_norm, w_out, bsz, seq):
    rank = w_gk_up.shape[0]
    n_main = w_in_all.shape[2] - rank
    rank_pad = _round_up(rank, LANES)
    w_in_t = jnp.swapaxes(w_in_all, 1, 2)
    proj = _matmul(hn, w_in_t, BF16, layer=layer, n_out=n_main, w_is_transposed=True)
    gk_lo = _matmul(hn, w_in_t, F32, layer=layer, col0=n_main, n_out=rank_pad, w_is_transposed=True)
    o = _gla_scan(proj, gk_lo, _pad_to(w_gk_up, 0, rank_pad), b_gk_up, o_norm, bsz, seq)
    return _matmul_norm(o, [w_out.astype(BF16)], g_post)


def _rwkv_layer(hn, g_post, mix, w_r, w_k, w_v, w_o, w0, w_w1, w_w2, a0, w_a1, w_a2,
                w_g1, w_g2, k_k, k_a, r_k, lnx_g, lnx_b, bsz, seq):
    d = hn.shape[1]
    rkv = _shiftmix_matmul(hn, jnp.stack([mix[0], mix[2], mix[3]]),
                           jnp.stack([w_r, w_k, w_v]).astype(BF16), BF16, seq)
    rank_pad = _round_up(max(w_w1.shape[1], w_a1.shape[1], w_g1.shape[1]), LANES)
    w1 = jnp.stack([_pad_to(w, 1, rank_pad) for w in (w_w1, w_a1, w_g1)]).astype(BF16)
    w2 = jnp.stack([_pad_to(w, 0, rank_pad) for w in (w_w2, w_a2, w_g2)]).astype(BF16)
    low = _shiftmix_matmul(hn, jnp.stack([mix[1], mix[4], mix[5]]), w1, F32, seq)
    bias = jnp.stack([w0, a0, jnp.zeros_like(w0)]).reshape(3, 1, d)
    lr = _lowrank_out(low, w2, bias)
    params = _pad_to(jnp.stack([k_k, k_a, r_k, lnx_g, lnx_b]), 0, SUBLANES)
    y = _rwkv_scan(rkv, lr, params, bsz, seq)
    return _matmul_norm(y, [w_o.astype(BF16)], g_post)


def _s5_layer(hn, g_post, lam_re, lam_im, log_dt, b_re, b_im, c_re, c_im, d_skip,
              w_glu1, w_glu2, bsz, seq):
    y = _s5_scan(hn, lam_re, lam_im, log_dt, b_re, b_im, c_re, c_im, d_skip, bsz, seq)
    return _matmul_norm(y, [w_glu1.astype(BF16), w_glu2.astype(BF16)], g_post)


def kernel(x, mem, norm_gains, mem_norm, mem_w_kv, xa_wq, xa_wo, mlp_w1, mlp_w2, gla_w_in, gla_w_gk_up, gla_b_gk_up, gla_o_norm, gla_w_out, rwkv_mix, rwkv_w_r, rwkv_w_k, rwkv_w_v, rwkv_w_o, rwkv_w0, rwkv_w_w1, rwkv_w_w2, rwkv_a0, rwkv_w_a1, rwkv_w_a2, rwkv_w_g1, rwkv_w_g2, rwkv_k_k, rwkv_k_a, rwkv_r_k, rwkv_lnx_g, rwkv_lnx_b, s5_lam_re, s5_lam_im, s5_log_dt, s5_b_re, s5_b_im, s5_c_re, s5_c_im, s5_d, s5_w_glu1, s5_w_glu2):
    bsz, seq, d = x.shape
    n_mem = mem.shape[1]
    depth = norm_gains.shape[0]
    assert seq % CHUNK == 0 and xa_wq.shape[2] == XA_HEADS * XA_HEAD_DIM

    mem_kv = _norm_matmul(mem.reshape(bsz * n_mem, d), mem_norm, mem_w_kv.astype(BF16), BF16)
    mem_kv = mem_kv.reshape(bsz, n_mem, mem_w_kv.shape[1])

    x = x.reshape(bsz * seq, d)
    hn = _norm(x, norm_gains[0, 0], BF16)
    for i in range(depth):
        kind, j = i % N_MIXERS, i // N_MIXERS
        g = norm_gains[i]
        if kind == 0:
            branch = _gla_layer(hn, g[1], gla_w_in, j, gla_w_gk_up[j], gla_b_gk_up[j],
                           gla_o_norm[j], gla_w_out[j], bsz, seq)
        elif kind == 1:
            branch = _rwkv_layer(hn, g[1], rwkv_mix[j], rwkv_w_r[j], rwkv_w_k[j], rwkv_w_v[j],
                            rwkv_w_o[j], rwkv_w0[j], rwkv_w_w1[j], rwkv_w_w2[j], rwkv_a0[j],
                            rwkv_w_a1[j], rwkv_w_a2[j], rwkv_w_g1[j], rwkv_w_g2[j], rwkv_k_k[j],
                            rwkv_k_a[j], rwkv_r_k[j], rwkv_lnx_g[j], rwkv_lnx_b[j], bsz, seq)
        else:
            branch = _s5_layer(hn, g[1], s5_lam_re[j], s5_lam_im[j], s5_log_dt[j], s5_b_re[j],
                          s5_b_im[j], s5_c_re[j], s5_c_im[j], s5_d[j], s5_w_glu1[j], s5_w_glu2[j],
                          bsz, seq)
        x, hn = _cross_attention(x, branch, g[2], xa_wq[i].astype(BF16), mem_kv, xa_wo[i].astype(BF16), g[3],
                                 g[4], seq)
        hidden = _matmul(hn, mlp_w1, BF16, act="relu2", layer=i)
        branch = _matmul_acc(hidden, mlp_w2, layer=i)
        x, hn = _add_norm(x, branch, g[5], norm_gains[i + 1, 0] if i + 1 < depth else None)
    return x.reshape(bsz, seq, d)
```

```python
import functools

import jax
import jax.numpy as jnp
from jax import lax
from jax.experimental import pallas as pl
from jax.experimental.pallas import tpu as pltpu

F32 = jnp.float32
BF16 = jnp.bfloat16

NORM_EPS = 1e-6
CHUNK = 64
N_MIXERS = 3
GLA_HEADS = 4
GLA_GATE_NORMALIZER = 16.0
RWKV_HEAD = 64
RWKV_LNX_EPS = 64e-5
XA_HEADS = 4
XA_HEAD_DIM = 128
S5_SLAB_GROUPS = 16
SUBLANES = 8
LANES = 128
VMEM_LIMIT = 60 * 1024 * 1024


def _cp(*sem):
    return pltpu.CompilerParams(dimension_semantics=sem, vmem_limit_bytes=VMEM_LIMIT)


def _rms(x, gain, eps=NORM_EPS):
    ms = jnp.mean(x * x, axis=-1, keepdims=True)
    return x * lax.rsqrt(ms + eps) * gain


def _stat_scratch(rows):
    return pltpu.VMEM((rows, LANES), F32)


def _for_row_blocks(n_rows, body, carry=None, rows=2 * SUBLANES, unroll=1):
    def step(i, c):
        return body(pl.ds(pl.multiple_of(i * rows, rows), rows), c)
    return lax.fori_loop(0, n_rows // rows, step, carry, unroll=unroll)


def _sumsq_lanes(x):
    acc = x[:, 0:LANES] * x[:, 0:LANES]
    for j in range(1, x.shape[1] // LANES):
        blk = x[:, j * LANES:(j + 1) * LANES]
        acc = acc + blk * blk
    return acc


def _finish_scales(s_ref, d):
    ms = jnp.sum(s_ref[...], axis=-1, keepdims=True) * (1.0 / d)
    s_ref[...] = jnp.broadcast_to(lax.rsqrt(ms + NORM_EPS), s_ref.shape)


def _rows_of(ref):
    return lambda rs: ref[rs, :]


def _row_scales_into(s_ref, load, d):
    def body(rs, c):
        s_ref[rs, :] = _sumsq_lanes(load(rs))
        return c
    _for_row_blocks(s_ref.shape[0], body, rows=SUBLANES, unroll=2)
    _finish_scales(s_ref, d)


GAIN_ROWS = 2 * SUBLANES


def _row_vector(v):
    return jnp.broadcast_to(v.reshape(1, -1), (GAIN_ROWS, v.shape[-1]))


def _vector_spec(d, index_map):
    return pl.BlockSpec((GAIN_ROWS, d), index_map)


def _scaled(x, scale, g_ref):
    return x * jnp.tile(scale, (1, x.shape[1] // LANES)) * g_ref[0:x.shape[0], :]


def _norm_rows_into(dst_ref, load, g_ref, s_ref):
    _row_scales_into(s_ref, load, dst_ref.shape[1])

    def body(rs, c):
        dst_ref[rs, :] = _scaled(load(rs), s_ref[rs, :], g_ref).astype(dst_ref.dtype)
        return c
    _for_row_blocks(dst_ref.shape[0], body, unroll=2)


def _add_norm_rows(o_ref, load_m, load_res, g_ref, s_ref, hn_ref=None, g_next_ref=None):
    _row_scales_into(s_ref, load_m, o_ref.shape[1])

    def body(rs, c):
        x_new = load_res(rs) + _scaled(load_m(rs), s_ref[rs, :], g_ref)
        o_ref[rs, :] = x_new
        if hn_ref is not None:
            s_ref[rs, :] = _sumsq_lanes(x_new)
        return c
    _for_row_blocks(o_ref.shape[0], body, rows=SUBLANES, unroll=2)
    if hn_ref is not None:
        _finish_scales(s_ref, o_ref.shape[1])

        def body2(rs, c):
            hn_ref[rs, :] = _scaled(o_ref[rs, :], s_ref[rs, :], g_next_ref).astype(hn_ref.dtype)
            return c
        _for_row_blocks(o_ref.shape[0], body2, unroll=2)


def _dot(a, b):
    return jnp.dot(a, b, preferred_element_type=F32)


def _dot_nt(a, b):
    return lax.dot_general(a, b, (((1,), (1,)), ((), ())), preferred_element_type=F32)


def _dot_tn(a, b):
    return lax.dot_general(a, b, (((0,), (0,)), ((), ())), preferred_element_type=F32)


def _split_bf16(x, pieces):
    out = []
    for _ in range(pieces - 1):
        out.append(x.astype(BF16))
        x = x - out[-1].astype(F32)
    return out + [x.astype(BF16)]


def _dot_ones(ones, x):
    hi, mid, lo = _split_bf16(x, 3)
    return _dot(ones, hi) + (_dot(ones, mid) + _dot(ones, lo))


def _dot_split(a, b):
    a_hi, a_lo = _split_bf16(a, 2)
    b_hi, b_lo = _split_bf16(b, 2)
    return _dot(a_hi, b_hi) + (_dot(a_hi, b_lo) + _dot(a_lo, b_hi))


def _sigmoid(x):
    return 1.0 / (1.0 + jnp.exp(-x))


def _softplus(x):
    return jnp.maximum(x, 0.0) + jnp.log1p(jnp.exp(-jnp.abs(x)))


def _norm_kernel(x_ref, g_ref, o_ref, s_ref):
    _norm_rows_into(o_ref, _rows_of(x_ref), g_ref, s_ref)


def _norm(x, gain, out_dtype=F32, tm=256):
    m, d = x.shape
    tm = min(tm, m)
    return pl.pallas_call(
        _norm_kernel, out_shape=jax.ShapeDtypeStruct((m, d), out_dtype),
        grid=(m // tm,),
        in_specs=[pl.BlockSpec((tm, d), lambda i: (i, 0)), _vector_spec(d, lambda i: (0, 0))],
        out_specs=pl.BlockSpec((tm, d), lambda i: (i, 0)),
        scratch_shapes=[_stat_scratch(tm)],
        compiler_params=_cp("parallel"), name="rmsnorm")(x, _row_vector(gain))


def _norm_matmul_kernel(x_ref, g_ref, w_ref, o_ref, hn_ref, s_ref):
    @pl.when(pl.program_id(1) == 0)
    def _():
        _norm_rows_into(hn_ref, _rows_of(x_ref), g_ref, s_ref)

    o_ref[...] = _dot(hn_ref[...], w_ref[...]).astype(o_ref.dtype)


def _norm_matmul(x, gain, w, out_dtype, tm=512, tn=512):
    m, d = x.shape
    n = w.shape[1]
    tm, tn = min(tm, m), min(tn, n)
    return pl.pallas_call(
        _norm_matmul_kernel, out_shape=jax.ShapeDtypeStruct((m, n), out_dtype),
        grid=(m // tm, n // tn),
        in_specs=[pl.BlockSpec((tm, d), lambda i, j: (i, 0)),
                  _vector_spec(d, lambda i, j: (0, 0)),
                  pl.BlockSpec((d, tn), lambda i, j: (0, j))],
        out_specs=pl.BlockSpec((tm, tn), lambda i, j: (i, j)),
        scratch_shapes=[pltpu.VMEM((tm, d), BF16), _stat_scratch(tm)],
        compiler_params=_cp("parallel", "arbitrary"), name="norm_matmul")(x, _row_vector(gain), w)


def _matmul_norm_kernel(*refs, glu, tn):
    if glu:
        a_ref, w_ref, w2_ref, g_ref, o_ref, acc_ref, s_ref = refs
    else:
        a_ref, w_ref, g_ref, o_ref, acc_ref, s_ref = refs
    j = pl.program_id(1)
    a = a_ref[...]
    y = _dot(a, w_ref[...])
    if glu:
        y = y * _sigmoid(_dot(a, w2_ref[...]))
    acc_ref[:, pl.ds(pl.multiple_of(j * tn, tn), tn)] = y

    @pl.when(j == pl.num_programs(1) - 1)
    def _():
        _norm_rows_into(o_ref, _rows_of(acc_ref), g_ref, s_ref)


def _matmul_norm(a, ws, gain, tm=1024):
    m, k = a.shape
    n = ws[0].shape[1]
    glu = len(ws) == 2
    tm, tn = min(tm, m), min(256 if glu else 512, n)
    w_specs = [pl.BlockSpec((k, tn), lambda i, j: (0, j)) for _ in ws]
    return pl.pallas_call(
        functools.partial(_matmul_norm_kernel, glu=glu, tn=tn),
        out_shape=jax.ShapeDtypeStruct((m, n), BF16),
        grid=(m // tm, n // tn),
        in_specs=[pl.BlockSpec((tm, k), lambda i, j: (i, 0))] + w_specs + [
            _vector_spec(n, lambda i, j: (0, 0))],
        out_specs=pl.BlockSpec((tm, n), lambda i, j: (i, 0)),
        scratch_shapes=[pltpu.VMEM((tm, n), F32), _stat_scratch(tm)],
        compiler_params=_cp("parallel", "arbitrary"),
        name="glu_norm" if glu else "matmul_norm")(a, *ws, _row_vector(gain))


def _xa_kernel(x_ref, br_ref, g_in_ref, wq_ref, k_ref, v_ref, wo_ref, g_out_ref, g_next_ref, o_ref, hn_out_ref,
               hn_ref, x1_ref, s_ref):
    def add_branch(rs):
        x1_ref[rs, :] = x_ref[rs, :] + br_ref[rs, :].astype(F32)
        return x1_ref[rs, :]

    _row_scales_into(s_ref, add_branch, x_ref.shape[1])
    x1 = _rows_of(x1_ref)

    def normed(rs, c):
        hn_ref[rs, :] = _scaled(x1(rs), s_ref[rs, :], g_in_ref).astype(hn_ref.dtype)
        return c
    _for_row_blocks(hn_ref.shape[0], normed, unroll=2)
    q = _dot(hn_ref[...], wq_ref[...]) * (XA_HEAD_DIM ** -0.5)
    heads = []
    for h in range(XA_HEADS):
        sl = slice(h * XA_HEAD_DIM, (h + 1) * XA_HEAD_DIM)
        s = _dot_nt(q[:, sl].astype(BF16), k_ref[0, :, sl])
        s = s - jnp.max(s, axis=-1, keepdims=True)
        p = jnp.exp(s)
        p = p / jnp.sum(p, axis=-1, keepdims=True)
        heads.append(_dot(p.astype(BF16), v_ref[0, :, sl]))
    o = jnp.concatenate(heads, axis=-1).astype(BF16)
    o_ref[...] = _dot(o, wo_ref[...])
    _add_norm_rows(o_ref, _rows_of(o_ref), x1, g_out_ref, s_ref, hn_out_ref, g_next_ref)


def _cross_attention(x, branch, g_in, wq, mem_kv, wo, g_out, g_next, seq, tm=256):
    m, d = x.shape
    xw = wq.shape[1]
    n_mem = mem_kv.shape[1]
    tm = min(tm, seq)
    tiles_per_seq = seq // tm
    return pl.pallas_call(
        _xa_kernel, out_shape=[jax.ShapeDtypeStruct((m, d), F32), jax.ShapeDtypeStruct((m, d), BF16)],
        grid=(m // tm,),
        in_specs=[pl.BlockSpec((tm, d), lambda i: (i, 0)),
                  pl.BlockSpec((tm, d), lambda i: (i, 0)),
                  _vector_spec(d, lambda i: (0, 0)),
                  pl.BlockSpec((d, xw), lambda i: (0, 0)),
                  pl.BlockSpec((1, n_mem, xw), lambda i: (i // tiles_per_seq, 0, 0)),
                  pl.BlockSpec((1, n_mem, xw), lambda i: (i // tiles_per_seq, 0, 1)),
                  pl.BlockSpec((xw, d), lambda i: (0, 0)),
                  _vector_spec(d, lambda i: (0, 0)),
                  _vector_spec(d, lambda i: (0, 0))],
        out_specs=[pl.BlockSpec((tm, d), lambda i: (i, 0)), pl.BlockSpec((tm, d), lambda i: (i, 0))],
        scratch_shapes=[pltpu.VMEM((tm, d), BF16), pltpu.VMEM((tm, d), F32), _stat_scratch(tm)],
        compiler_params=_cp("parallel"), name="cross_attention")(
            x, branch, _row_vector(g_in), wq, mem_kv, mem_kv, wo, _row_vector(g_out), _row_vector(g_next))


def _matmul_kernel(a_ref, w_ref, o_ref, *, act, col0, w_cols, w_is_transposed):
    w = w_ref[...]
    if w_cols is not None:
        tn = o_ref.shape[1]
        shape, axis = ((tn, 1), 0) if w_is_transposed else ((1, tn), 1)
        cols = col0 + pl.program_id(1) * tn + lax.broadcasted_iota(jnp.int32, shape, axis)
        w = jnp.where(cols < w_cols, w, 0.0)
    w = w.astype(BF16)
    y = _dot_nt(a_ref[...], w) if w_is_transposed else _dot(a_ref[...], w)
    if act == "relu2":
        y = jnp.maximum(y, 0.0)
        y = y * y
    o_ref[...] = y.astype(o_ref.dtype)


def _layer_weight_spec(w, layer, block, index_map):
    if w.ndim == 2:
        return pl.BlockSpec(block, index_map)
    return pl.BlockSpec((None,) + block, lambda *idx: (layer,) + index_map(*idx))


def _matmul(a, w, out_dtype, act=None, layer=None, col0=0, n_out=None, w_is_transposed=False, tm=2048, tn=512):
    m, k = a.shape
    w_cols = w.shape[-2] if w_is_transposed else w.shape[-1]
    n = w_cols if n_out is None else n_out
    tm, tn = min(tm, m), min(tn, n)
    assert col0 % tn == 0
    ragged = col0 + n > w_cols
    block, index_map = (((tn, k), lambda i, j: (col0 // tn + j, 0)) if w_is_transposed
                        else ((k, tn), lambda i, j: (0, col0 // tn + j)))
    return pl.pallas_call(
        functools.partial(_matmul_kernel, act=act, col0=col0, w_cols=w_cols if ragged else None,
                          w_is_transposed=w_is_transposed),
        out_shape=jax.ShapeDtypeStruct((m, n), out_dtype),
        grid=(m // tm, n // tn),
        in_specs=[pl.BlockSpec((tm, k), lambda i, j: (i, 0), pipeline_mode=pl.Buffered(1)),
                  _layer_weight_spec(w, layer, block, index_map)],
        out_specs=pl.BlockSpec((tm, tn), lambda i, j: (i, j)),
        compiler_params=_cp("parallel", "arbitrary"), name="matmul")(a, w)


def _matmul_acc_kernel(a_ref, w_ref, o_ref, acc_ref, *, ts):
    @pl.when(pl.program_id(2) == 0)
    def _():
        acc_ref[...] = jnp.zeros_like(acc_ref)

    a = a_ref[...]
    for n in range(acc_ref.shape[1] // ts):
        sl = slice(n * ts, (n + 1) * ts)
        acc_ref[:, sl] += _dot(a, w_ref[:, sl].astype(BF16))

    @pl.when(pl.program_id(2) == pl.num_programs(2) - 1)
    def _():
        o_ref[...] = acc_ref[...].astype(o_ref.dtype)


def _matmul_acc(a, w, layer=None, tm=2048, tn=1024, tk=2048):
    m, k = a.shape
    n = w.shape[-1]
    tm, tn, tk = min(tm, m), min(tn, n), min(tk, k)
    return pl.pallas_call(
        functools.partial(_matmul_acc_kernel, ts=min(256, tn)),
        out_shape=jax.ShapeDtypeStruct((m, n), BF16),
        grid=(m // tm, n // tn, k // tk),
        in_specs=[pl.BlockSpec((tm, tk), lambda i, j, kk: (i, kk)),
                  _layer_weight_spec(w, layer, (tk, tn), lambda i, j, kk: (kk, j))],
        out_specs=pl.BlockSpec((tm, tn), lambda i, j, kk: (i, j)),
        scratch_shapes=[pltpu.VMEM((tm, tn), F32)],
        compiler_params=_cp("parallel", "parallel", "arbitrary"), name="matmul_acc")(a, w)


def _add_norm_kernel(x_ref, m_ref, gp_ref, gn_ref, o_ref, *rest):
    s_ref = rest[-1]
    hn_ref = rest[0] if len(rest) == 2 else None
    _add_norm_rows(o_ref, lambda rs: m_ref[rs, :].astype(F32), _rows_of(x_ref), gp_ref, s_ref, hn_ref, gn_ref)


def _add_norm(x, m_branch, g_post, g_next, tm=256):
    m, d = x.shape
    tm = min(tm, m)
    row = pl.BlockSpec((tm, d), lambda i: (i, 0))
    vec = _vector_spec(d, lambda i: (0, 0))
    emit_hn = g_next is not None
    out_shape = [jax.ShapeDtypeStruct((m, d), F32)] + ([jax.ShapeDtypeStruct((m, d), BF16)] if emit_hn else [])
    outs = pl.pallas_call(
        _add_norm_kernel, out_shape=out_shape, grid=(m // tm,),
        in_specs=[row, row, vec, vec], out_specs=[row] * len(out_shape),
        scratch_shapes=[_stat_scratch(tm)],
        compiler_params=_cp("parallel"), name="add_norm")(
            x, m_branch, _row_vector(g_post), _row_vector(g_next if emit_hn else g_post))
    return (outs[0], outs[1]) if emit_hn else (outs[0], None)


def _gla_kernel(q_ref, k_ref, v_ref, g_ref, gk_ref, wup_ref, bup_ref, onorm_ref, tril_ref,
                o_ref, st_ref):
    @pl.when(pl.program_id(1) == 0)
    def _():
        st_ref[...] = jnp.zeros_like(st_ref)

    nb, _, dk = q_ref.shape
    bs = range(nb)
    gk = gk_ref[...].reshape(nb * CHUNK, gk_ref.shape[2])
    z = _dot_split(gk, wup_ref[...]) + bup_ref[...]
    log_alpha = (jnp.minimum(z, 0.0) - jnp.log1p(jnp.exp(-jnp.abs(z)))) / GLA_GATE_NORMALIZER
    cum_all = _dot_ones(tril_ref[...], log_alpha)
    cum = [cum_all[b * CHUNK:(b + 1) * CHUNK] for b in bs]
    cum_last = [cum[b][CHUNK - 1:CHUNK, :] for b in bs]
    k_dec = [(k_ref[b].astype(F32) * jnp.exp(cum_last[b] - cum[b])).astype(BF16) for b in bs]
    st = [st_ref[b] * jnp.exp(cum_last[b]) + _dot_tn(v_ref[b], k_dec[b]) for b in bs]
    for b in bs:
        st_ref[b] = st[b]
    q = [(q_ref[b].astype(F32) * dk ** -0.5).astype(BF16) for b in bs]
    o = [_dot_nt(q[b], st[b].astype(BF16)) for b in bs]
    for b in bs:
        g = g_ref[b].astype(F32)
        o_ref[b] = (_rms(o[b], onorm_ref[...]) * (g * _sigmoid(g))).astype(o_ref.dtype)


def _gla_scan(proj, gk_lo, w_up, b_up, o_norm, bsz, seq):
    dk_all = w_up.shape[1]
    dkh = dk_all // GLA_HEADS
    dv_all = (proj.shape[1] - 2 * dk_all) // 2
    dvh = dv_all // GLA_HEADS
    nc = seq // CHUNK
    rank_pad = gk_lo.shape[1]
    proj = proj.reshape(bsz, seq, proj.shape[1])
    gk_lo = gk_lo.reshape(bsz, seq, rank_pad)
    tril = jnp.kron(jnp.eye(bsz, dtype=F32), jnp.tril(jnp.ones((CHUNK, CHUNK), F32))).astype(BF16)
    k_off = dk_all // dkh
    v_off = 2 * dk_all // dvh
    g_off = v_off + dv_all // dvh
    out = pl.pallas_call(
        _gla_kernel, out_shape=jax.ShapeDtypeStruct((bsz, seq, dv_all), BF16),
        grid=(GLA_HEADS, nc),
        in_specs=[pl.BlockSpec((bsz, CHUNK, dkh), lambda h, c: (0, c, h)),
                  pl.BlockSpec((bsz, CHUNK, dkh), lambda h, c: (0, c, k_off + h)),
                  pl.BlockSpec((bsz, CHUNK, dvh), lambda h, c: (0, c, v_off + h)),
                  pl.BlockSpec((bsz, CHUNK, dvh), lambda h, c: (0, c, g_off + h)),
                  pl.BlockSpec((bsz, CHUNK, rank_pad), lambda h, c: (0, c, 0)),
                  pl.BlockSpec((rank_pad, dkh), lambda h, c: (0, h)),
                  pl.BlockSpec((1, dkh), lambda h, c: (0, h)),
                  pl.BlockSpec((1, dvh), lambda h, c: (0, 0)),
                  pl.BlockSpec((bsz * CHUNK, bsz * CHUNK), lambda h, c: (0, 0))],
        out_specs=pl.BlockSpec((bsz, CHUNK, dvh), lambda h, c: (0, c, h)),
        scratch_shapes=[pltpu.VMEM((bsz, dvh, dkh), F32)],
        compiler_params=_cp("parallel", "arbitrary"), name="gla_scan")(
            proj, proj, proj, proj, gk_lo, w_up, b_up.reshape(1, dk_all), o_norm.reshape(1, dvh), tril)
    return out.reshape(bsz * seq, dv_all)


def _shiftmix_matmul_kernel(hn_ref, prev_ref, mix_ref, w_ref, o_ref, xm_ref, *, tiles_per_seq):
    @pl.when(pl.program_id(2) == 0)
    def _():
        prev_rows = prev_ref.shape[0]
        last = prev_ref[...].astype(F32)[prev_rows - 1:prev_rows, :]
        first_tile = pl.program_id(0) % tiles_per_seq == 0
        last = jnp.where(first_tile, 0.0, last)

        def body(rs, last):
            hn = hn_ref[rs, :].astype(F32)
            rows = lax.broadcasted_iota(jnp.int32, hn.shape, 0)
            shifted = jnp.where(rows == 0, last, pltpu.roll(hn, 1, axis=0))
            xm_ref[rs, :] = (hn + (shifted - hn) * mix_ref[0]).astype(BF16)
            return hn[hn.shape[0] - 1:, :]

        _for_row_blocks(hn_ref.shape[0], body, last)

    o_ref[0] = _dot(xm_ref[...], w_ref[0]).astype(o_ref.dtype)


def _shiftmix_matmul(hn, mix, w, out_dtype, seq, tm=2048, tn=512):
    m, d = hn.shape
    p_cnt, _, n = w.shape
    tm, tn = min(tm, seq), min(tn, n)
    prev_rows = 2 * SUBLANES
    blk = tm // prev_rows
    return pl.pallas_call(
        functools.partial(_shiftmix_matmul_kernel, tiles_per_seq=seq // tm),
        out_shape=jax.ShapeDtypeStruct((p_cnt, m, n), out_dtype),
        grid=(m // tm, p_cnt, n // tn),
        in_specs=[pl.BlockSpec((tm, d), lambda i, p, j: (i, 0), pipeline_mode=pl.Buffered(1)),
                  pl.BlockSpec((prev_rows, d), lambda i, p, j: (jnp.maximum(i * blk - 1, 0), 0)),
                  pl.BlockSpec((1, GAIN_ROWS, d), lambda i, p, j: (p, 0, 0)),
                  pl.BlockSpec((1, d, tn), lambda i, p, j: (p, 0, j))],
        out_specs=pl.BlockSpec((1, tm, tn), lambda i, p, j: (p, i, j)),
        scratch_shapes=[pltpu.VMEM((tm, d), BF16)],
        compiler_params=_cp("parallel", "arbitrary", "arbitrary"), name="shiftmix_matmul")(
            hn, hn, jnp.broadcast_to(mix.reshape(p_cnt, 1, d), (p_cnt, GAIN_ROWS, d)), w)


def _lowrank_out_kernel(h_ref, w_ref, b_ref, o_ref):
    p = pl.program_id(0)
    h = h_ref[0]
    act = jnp.where(p == 0, jnp.tanh(h), jnp.where(p == 1, h, _sigmoid(h)))
    o_ref[0] = _dot(act.astype(BF16), w_ref[0]) + b_ref[0]


def _lowrank_out(h, w2, bias, tm=512):
    p_cnt, m, r = h.shape
    d = w2.shape[2]
    tm = min(tm, m)
    return pl.pallas_call(
        _lowrank_out_kernel, out_shape=jax.ShapeDtypeStruct((p_cnt, m, d), F32),
        grid=(p_cnt, m // tm),
        in_specs=[pl.BlockSpec((1, tm, r), lambda p, i: (p, i, 0)),
                  pl.BlockSpec((1, r, d), lambda p, i: (p, 0, 0)),
                  pl.BlockSpec((1, 1, d), lambda p, i: (p, 0, 0))],
        out_specs=pl.BlockSpec((1, tm, d), lambda p, i: (p, i, 0)),
        compiler_params=_cp("arbitrary", "arbitrary"), name="lowrank_out")(h, w2, bias)


def _rwkv_kernel(rkv_ref, lr_ref, par_ref, bd_ref, tril_ref, o_ref, st_ref):
    @pl.when(pl.program_id(2) == 0)
    def _():
        st_ref[...] = jnp.zeros_like(st_ref)

    n = RWKV_HEAD
    heads = 2 * st_ref.shape[0]
    r = rkv_ref[0].astype(F32)
    k = rkv_ref[1].astype(F32)
    v = rkv_ref[2].astype(F32)
    w_log = -_softplus(-lr_ref[0]) - 0.5
    log_w = -jnp.exp(w_log)
    a = _sigmoid(lr_ref[1])
    gate = lr_ref[2]
    k_k, k_a, r_k = par_ref[0:1, :], par_ref[1:2, :], par_ref[2:3, :]
    lnx_g, lnx_b = par_ref[3:4, :], par_ref[4:5, :]
    bd = bd_ref[...]
    grp = bd.shape[0]

    def head_sums(x):
        x = x.astype(BF16)
        return jnp.concatenate([_dot(x[:, i:i + grp], bd) for i in range(0, x.shape[1], grp)], axis=1)

    kk = k * k_k
    kk = kk / jnp.maximum(jnp.sqrt(head_sums(kk * kk)), 1e-12)
    k2 = k * (1.0 + (a - 1.0) * k_a)
    cw = _dot_ones(tril_ref[...], log_w)
    cw_last = cw[CHUNK - 1:CHUNK, :]
    e_neg = jnp.exp(-cw)
    e_end = jnp.exp(cw_last - cw)
    a_t = (-kk * jnp.exp(cw - log_w)).astype(BF16)
    r_t = (r * jnp.exp(cw)).astype(BF16)
    b_vec = kk * a
    b_t = (b_vec * e_neg).astype(BF16)
    k_t = (k2 * e_neg).astype(BF16)
    b_w = (b_vec * e_end).astype(BF16)
    k_w = (k2 * e_end).astype(BF16)
    w_end = jnp.exp(cw_last)
    v_b = v.astype(BF16)

    pw = 2 * n
    rows = lax.broadcasted_iota(jnp.int32, (CHUNK, 2 * pw), 0)
    cols = lax.broadcasted_iota(jnp.int32, (CHUNK, 2 * pw), 1) % CHUNK
    strict = rows > cols
    incl = rows >= cols
    t_row = lax.broadcasted_iota(jnp.int32, (CHUNK, pw), 0)
    t_col = lax.broadcasted_iota(jnp.int32, (CHUNK, pw), 1) % CHUNK
    eye = (t_row == t_col).astype(F32)

    def pair_diag(x):
        first = lax.broadcasted_iota(jnp.int32, x.shape, 1) < n
        zero = jnp.zeros_like(x)
        return jnp.concatenate([jnp.where(first, x, zero), jnp.where(first, zero, x)], axis=0)

    def lower_left(b):
        return (t_row // (2 * b) == t_col // (2 * b)) & (t_row % (2 * b) >= b) & (t_col % (2 * b) < b)

    ps = range(heads // 2)
    sls = [slice(p * pw, (p + 1) * pw) for p in ps]
    ar = [jnp.concatenate([a_t[:, sl], r_t[:, sl]], axis=0) for sl in sls]
    bk = [jnp.concatenate([pair_diag(b_t[:, sl]), pair_diag(k_t[:, sl])], axis=0) for sl in sls]
    s0 = [st_ref[p] for p in ps]
    gram = [_dot_nt(ar[p], bk[p]) for p in ps]
    proj = [_dot_nt(ar[p], s0[p].astype(BF16)) for p in ps]
    v_d = [pair_diag(v_b[:, sl]) for sl in sls]
    low = [jnp.where(strict, gram[p][:CHUNK], 0.0).astype(BF16) for p in ps]
    nil = [low[p][:, :pw] for p in ps]
    t_inv = [eye + jnp.where(lower_left(1), nil[p], 0).astype(F32) for p in ps]
    b = 2
    while b < CHUNK:
        mask = lower_left(b)
        off = [jnp.where(mask, nil[p], 0) for p in ps]
        t_b = [t_inv[p].astype(BF16) for p in ps]
        right = [_dot(off[p], pair_diag(t_b[p])).astype(BF16) for p in ps]
        t_inv = [t_inv[p] + _dot(t_b[p], pair_diag(right[p])) for p in ps]
        b *= 2
    rhs = [proj[p][:CHUNK] + _dot(low[p][:, pw:], v_d[p]) for p in ps]
    u = [_dot(t_inv[p].astype(BF16), pair_diag(rhs[p].astype(BF16))).astype(BF16) for p in ps]
    upper = [jnp.where(incl, gram[p][CHUNK:], 0.0).astype(BF16) for p in ps]
    y_pairs = [proj[p][CHUNK:] + _dot(upper[p], jnp.concatenate([pair_diag(u[p]), v_d[p]], axis=0)) for p in ps]
    pair_rows = lax.broadcasted_iota(jnp.int32, (pw, pw), 0) // n
    pair_cols = lax.broadcasted_iota(jnp.int32, (pw, pw), 1) // n
    for p in ps:
        uv = jnp.concatenate([u[p], v_b[:, sls[p]]], axis=0)
        bkw = jnp.concatenate([b_w[:, sls[p]], k_w[:, sls[p]]], axis=0)
        grown = s0[p] * w_end[:, sls[p]] + _dot_tn(uv, bkw)
        st_ref[p] = jnp.where(pair_rows == pair_cols, grown, 0.0)

    y = jnp.concatenate(y_pairs, axis=1)
    inv_n = 1.0 / n
    mu = head_sums(y) * inv_n
    yc = y - mu
    var = head_sums(yc * yc) * inv_n
    y = yc * lax.rsqrt(var + RWKV_LNX_EPS) * lnx_g + lnx_b
    bonus = head_sums(r * k2 * r_k) * v
    o_ref[...] = ((y + bonus) * gate).astype(o_ref.dtype)


def _rwkv_scan(rkv, lr, params, bsz, seq, heads_per_step=32):
    _, m, d = rkv.shape
    n_heads = d // RWKV_HEAD
    hg = min(heads_per_step, n_heads)
    w = hg * RWKV_HEAD
    nc = seq // CHUNK
    grp = min(2 * LANES, w)
    lane_head = jnp.arange(grp) // RWKV_HEAD
    bd = (lane_head[:, None] == lane_head[None, :]).astype(BF16)
    tril = jnp.tril(jnp.ones((CHUNK, CHUNK), BF16))
    n_par = params.shape[0]
    return pl.pallas_call(
        _rwkv_kernel, out_shape=jax.ShapeDtypeStruct((m, d), BF16),
        grid=(bsz, n_heads // hg, nc),
        in_specs=[pl.BlockSpec((3, CHUNK, w), lambda b, g, c: (0, b * nc + c, g)),
                  pl.BlockSpec((3, CHUNK, w), lambda b, g, c: (0, b * nc + c, g)),
                  pl.BlockSpec((n_par, w), lambda b, g, c: (0, g)),
                  pl.BlockSpec((grp, grp), lambda b, g, c: (0, 0)),
                  pl.BlockSpec((CHUNK, CHUNK), lambda b, g, c: (0, 0))],
        out_specs=pl.BlockSpec((CHUNK, w), lambda b, g, c: (b * nc + c, g)),
        scratch_shapes=[pltpu.VMEM((hg // 2, 2 * RWKV_HEAD, 2 * RWKV_HEAD), F32)],
        compiler_params=_cp("parallel", "parallel", "arbitrary"), name="rwkv_scan")(
            rkv, lr, params, bd, tril)


def _cmul_add(acc_r, acc_i, cr, ci, xr, xi):
    return acc_r + cr * xr - ci * xi, acc_i + cr * xi + ci * xr


def _s5_kernel(h_ref, wb_ref, wc_ref, coef_ref, pow_ref, d_ref, o_ref, x_ref, carry_ref):
    @pl.when(pl.program_id(2) == 0)
    def _():
        carry_ref[...] = jnp.zeros_like(carry_ref)

    tt = h_ref.shape[0]
    ns = carry_ref.shape[2]
    seg = tt // SUBLANES
    h32 = h_ref[...].astype(F32).reshape(SUBLANES, seg, h_ref.shape[1])
    h = jnp.swapaxes(h32, 0, 1).reshape(tt, h_ref.shape[1]).astype(BF16)
    x_ref[...] = _dot(h, wb_ref[0])
    a_r, a_i = coef_ref[0, 8], coef_ref[0, 9]

    def step_rows(k, c_r, c_i, x_r, x_i):
        rows = pl.ds(pl.multiple_of(k * SUBLANES, SUBLANES), SUBLANES)
        xr, xi = _cmul_add(x_ref[rows, 0:ns], x_ref[rows, ns:2 * ns], c_r, c_i, x_r, x_i)
        x_ref[rows, 0:ns] = xr
        x_ref[rows, ns:2 * ns] = xi
        return xr, xi

    zeros = jnp.zeros((SUBLANES, ns), F32)
    end_r, end_i = lax.fori_loop(0, seg, lambda k, c: step_rows(k, a_r, a_i, *c), (zeros, zeros))

    for s_idx, shift in enumerate((1, 2, 4)):
        end_r, end_i = _cmul_add(end_r, end_i, coef_ref[0, 2 * s_idx], coef_ref[0, 2 * s_idx + 1],
                                 pltpu.roll(end_r, shift, axis=0), pltpu.roll(end_i, shift, axis=0))
    in_r = jnp.broadcast_to(carry_ref[0], (SUBLANES, ns))
    in_i = jnp.broadcast_to(carry_ref[1], (SUBLANES, ns))
    end_r, end_i = _cmul_add(end_r, end_i, coef_ref[0, 6], coef_ref[0, 7], in_r, in_i)
    carry_ref[0] = end_r[SUBLANES - 1:SUBLANES, :]
    carry_ref[1] = end_i[SUBLANES - 1:SUBLANES, :]
    first = lax.broadcasted_iota(jnp.int32, (SUBLANES, ns), 0) == 0
    in_r = jnp.where(first, in_r, pltpu.roll(end_r, 1, axis=0))
    in_i = jnp.where(first, in_i, pltpu.roll(end_i, 1, axis=0))

    def correct(k, c):
        p_r = jnp.broadcast_to(pow_ref[0, 0, pl.ds(k, 1), :], (SUBLANES, ns))
        p_i = jnp.broadcast_to(pow_ref[0, 1, pl.ds(k, 1), :], (SUBLANES, ns))
        step_rows(k, p_r, p_i, in_r, in_i)
        return c

    lax.fori_loop(0, seg, correct, 0, unroll=2)

    y = _dot(x_ref[...].astype(BF16), wc_ref[0]) + d_ref[...] * h.astype(F32)
    y = 0.5 * y * (1.0 + lax.erf(y * (2.0 ** -0.5)))
    y = jnp.swapaxes(y.reshape(seg, SUBLANES, y.shape[1]), 0, 1)
    o_ref[...] = y.reshape(tt, o_ref.shape[1]).astype(o_ref.dtype)


def _s5_scan(hn, lam_re, lam_im, log_dt, b_re, b_im, c_re, c_im, d_skip, bsz, seq, tt=1024):
    m, d = hn.shape
    n_groups, n_state, grp = b_re.shape
    sg = min(S5_SLAB_GROUPS, n_groups)
    slabs = n_groups // sg
    ch = sg * grp
    ns = sg * n_state
    tt = min(tt, seq)

    lr = jnp.minimum(lam_re.astype(F32), -1e-4)
    li = lam_im.astype(F32)
    dt = jnp.exp(log_dt.astype(F32))[:, None]
    mag = jnp.exp(lr * dt)
    ab_re = mag * jnp.cos(li * dt)
    ab_im = mag * jnp.sin(li * dt)
    den = lr * lr + li * li
    nr, ni = ab_re - 1.0, ab_im
    z_re = (nr * lr + ni * li) / den
    z_im = (ni * lr - nr * li) / den
    bb_re = z_re[..., None] * b_re - z_im[..., None] * b_im
    bb_im = z_re[..., None] * b_im + z_im[..., None] * b_re

    eye = jnp.eye(sg, dtype=F32)

    def block_diag_in(bb):
        t = bb.reshape(slabs, sg, n_state, grp)
        return jnp.einsum('sgph,gk->sghkp', t, eye).reshape(slabs, ch, ns)

    def block_diag_out(cc):
        t = cc.reshape(slabs, sg, grp, n_state)
        return jnp.einsum('sgqp,gk->sgpkq', t, eye).reshape(slabs, ns, ch)

    w_b = jnp.concatenate([block_diag_in(bb_re), block_diag_in(bb_im)], axis=2).astype(BF16)
    w_c = jnp.concatenate([block_diag_out(c_re.astype(F32)), -block_diag_out(c_im.astype(F32))],
                          axis=1).astype(BF16)

    def powers(base_re, base_im, n):
        re, im = base_re[None], base_im[None]
        while re.shape[0] < n:
            top_re, top_im = re[-1:], im[-1:]
            re, im = (jnp.concatenate([re, re * top_re - im * top_im]),
                      jnp.concatenate([im, re * top_im + im * top_re]))
        return re[:n].reshape(n, slabs, ns), im[:n].reshape(n, slabs, ns)

    seg = tt // SUBLANES
    pw_re, pw_im = powers(ab_re, ab_im, seg)
    sg_re, sg_im = powers(pw_re[-1], pw_im[-1], SUBLANES)
    row = jnp.arange(SUBLANES)
    coefs = []
    for shift in (1, 2, 4):
        mask = (row >= shift)[:, None, None]
        coefs += [jnp.where(mask, sg_re[shift - 1][None], 0.0), jnp.where(mask, sg_im[shift - 1][None], 0.0)]
    coefs += [sg_re, sg_im]
    coefs += [jnp.broadcast_to(pw_re[:1], sg_re.shape), jnp.broadcast_to(pw_im[:1], sg_im.shape)]
    coef = jnp.stack(coefs, axis=0).transpose(2, 0, 1, 3)
    pw = jnp.stack([pw_re, pw_im], axis=0).transpose(2, 0, 1, 3)

    nt = seq // tt
    return pl.pallas_call(
        _s5_kernel, out_shape=jax.ShapeDtypeStruct((m, d), BF16),
        grid=(bsz, slabs, nt),
        in_specs=[pl.BlockSpec((tt, ch), lambda b, s, t: (b * nt + t, s)),
                  pl.BlockSpec((1, ch, 2 * ns), lambda b, s, t: (s, 0, 0)),
                  pl.BlockSpec((1, 2 * ns, ch), lambda b, s, t: (s, 0, 0)),
                  pl.BlockSpec((1, 10, SUBLANES, ns), lambda b, s, t: (s, 0, 0, 0)),
                  pl.BlockSpec((1, 2, seg, ns), lambda b, s, t: (s, 0, 0, 0)),
                  pl.BlockSpec((1, ch), lambda b, s, t: (0, s))],
        out_specs=pl.BlockSpec((tt, ch), lambda b, s, t: (b * nt + t, s)),
        scratch_shapes=[pltpu.VMEM((tt, 2 * ns), F32), pltpu.VMEM((2, 1, ns), F32)],
        compiler_params=_cp("parallel", "parallel", "arbitrary"), name="s5_scan")(
            hn, w_b, w_c, coef, pw, d_skip.reshape(1, d))


def _pad_to(x, axis, size):
    pad = size - x.shape[axis]
    if pad == 0:
        return x
    widths = [(0, 0)] * x.ndim
    widths[axis] = (0, pad)
    return jnp.pad(x, widths)


def _round_up(n, k):
    return (n + k - 1) // k * k


def _gla_layer(hn, g_post, w_in_all, layer, w_gk_up, b_gk_up, o_norm, w_out, bsz, seq):
    rank = w_gk_up.shape[0]
    n_main = w_in_all.shape[2] - rank
    rank_pad = _round_up(rank, LANES)
    w_in_t = jnp.swapaxes(w_in_all, 1, 2)
    proj = _matmul(hn, w_in_t, BF16, layer=layer, n_out=n_main, w_is_transposed=True)
    gk_lo = _matmul(hn, w_in_t, F32, layer=layer, col0=n_main, n_out=rank_pad, w_is_transposed=True)
    o = _gla_scan(proj, gk_lo, _pad_to(w_gk_up, 0, rank_pad), b_gk_up, o_norm, bsz, seq)
    return _matmul_norm(o, [w_out.astype(BF16)], g_post)


def _rwkv_layer(hn, g_post, mix, w_r, w_k, w_v, w_o, w0, w_w1, w_w2, a0, w_a1, w_a2,
                w_g1, w_g2, k_k, k_a, r_k, lnx_g, lnx_b, bsz, seq):
    d = hn.shape[1]
    rkv = _shiftmix_matmul(hn, jnp.stack([mix[0], mix[2], mix[3]]),
                           jnp.stack([w_r, w_k, w_v]).astype(BF16), BF16, seq)
    rank_pad = _round_up(max(w_w1.shape[1], w_a1.shape[1], w_g1.shape[1]), LANES)
    w1 = jnp.stack([_pad_to(w, 1, rank_pad) for w in (w_w1, w_a1, w_g1)]).astype(BF16)
    w2 = jnp.stack([_pad_to(w, 0, rank_pad) for w in (w_w2, w_a2, w_g2)]).astype(BF16)
    low = _shiftmix_matmul(hn, jnp.stack([mix[1], mix[4], mix[5]]), w1, F32, seq)
    bias = jnp.stack([w0, a0, jnp.zeros_like(w0)]).reshape(3, 1, d)
    lr = _lowrank_out(low, w2, bias)
    params = _pad_to(jnp.stack([k_k, k_a, r_k, lnx_g, lnx_b]), 0, SUBLANES)
    y = _rwkv_scan(rkv, lr, params, bsz, seq)
    return _matmul_norm(y, [w_o.astype(BF16)], g_post)


def _s5_layer(hn, g_post, lam_re, lam_im, log_dt, b_re, b_im, c_re, c_im, d_skip,
              w_glu1, w_glu2, bsz, seq):
    y = _s5_scan(hn, lam_re, lam_im, log_dt, b_re, b_im, c_re, c_im, d_skip, bsz, seq)
    return _matmul_norm(y, [w_glu1.astype(BF16), w_glu2.astype(BF16)], g_post)


def kernel(x, mem, norm_gains, mem_norm, mem_w_kv, xa_wq, xa_wo, mlp_w1, mlp_w2, gla_w_in, gla_w_gk_up, gla_b_gk_up, gla_o_norm, gla_w_out, rwkv_mix, rwkv_w_r, rwkv_w_k, rwkv_w_v, rwkv_w_o, rwkv_w0, rwkv_w_w1, rwkv_w_w2, rwkv_a0, rwkv_w_a1, rwkv_w_a2, rwkv_w_g1, rwkv_w_g2, rwkv_k_k, rwkv_k_a, rwkv_r_k, rwkv_lnx_g, rwkv_lnx_b, s5_lam_re, s5_lam_im, s5_log_dt, s5_b_re, s5_b_im, s5_c_re, s5_c_im, s5_d, s5_w_glu1, s5_w_glu2):
    bsz, seq, d = x.shape
    n_mem = mem.shape[1]
    depth = norm_gains.shape[0]
    assert seq % CHUNK == 0 and xa_wq.shape[2] == XA_HEADS * XA_HEAD_DIM

    mem_kv = _norm_matmul(mem.reshape(bsz * n_mem, d), mem_norm, mem_w_kv.astype(BF16), BF16)
    mem_kv = mem_kv.reshape(bsz, n_mem, mem_w_kv.shape[1])

    x = x.reshape(bsz * seq, d)
    hn = _norm(x, norm_gains[0, 0], BF16)
    for i in range(depth):
        kind, j = i % N_MIXERS, i // N_MIXERS
        g = norm_gains[i]
        if kind == 0:
            branch = _gla_layer(hn, g[1], gla_w_in, j, gla_w_gk_up[j], gla_b_gk_up[j],
                           gla_o_norm[j], gla_w_out[j], bsz, seq)
        elif kind == 1:
            branch = _rwkv_layer(hn, g[1], rwkv_mix[j], rwkv_w_r[j], rwkv_w_k[j], rwkv_w_v[j],
                            rwkv_w_o[j], rwkv_w0[j], rwkv_w_w1[j], rwkv_w_w2[j], rwkv_a0[j],
                            rwkv_w_a1[j], rwkv_w_a2[j], rwkv_w_g1[j], rwkv_w_g2[j], rwkv_k_k[j],
                            rwkv_k_a[j], rwkv_r_k[j], rwkv_lnx_g[j], rwkv_lnx_b[j], bsz, seq)
        else:
            branch = _s5_layer(hn, g[1], s5_lam_re[j], s5_lam_im[j], s5_log_dt[j], s5_b_re[j],
                          s5_b_im[j], s5_c_re[j], s5_c_im[j], s5_d[j], s5_w_glu1[j], s5_w_glu2[j],
                          bsz, seq)
        x, hn = _cross_attention(x, branch, g[2], xa_wq[i].astype(BF16), mem_kv, xa_wo[i].astype(BF16), g[3],
                                 g[4], seq)
        hidden = _matmul(hn, mlp_w1, BF16, act="relu2", layer=i)
        branch = _matmul_acc(hidden, mlp_w2, layer=i)
        x, hn = _add_norm(x, branch, g[5], norm_gains[i + 1, 0] if i + 1 < depth else None)
    return x.reshape(bsz, seq, d)
```
